```python
import jax
import jax.numpy as jnp
from jax import lax
import numpy as np

D_MODEL = 2048
BATCH = 2
SEQ = 16384
DEPTH = 1

EPS = 1e-6
HEAD_DIM = 128

GLA_HEADS = 4
GLA_DK = 64
GLA_DV = 128
GLA_GATE_RANK = 16
GLA_TAU = 16.0
GLA_CHUNK = 64

DIL_HEADS = 12
DIL_PATTERNS = ((128, 1), (512, 4), (2048, 16))
ROPE_THETA = 500000.0
ROPE_DIM = HEAD_DIM // 4
MASK_VALUE = -1e30

PEER_HEADS = 8
PEER_NKEYS = 128
PEER_NEXPERTS = PEER_NKEYS * PEER_NKEYS
PEER_QDIM = 256
PEER_TOPK = 16
PEER_TOKEN_BLOCK = 128

GLA_QK_W = GLA_HEADS * GLA_DK
GLA_V_W = GLA_HEADS * GLA_DV
DIL_W = DIL_HEADS * HEAD_DIM
MIX_W = GLA_V_W + DIL_W
IN_SPLITS = (GLA_QK_W, GLA_QK_W, GLA_V_W, GLA_GATE_RANK, GLA_GATE_RANK, GLA_V_W, DIL_W, DIL_W, DIL_W)
IN_W = sum(IN_SPLITS)

kernel_name = 'hybrid_gla_dilated_peer_encoder'


def rms_norm(x, g):
    xf = x.astype(jnp.float32)
    y = xf * lax.rsqrt(jnp.mean(xf * xf, axis=-1, keepdims=True) + EPS)
    return (y * g.astype(jnp.float32)).astype(x.dtype)


def partial_rope(t, positions):
    half = ROPE_DIM // 2
    inv_freq = ROPE_THETA ** (-jnp.arange(half, dtype=jnp.float32) / half)
    ang = positions.astype(jnp.float32)[:, None] * inv_freq[None, :]
    cos = jnp.cos(ang)[None, :, None, :]
    sin = jnp.sin(ang)[None, :, None, :]
    tf = t.astype(jnp.float32)
    x1 = tf[..., :half]
    x2 = tf[..., half:ROPE_DIM]
    return jnp.concatenate([x1 * cos - x2 * sin, x2 * cos + x1 * sin, tf[..., ROPE_DIM:]], axis=-1)


def gla_direction(q, k, v, log_a, include_diag):
    Bsz, H, S, dk = q.shape
    dv = v.shape[-1]
    C = GLA_CHUNK
    N = S // C
    q = q.reshape(Bsz, H, N, C, dk)
    k = k.reshape(Bsz, H, N, C, dk)
    v = v.reshape(Bsz, H, N, C, dv)
    b = jnp.cumsum(log_a.reshape(Bsz, H, N, C, dk), axis=3)
    b_last = b[:, :, :, -1:, :]
    q_in = q * jnp.exp(b)
    k_in = k * jnp.exp(-b)
    scores = jnp.einsum('bhnik,bhnjk->bhnij', q_in, k_in)
    mask = jnp.tril(jnp.ones((C, C), dtype=bool), k=0 if include_diag else -1)
    o_intra = jnp.einsum('bhnij,bhnjv->bhniv', jnp.where(mask, scores, 0.0), v)
    k_state = k * jnp.exp(b_last - b)
    inc = jnp.einsum('bhnjk,bhnjv->bhnkv', k_state, v)
    decay = jnp.exp(b_last[:, :, :, 0, :])

    def step(state, inp):
        dec, add = inp
        return dec[..., None] * state + add, state

    s0 = jnp.zeros((Bsz, H, dk, dv), jnp.float32)
    _, s_before = lax.scan(step, s0, (jnp.moveaxis(decay, 2, 0), jnp.moveaxis(inc, 2, 0)))
    s_before = jnp.moveaxis(s_before, 0, 2)
    o_inter = jnp.einsum('bhnik,bhnkv->bhniv', q_in, s_before)
    return (o_intra + o_inter).reshape(Bsz, H, S, dv)


def gla_mixer(q, k, v, gate_down_f, gate_down_b, r, up_f, bias_f, up_b, bias_b, out_g):
    Bsz, S, _ = q.shape
    f32 = jnp.float32

    def heads(t, d):
        return t.astype(f32).reshape(Bsz, S, GLA_HEADS, d).transpose(0, 2, 1, 3)

    qh = heads(q, GLA_DK) * (GLA_DK ** -0.5)
    kh = heads(k, GLA_DK)
    vh = heads(v, GLA_DV)
    la_f = heads(jax.nn.log_sigmoid(gate_down_f.astype(f32) @ up_f.astype(f32) + bias_f.astype(f32)) / GLA_TAU, GLA_DK)
    la_b = heads(jax.nn.log_sigmoid(gate_down_b.astype(f32) @ up_b.astype(f32) + bias_b.astype(f32)) / GLA_TAU, GLA_DK)

    def rev(t):
        return jnp.flip(t, axis=2)

    o_fwd = gla_direction(qh, kh, vh, la_f, True)
    o_bwd = rev(gla_direction(rev(qh), rev(kh), rev(vh), rev(la_b), False))
    o = (o_fwd + o_bwd).transpose(0, 2, 1, 3)
    o = rms_norm(o, out_g.reshape(GLA_HEADS, GLA_DV))
    return o.reshape(Bsz, S, GLA_V_W) * jax.nn.silu(r.astype(f32))


def dilated_window_attention(q, k, v, window, dilation):
    Bsz, S, H, hd = q.shape
    half = window // (2 * dilation)
    L = S // dilation
    nb = -(-L // half)
    Lp = nb * half

    def to_blocks(t):
        t = t.reshape(Bsz, L, dilation, H, hd)
        t = jnp.pad(t, ((0, 0), (0, Lp - L), (0, 0), (0, 0), (0, 0)))
        return t.reshape(Bsz, nb, half, dilation, H, hd)

    def with_neighbours(t):
        tp = jnp.pad(t, ((0, 0), (1, 1), (0, 0), (0, 0), (0, 0), (0, 0)))
        return jnp.concatenate([tp[:, :-2], tp[:, 1:-1], tp[:, 2:]], axis=2)

    qb = to_blocks(q)
    kn = with_neighbours(to_blocks(k))
    vn = with_neighbours(to_blocks(v))
    s = jnp.einsum('bnqrhe,bnkrhe->bnrhqk', qb, kn)
    qpos = jnp.arange(nb)[:, None] * half + jnp.arange(half)[None, :]
    kpos = jnp.arange(nb)[:, None] * half - half + jnp.arange(3 * half)[None, :]
    rel = kpos[:, None, :] - qpos[:, :, None]
    valid = (jnp.abs(rel) <= half) & (kpos[:, None, :] >= 0) & (kpos[:, None, :] < L)
    s = jnp.where(valid[None, :, None, None], s, MASK_VALUE)
    m = jnp.max(s, axis=-1)
    p = jnp.exp(s - m[..., None])
    den = jnp.sum(p, axis=-1)
    m_t = jnp.moveaxis(m, (2, 3, 4), (3, 4, 2))
    den_t = jnp.moveaxis(den, (2, 3, 4), (3, 4, 2))
    o = jnp.einsum('bnrhqk,bnkrhe->bnqrhe', p, vn) / den_t[..., None]

    def back(t):
        t = t.reshape((Bsz, Lp, dilation, H) + t.shape[5:])[:, :L]
        return t.reshape((Bsz, S, H) + t.shape[4:])

    return back(o), back(m_t), back(den_t)


def dilated_mixer(q, k, v, q_norm_g, k_norm_g, positions):
    Bsz, S, _ = q.shape
    shp = (Bsz, S, DIL_HEADS, HEAD_DIM)
    qh = partial_rope(rms_norm(q.reshape(shp), q_norm_g), positions) * (HEAD_DIM ** -0.5)
    kh = partial_rope(rms_norm(k.reshape(shp), k_norm_g), positions)
    vh = v.reshape(shp).astype(jnp.float32)
    outs, maxes, dens = [], [], []
    for window, dilation in DIL_PATTERNS:
        o, m, d = dilated_window_attention(qh, kh, vh, window, dilation)
        outs.append(o)
        maxes.append(m)
        dens.append(d)
    m_all = jnp.stack(maxes)
    w = jnp.stack(dens) * jnp.exp(m_all - jnp.max(m_all, axis=0, keepdims=True))
    w = w / jnp.sum(w, axis=0, keepdims=True)
    o = jnp.sum(w[..., None] * jnp.stack(outs), axis=0)
    return o.reshape(Bsz, S, DIL_W)


def peer_ffn(xn, w_query, sub_keys, expert_down, expert_up):
    Bsz, S, D = xn.shape
    f32 = jnp.float32
    T = Bsz * S
    xt = xn.reshape(T, D)
    qry = (xt @ w_query).reshape(T, PEER_HEADS, 2, PEER_QDIM // 2)
    sc = jnp.einsum('thcd,hckd->thck', qry.astype(f32), sub_keys.astype(f32))
    top_s, top_i = lax.top_k(sc, PEER_TOPK)
    kk = PEER_TOPK * PEER_TOPK
    cand_s = (top_s[:, :, 0, :, None] + top_s[:, :, 1, None, :]).reshape(T, PEER_HEADS, kk)
    cand_i = (top_i[:, :, 0, :, None] * PEER_NKEYS + top_i[:, :, 1, None, :]).reshape(T, PEER_HEADS, kk)
    best_s, best_pos = lax.top_k(cand_s, PEER_TOPK)
    idx = jnp.take_along_axis(cand_i, best_pos, axis=-1)
    gates = jax.nn.softmax(best_s, axis=-1)
    K = PEER_HEADS * PEER_TOPK
    nblk = T // PEER_TOKEN_BLOCK

    def block(args):
        xb, ib, gb = args
        u = expert_down[ib]
        act = jax.nn.gelu(jnp.einsum('tkd,td->tk', u, xb).astype(f32))
        vv = expert_up[ib]
        return jnp.einsum('tk,tkd->td', (gb * act).astype(vv.dtype), vv)

    out = lax.map(block, (xt.reshape(nblk, PEER_TOKEN_BLOCK, D),
                          idx.reshape(nblk, PEER_TOKEN_BLOCK, K),
                          gates.reshape(nblk, PEER_TOKEN_BLOCK, K)))
    return out.reshape(Bsz, S, D).astype(xn.dtype)


def setup_inputs(seed: int = 0) -> dict:
    key = jax.random.key(seed)
    ks = jax.random.split(key, 16)
    f32 = jnp.float32

    def nrm(k, shape, scale):
        return jax.random.normal(k, shape, f32) * scale

    def gain(k, shape):
        return 1.0 + 0.02 * jax.random.normal(k, shape, f32)

    L = DEPTH
    return {
        'x': nrm(ks[0], (BATCH, SEQ, D_MODEL), 1.0),
        'norm1_g': gain(ks[1], (L, D_MODEL)),
        'w_in': nrm(ks[2], (L, D_MODEL, IN_W), D_MODEL ** -0.5),
        'gla_up_f': nrm(ks[3], (L, GLA_GATE_RANK, GLA_QK_W), GLA_GATE_RANK ** -0.5),
        'gla_bias_f': nrm(ks[4], (L, GLA_QK_W), 0.1),
        'gla_up_b': nrm(ks[5], (L, GLA_GATE_RANK, GLA_QK_W), GLA_GATE_RANK ** -0.5),
        'gla_bias_b': nrm(ks[6], (L, GLA_QK_W), 0.1),
        'gla_out_g': gain(ks[7], (L, GLA_V_W)),
        'q_norm_g': gain(ks[8], (L, HEAD_DIM)),
        'k_norm_g': gain(ks[9], (L, HEAD_DIM)),
        'w_out': nrm(ks[10], (L, MIX_W, D_MODEL), MIX_W ** -0.5),
        'norm2_g': gain(ks[11], (L, D_MODEL)),
        'peer_w_query': nrm(ks[12], (L, D_MODEL, PEER_HEADS * PEER_QDIM), D_MODEL ** -0.5),
        'peer_sub_keys': nrm(ks[13], (L, PEER_HEADS, 2, PEER_NKEYS, PEER_QDIM // 2), (PEER_QDIM // 2) ** -0.5),
        'peer_down': nrm(ks[14], (L, PEER_NEXPERTS, D_MODEL), D_MODEL ** -0.5),
        'peer_up': nrm(ks[15], (L, PEER_NEXPERTS, D_MODEL), PEER_HEADS ** -0.5),
    }


def reference(x, norm1_g, w_in, gla_up_f, gla_bias_f, gla_up_b, gla_bias_b, gla_out_g, q_norm_g, k_norm_g, w_out, norm2_g, peer_w_query, peer_sub_keys, peer_down, peer_up):
    positions = jnp.arange(x.shape[1], dtype=jnp.int32)
    split_points = [int(c) for c in np.cumsum(IN_SPLITS)[:-1]]
    for l in range(DEPTH):
        xn = rms_norm(x, norm1_g[l])
        h = xn @ w_in[l]
        qa, ka, va, gdf, gdb, ra, qd, kd, vd = jnp.split(h, split_points, axis=-1)
        oa = gla_mixer(qa, ka, va, gdf, gdb, ra, gla_up_f[l], gla_bias_f[l], gla_up_b[l], gla_bias_b[l], gla_out_g[l])
        od = dilated_mixer(qd, kd, vd, q_norm_g[l], k_norm_g[l], positions)
        mixed = jnp.concatenate([oa, od], axis=-1).astype(x.dtype)
        x = x + mixed @ w_out[l]
        x = x + peer_ffn(rms_norm(x, norm2_g[l]), peer_w_query[l], peer_sub_keys[l], peer_down[l], peer_up[l])
    return x
```

```python
import functools

import jax
import jax.numpy as jnp
import numpy as np
from jax import lax
from jax.experimental import pallas as pl
from jax.experimental.pallas import tpu as pltpu

F32 = jnp.float32
BF16 = jnp.bfloat16

EPS = 1e-6
HEAD_DIM = 128

GLA_HEADS = 4
GLA_DK = 64
GLA_DV = 128
GLA_GATE_RANK = 16
GLA_TAU = 16.0
GLA_CHUNK = 64
GLA_QK_W = GLA_HEADS * GLA_DK
GLA_V_W = GLA_HEADS * GLA_DV

DIL_HEADS = 12
DIL_PATTERNS = ((128, 1), (512, 4), (2048, 16))
DIL_W = DIL_HEADS * HEAD_DIM
ROPE_THETA = 500000.0
ROPE_DIM = HEAD_DIM // 4
MASK_VALUE = -1e30

PEER_HEADS = 8
PEER_NKEYS = 128
PEER_QDIM = 256
PEER_TOPK = 16

MIX_W = GLA_V_W + DIL_W
IN_SPLITS = (GLA_QK_W, GLA_QK_W, GLA_V_W, GLA_GATE_RANK, GLA_GATE_RANK, GLA_V_W, DIL_W, DIL_W, DIL_W)

LANES = 128
COL_QD = 0
COL_KD = COL_QD + DIL_W
COL_VD = COL_KD + DIL_W
COL_QA = COL_VD + DIL_W
COL_KA = COL_QA + GLA_QK_W
COL_VA = COL_KA + GLA_QK_W
COL_RA = COL_VA + GLA_V_W
COL_GD = COL_RA + GLA_V_W
IN_W_PAD = COL_GD + LANES

VMEM_LIMIT = 56 * 1024 * 1024


def _cparams(sem):
    return pltpu.CompilerParams(dimension_semantics=sem, vmem_limit_bytes=VMEM_LIMIT)


def _split_bf16(a):
    hi = a.astype(BF16)
    lo = (a - hi.astype(F32)).astype(BF16)
    return hi, lo


def _dot(a, b):
    return jnp.dot(a, b, preferred_element_type=F32)


def _dot_nt(a, b):
    return lax.dot_general(a, b, (((1,), (1,)), ((), ())), preferred_element_type=F32)


def _dot_tn(a, b):
    return lax.dot_general(a, b, (((0,), (0,)), ((), ())), preferred_element_type=F32)


def _dot3(a, b):
    ah, al = _split_bf16(a)
    bh, bl = _split_bf16(b)
    return _dot(ah, bh) + (_dot(ah, bl) + _dot(al, bh))


def _rms_matmul_kernel(x_ref, g_ref, w_ref, o_ref, xn_ref):
    @pl.when(pl.program_id(1) == 0)
    def _():
        x = x_ref[...]
        ms = jnp.mean(x * x, axis=-1, keepdims=True)
        xn_ref[...] = (x * lax.rsqrt(ms + EPS) * g_ref[...]).astype(BF16)

    o_ref[...] = _dot(xn_ref[...], w_ref[...])


def _rms_matmul(x, g, w, tm, tn):
    m, k = x.shape
    n = w.shape[1]
    return pl.pallas_call(
        _rms_matmul_kernel,
        grid=(m // tm, n // tn),
        in_specs=[
            pl.BlockSpec((tm, k), lambda i, j: (i, 0)),
            pl.BlockSpec((1, k), lambda i, j: (0, 0)),
            pl.BlockSpec((k, tn), lambda i, j: (0, j)),
        ],
        out_specs=pl.BlockSpec((tm, tn), lambda i, j: (i, j)),
        out_shape=jax.ShapeDtypeStruct((m, n), F32),
        scratch_shapes=[pltpu.VMEM((tm, k), BF16)],
        compiler_params=_cparams(("parallel", "arbitrary")),
        name="rms_inproj",
    )(x, g, w)


def _log_sigmoid(z):
    return -(jnp.maximum(-z, 0.0) + jnp.log1p(jnp.exp(-jnp.abs(z))))


def _gla_kernel(*refs, reverse, finalize):
    if finalize:
        q_ref, k_ref, v_ref, gd_ref, up_ref, bias_ref, ofwd_ref, r_ref, og_ref, o_ref, st_ref = refs
    else:
        q_ref, k_ref, v_ref, gd_ref, up_ref, bias_ref, o_ref, st_ref = refs
    rc = q_ref.shape[0]
    nchunk = rc // GLA_CHUNK

    @pl.when(pl.program_id(1) == 0)
    def _():
        st_ref[...] = jnp.zeros_like(st_ref)

    z = _dot3(gd_ref[...], up_ref[...]) + bias_ref[...]
    la = _log_sigmoid(z) * (1.0 / GLA_TAU)
    row = lax.broadcasted_iota(jnp.int32, (rc, rc), 0)
    col = lax.broadcasted_iota(jnp.int32, (rc, rc), 1)
    same = (row // GLA_CHUNK) == (col // GLA_CHUNK)
    tri = same & ((col >= row) if reverse else (col <= row))
    la_hi, la_lo = _split_bf16(la)
    tri_b = tri.astype(BF16)
    same_b = same.astype(BF16)
    b = _dot(tri_b, la_hi) + _dot(tri_b, la_lo)
    tot = _dot(same_b, la_hi) + _dot(same_b, la_lo)
    q = q_ref[...]
    k = k_ref[...]
    q_in = (q * jnp.exp(b) * (GLA_DK ** -0.5)).astype(BF16)
    k_in = (k * jnp.exp(-b)).astype(BF16)
    k_st = (k * jnp.exp(tot - b)).astype(BF16)
    decay = jnp.exp(tot)
    v = v_ref[...].astype(BF16)

    r64 = lax.broadcasted_iota(jnp.int32, (GLA_CHUNK, GLA_CHUNK), 0)
    c64 = lax.broadcasted_iota(jnp.int32, (GLA_CHUNK, GLA_CHUNK), 1)
    mask = (c64 > r64) if reverse else (c64 <= r64)

    chunks = range(nchunk - 1, -1, -1) if reverse else range(nchunk)
    for c in chunks:
        rs = slice(c * GLA_CHUNK, (c + 1) * GLA_CHUNK)
        for h in range(GLA_HEADS):
            ks = slice(h * GLA_DK, (h + 1) * GLA_DK)
            vs = slice(h * GLA_DV, (h + 1) * GLA_DV)
            qc = q_in[rs, ks]
            vc = v[rs, vs]
            sc = _dot_nt(qc, k_in[rs, ks])
            p = jnp.where(mask, sc, 0.0).astype(BF16)
            st = st_ref[h]
            o = _dot(p, vc) + _dot_nt(qc, st.astype(BF16))
            inc = _dot_tn(vc, k_st[rs, ks])
            st_ref[h] = decay[c * GLA_CHUNK:c * GLA_CHUNK + 1, ks] * st + inc
            if finalize:
                o = o + ofwd_ref[rs, vs]
                y = o * lax.rsqrt(jnp.mean(o * o, axis=-1, keepdims=True) + EPS) * og_ref[:, vs]
                rr = r_ref[rs, vs]
                o = y * (rr * jax.nn.sigmoid(rr))
            o_ref[rs, vs] = o


def _gla(h, up_pad, bias, ofwd, out_g, bsz, seq, rc, reverse):
    nsteps = seq // rc
    finalize = ofwd is not None

    def rowblk(b, n):
        return b * nsteps + ((nsteps - 1 - n) if reverse else n)

    in_specs = [
        pl.BlockSpec((rc, GLA_QK_W), lambda b, n: (rowblk(b, n), COL_QA // GLA_QK_W)),
        pl.BlockSpec((rc, GLA_QK_W), lambda b, n: (rowblk(b, n), COL_KA // GLA_QK_W)),
        pl.BlockSpec((rc, GLA_V_W), lambda b, n: (rowblk(b, n), COL_VA // GLA_V_W)),
        pl.BlockSpec((rc, LANES), lambda b, n: (rowblk(b, n), COL_GD // LANES)),
        pl.BlockSpec((LANES, GLA_QK_W), lambda b, n: (0, 0)),
        pl.BlockSpec((1, GLA_QK_W), lambda b, n: (0, 0)),
    ]
    args = [h, h, h, h, up_pad, bias]
    if finalize:
        in_specs += [
            pl.BlockSpec((rc, GLA_V_W), lambda b, n: (rowblk(b, n), 0)),
            pl.BlockSpec((rc, GLA_V_W), lambda b, n: (rowblk(b, n), COL_RA // GLA_V_W)),
            pl.BlockSpec((1, GLA_V_W), lambda b, n: (0, 0)),
        ]
        args += [ofwd, h, out_g]
    return pl.pallas_call(
        functools.partial(_gla_kernel, reverse=reverse, finalize=finalize),
        grid=(bsz, nsteps),
        in_specs=in_specs,
        out_specs=pl.BlockSpec((rc, GLA_V_W), lambda b, n: (rowblk(b, n), 0)),
        out_shape=jax.ShapeDtypeStruct((bsz * seq, GLA_V_W), F32),
        scratch_shapes=[pltpu.VMEM((GLA_HEADS, GLA_DV, GLA_DK), F32)],
        compiler_params=_cparams(("parallel", "arbitrary")),
        name="gla_bwd" if reverse else "gla_fwd",
    )(*args)


def _qk_prep_kernel(q_ref, k_ref, v_ref, qg_ref, kg_ref, c_ref, s1_ref, s2_ref, qo_ref, ko_ref, vo_ref):
    cos = c_ref[...]
    sn1 = s1_ref[...]
    sn2 = s2_ref[...]

    def norm_rope(x, g):
        y = x * lax.rsqrt(jnp.mean(x * x, axis=-1, keepdims=True) + EPS) * g
        return (y * cos
                + pltpu.roll(y, HEAD_DIM - ROPE_DIM // 2, axis=1) * sn1
                + pltpu.roll(y, ROPE_DIM // 2, axis=1) * sn2)

    for h in range(DIL_HEADS):
        hs = slice(h * HEAD_DIM, (h + 1) * HEAD_DIM)
        qo_ref[:, hs] = (norm_rope(q_ref[:, hs], qg_ref[...]) * (HEAD_DIM ** -0.5)).astype(BF16)
        ko_ref[:, hs] = norm_rope(k_ref[:, hs], kg_ref[...]).astype(BF16)
    vo_ref[...] = v_ref[...].astype(BF16)


def _qk_prep(h, qg, kg, cos, sn1, sn2, bsz, seq, rc):
    nsteps = seq // rc
    t = bsz * seq
    hblk = lambda c: pl.BlockSpec((rc, DIL_W), lambda b, n: (b * nsteps + n, c // DIL_W))
    tab = pl.BlockSpec((rc, HEAD_DIM), lambda b, n: (n, 0))
    gsp = pl.BlockSpec((1, HEAD_DIM), lambda b, n: (0, 0))
    osp = pl.BlockSpec((rc, DIL_W), lambda b, n: (b * nsteps + n, 0))
    oshape = jax.ShapeDtypeStruct((t, DIL_W), BF16)
    return pl.pallas_call(
        _qk_prep_kernel,
        grid=(bsz, nsteps),
        in_specs=[hblk(COL_QD), hblk(COL_KD), hblk(COL_VD), gsp, gsp, tab, tab, tab],
        out_specs=[osp, osp, osp],
        out_shape=[oshape, oshape, oshape],
        compiler_params=_cparams(("parallel", "parallel")),
        name="qk_prep",
    )(h, h, h, qg, kg, cos, sn1, sn2)


def _dilated_kernel(q_ref, km_ref, kl_ref, kr_ref, vm_ref, vl_ref, vr_ref, o_ref, lse_ref, *, half):
    n = pl.program_id(2)
    last = pl.num_programs(2) - 1
    tq = q_ref.shape[1]
    r_m = lax.broadcasted_iota(jnp.int32, (tq, tq), 0)
    c_m = lax.broadcasted_iota(jnp.int32, (tq, tq), 1)
    valid_m = jnp.abs(c_m - r_m) <= half
    r_h = lax.broadcasted_iota(jnp.int32, (tq, half), 0)
    c_h = lax.broadcasted_iota(jnp.int32, (tq, half), 1)
    valid_l = (c_h >= r_h) & (n > 0)
    valid_r = (c_h <= r_h - (tq - half)) & (n < last)
    lane = lax.broadcasted_iota(jnp.int32, (tq, LANES), 1)
    lse_tile = jnp.zeros((tq, LANES), F32)
    for h in range(DIL_HEADS):
        hs = slice(h * HEAD_DIM, (h + 1) * HEAD_DIM)
        q = q_ref[0, :, hs]
        s_m = jnp.where(valid_m, _dot_nt(q, km_ref[0, :, hs]), MASK_VALUE)
        s_l = jnp.where(valid_l, _dot_nt(q, kl_ref[0, :, hs]), MASK_VALUE)
        s_r = jnp.where(valid_r, _dot_nt(q, kr_ref[0, :, hs]), MASK_VALUE)
        m = jnp.maximum(jnp.max(s_m, axis=-1, keepdims=True),
                        jnp.maximum(jnp.max(s_l, axis=-1, keepdims=True), jnp.max(s_r, axis=-1, keepdims=True)))
        p_m = jnp.exp(s_m - m)
        p_l = jnp.exp(s_l - m)
        p_r = jnp.exp(s_r - m)
        den = (jnp.sum(p_m, axis=-1, keepdims=True) + jnp.sum(p_l, axis=-1, keepdims=True)
               + jnp.sum(p_r, axis=-1, keepdims=True))
        o = (_dot(p_m.astype(BF16), vm_ref[0, :, hs]) + _dot(p_l.astype(BF16), vl_ref[0, :, hs])
             + _dot(p_r.astype(BF16), vr_ref[0, :, hs]))
        o_ref[0, :, hs] = (o / den).astype(o_ref.dtype)
        lse_tile = jnp.where(lane == h, m + jnp.log(den), lse_tile)
    lse_ref[0] = lse_tile


def _dilated(q, k, v, bsz, seq, window, dilation, tq):
    half = window // (2 * dilation)
    ln = seq // dilation
    nt = ln // tq
    per = tq // half
    nh = ln // half
    qv = q.reshape(bsz, ln, dilation * DIL_W)
    kv = k.reshape(bsz, ln, dilation * DIL_W)
    vv = v.reshape(bsz, ln, dilation * DIL_W)
    main = pl.BlockSpec((1, tq, DIL_W), lambda b, r, n: (b, n, r))
    left = pl.BlockSpec((1, half, DIL_W), lambda b, r, n: (b, jnp.maximum(n * per - 1, 0), r))
    right = pl.BlockSpec((1, half, DIL_W), lambda b, r, n: (b, jnp.minimum((n + 1) * per, nh - 1), r))
    o, lse = pl.pallas_call(
        functools.partial(_dilated_kernel, half=half),
        grid=(bsz, dilation, nt),
        in_specs=[main, main, left, right, main, left, right],
        out_specs=[main, pl.BlockSpec((1, tq, LANES), lambda b, r, n: (b, n, r))],
        out_shape=[jax.ShapeDtypeStruct((bsz, ln, dilation * DIL_W), BF16),
                   jax.ShapeDtypeStruct((bsz, ln, dilation * LANES), F32)],
        compiler_params=_cparams(("parallel", "parallel", "parallel")),
        name=f"dilated_d{dilation}",
    )(qv, kv, kv, kv, vv, vv, vv)
    return o.reshape(bsz * seq, DIL_W), lse.reshape(bsz * seq, LANES)


def _combine_kernel(oa_ref, o1_ref, o2_ref, o3_ref, l1_ref, l2_ref, l3_ref, out_ref):
    out_ref[:, :GLA_V_W] = oa_ref[...].astype(BF16)
    l1 = l1_ref[...]
    l2 = l2_ref[...]
    l3 = l3_ref[...]
    mx = jnp.maximum(l1, jnp.maximum(l2, l3))
    e1 = jnp.exp(l1 - mx)
    e2 = jnp.exp(l2 - mx)
    e3 = jnp.exp(l3 - mx)
    inv = 1.0 / (e1 + e2 + e3)
    w1 = e1 * inv
    w2 = e2 * inv
    w3 = e3 * inv
    rows = oa_ref.shape[0]
    for h in range(DIL_HEADS):
        hs = slice(h * HEAD_DIM, (h + 1) * HEAD_DIM)
        bc = lambda w: jnp.broadcast_to(w[:, h:h + 1], (rows, HEAD_DIM))
        o = (bc(w1) * o1_ref[:, hs].astype(F32) + bc(w2) * o2_ref[:, hs].astype(F32)
             + bc(w3) * o3_ref[:, hs].astype(F32))
        out_ref[:, GLA_V_W + h * HEAD_DIM:GLA_V_W + (h + 1) * HEAD_DIM] = o.astype(BF16)


def _combine(oa, os, ls, rc):
    t = oa.shape[0]
    rb = lambda w: pl.BlockSpec((rc, w), lambda i: (i, 0))
    return pl.pallas_call(
        _combine_kernel,
        grid=(t // rc,),
        in_specs=[rb(GLA_V_W), rb(DIL_W), rb(DIL_W), rb(DIL_W), rb(LANES), rb(LANES), rb(LANES)],
        out_specs=rb(MIX_W),
        out_shape=jax.ShapeDtypeStruct((t, MIX_W), BF16),
        compiler_params=_cparams(("parallel",)),
        name="combine",
    )(oa, *os, *ls)


def _outproj_kernel(mix_ref, w_ref, x_ref, g_ref, x2_ref, xnt_ref):
    x2 = x_ref[...] + _dot(mix_ref[...], w_ref[...])
    x2_ref[...] = x2
    xn = x2 * lax.rsqrt(jnp.mean(x2 * x2, axis=-1, keepdims=True) + EPS) * g_ref[...]
    xnt_ref[...] = xn.T.astype(BF16)


def _outproj(mixed, w_out, x, g2, tm):
    t, d = x.shape
    return pl.pallas_call(
        _outproj_kernel,
        grid=(t // tm,),
        in_specs=[
            pl.BlockSpec((tm, MIX_W), lambda i: (i, 0)),
            pl.BlockSpec((MIX_W, d), lambda i: (0, 0)),
            pl.BlockSpec((tm, d), lambda i: (i, 0)),
            pl.BlockSpec((1, d), lambda i: (0, 0)),
        ],
        out_specs=[pl.BlockSpec((tm, d), lambda i: (i, 0)), pl.BlockSpec((d, tm), lambda i: (0, i))],
        out_shape=[jax.ShapeDtypeStruct((t, d), F32), jax.ShapeDtypeStruct((d, t), BF16)],
        compiler_params=_cparams(("parallel",)),
        name="outproj",
    )(mixed, w_out, x, g2)


_CAND = [(a, b) for a in range(PEER_TOPK) for b in range(PEER_TOPK) if (a + 1) * (b + 1) <= PEER_TOPK]
_NCAND_PAD = -(-len(_CAND) // 8) * 8


def _top_values(sc, count):
    vals = []
    for _ in range(count):
        m = jnp.max(sc, axis=0, keepdims=True)
        vals.append(m)
        sc = jnp.where(sc >= m, -jnp.inf, sc)
    return vals


def _peer_route_kernel(wq_ref, xnt_ref, sk_ref, s1_ref, s2_ref, w1_ref, v2_ref, tau_ref, q_ref, cand_ref):
    q_ref[...] = _dot(wq_ref[...], xnt_ref[...])
    half = PEER_QDIM // 2
    cand_ref[...] = jnp.full(cand_ref.shape, -jnp.inf, F32)

    def head(h, carry):
        sc_a = _dot3(sk_ref[h, 0], q_ref[pl.ds(pl.multiple_of(h * PEER_QDIM, PEER_QDIM), half), :])
        sc_b = _dot3(sk_ref[h, 1], q_ref[pl.ds(pl.multiple_of(h * PEER_QDIM + half, half), half), :])
        top_a = _top_values(sc_a, PEER_TOPK)
        top_b = _top_values(sc_b, PEER_TOPK)
        for i, (a, b) in enumerate(_CAND):
            cand_ref[i:i + 1, :] = top_a[a] + top_b[b]
        cand = cand_ref[...]
        tau = _top_values(cand, PEER_TOPK)[-1]
        best = top_a[0] + top_b[0]
        zsum = jnp.sum(jnp.where(cand >= tau, jnp.exp(cand - best), 0.0), axis=0, keepdims=True)
        s1_ref[h] = sc_a
        s2_ref[h] = sc_b
        w1_ref[h] = jnp.exp(sc_a - top_a[0]) / zsum
        v2_ref[h] = jnp.exp(sc_b - top_b[0])
        tau_ref[h] = tau
        return carry

    lax.fori_loop(0, PEER_HEADS, head, 0)


def _peer_route(wq_t, xnt, sub_keys, tb):
    d, t = xnt.shape
    hq = wq_t.shape[0]
    key_blk = pl.BlockSpec((PEER_HEADS, PEER_NKEYS, tb), lambda i: (0, 0, i))
    key_shape = jax.ShapeDtypeStruct((PEER_HEADS, PEER_NKEYS, t), F32)
    return pl.pallas_call(
        _peer_route_kernel,
        grid=(t // tb,),
        in_specs=[
            pl.BlockSpec((hq, d), lambda i: (0, 0)),
            pl.BlockSpec((d, tb), lambda i: (0, i)),
            pl.BlockSpec(sub_keys.shape, lambda i: (0, 0, 0, 0)),
        ],
        out_specs=[key_blk, key_blk, key_blk, key_blk, pl.BlockSpec((PEER_HEADS, 1, tb), lambda i: (0, 0, i))],
        out_shape=[key_shape, key_shape, key_shape, key_shape, jax.ShapeDtypeStruct((PEER_HEADS, 1, t), F32)],
        scratch_shapes=[pltpu.VMEM((hq, tb), F32), pltpu.VMEM((_NCAND_PAD, tb), F32)],
        compiler_params=_cparams(("parallel",)),
        name="peer_route",
    )(wq_t, xnt, sub_keys)


ROWS_PER_STEP = 16


def _peer_expert_kernel(xnt_ref, down_ref, upt_ref, s1_ref, s2_ref, w1_ref, v2_ref, tau_ref, x2_ref,
                        o_ref, acc_ref, a_ref, h_ref):
    e = pl.program_id(1)
    eb, tb = a_ref.shape
    rows_i = eb // PEER_NKEYS
    groups = PEER_NKEYS // ROWS_PER_STEP

    @pl.when(e == 0)
    def _():
        acc_ref[...] = jnp.zeros_like(acc_ref)

    a_ref[...] = _dot(down_ref[...], xnt_ref[...])

    def body(idx, carry):
        ii = idx // groups
        jg = idx % groups
        i = e * rows_i + ii
        j0 = pl.multiple_of(jg * ROWS_PER_STEP, ROWS_PER_STEP)
        row0 = pl.multiple_of(ii * PEER_NKEYS + jg * ROWS_PER_STEP, ROWS_PER_STEP)
        gate = jnp.zeros((ROWS_PER_STEP, tb), F32)
        for h in range(PEER_HEADS):
            s1 = s1_ref[h, pl.ds(i, 1), :]
            w1 = w1_ref[h, pl.ds(i, 1), :]
            s2 = s2_ref[h, pl.ds(j0, ROWS_PER_STEP), :]
            v2 = v2_ref[h, pl.ds(j0, ROWS_PER_STEP), :]
            gate = gate + jnp.where(s1 + s2 >= tau_ref[h], w1 * v2, 0.0)
        act = jax.nn.gelu(a_ref[pl.ds(row0, ROWS_PER_STEP), :])
        h_ref[pl.ds(row0, ROWS_PER_STEP), :] = (gate * act).astype(BF16)
        return carry

    lax.fori_loop(0, rows_i * groups, body, 0)
    acc_ref[...] += _dot(upt_ref[...], h_ref[...])

    @pl.when(e == pl.num_programs(1) - 1)
    def _():
        o_ref[...] = x2_ref[...] + acc_ref[...].T


def _peer_experts(xnt, down, up_t, s1, s2, w1, v2, tau, x2, tb, eb):
    d, t = xnt.shape
    ne = down.shape[0]
    key_blk = pl.BlockSpec((PEER_HEADS, PEER_NKEYS, tb), lambda i, e: (0, 0, i))
    return pl.pallas_call(
        _peer_expert_kernel,
        grid=(t // tb, ne // eb),
        in_specs=[
            pl.BlockSpec((d, tb), lambda i, e: (0, i)),
            pl.BlockSpec((eb, d), lambda i, e: (e, 0)),
            pl.BlockSpec((d, eb), lambda i, e: (0, e)),
            key_blk, key_blk, key_blk, key_blk,
            pl.BlockSpec((PEER_HEADS, 1, tb), lambda i, e: (0, 0, i)),
            pl.BlockSpec((tb, d), lambda i, e: (i, 0)),
        ],
        out_specs=pl.BlockSpec((tb, d), lambda i, e: (i, 0)),
        out_shape=jax.ShapeDtypeStruct((t, d), F32),
        scratch_shapes=[pltpu.VMEM((d, tb), F32), pltpu.VMEM((eb, tb), F32), pltpu.VMEM((eb, tb), BF16)],
        compiler_params=_cparams(("parallel", "arbitrary")),
        name="peer_experts",
    )(xnt, down, up_t, s1, s2, w1, v2, tau, x2)


def _reorder_in_proj(w_in):
    qa, ka, va, gdf, gdb, ra, qd, kd, vd = jnp.split(w_in, [int(c) for c in np.cumsum(IN_SPLITS)[:-1]], axis=1)
    pad = jnp.zeros((w_in.shape[0], LANES - 2 * GLA_GATE_RANK), w_in.dtype)
    return jnp.concatenate([qd, kd, vd, qa, ka, va, ra, gdf, gdb, pad], axis=1).astype(BF16)


def _rope_tables(seq):
    half = ROPE_DIM // 2
    inv_freq = ROPE_THETA ** (-jnp.arange(half, dtype=F32) / half)
    ang = jnp.arange(seq, dtype=jnp.int32).astype(F32)[:, None] * inv_freq[None, :]
    cos = jnp.cos(ang)
    sin = jnp.sin(ang)
    zeros = jnp.zeros((seq, half), F32)
    rest = HEAD_DIM - ROPE_DIM
    c = jnp.concatenate([cos, cos, jnp.ones((seq, rest), F32)], axis=1)
    s1 = jnp.concatenate([-sin, zeros, jnp.zeros((seq, rest), F32)], axis=1)
    s2 = jnp.concatenate([zeros, sin, jnp.zeros((seq, rest), F32)], axis=1)
    return c, s1, s2


def _layer(x, norm1_g, w_in, up_f, bias_f, up_b, bias_b, out_g, q_norm_g, k_norm_g, w_out, norm2_g,
           w_query, sub_keys, down, up):
    bsz, seq, d = x.shape
    t = bsz * seq
    xf = x.reshape(t, d)

    h = _rms_matmul(xf, norm1_g.reshape(1, d), _reorder_in_proj(w_in), tm=512, tn=IN_W_PAD // 7)

    zpad = jnp.zeros((LANES, GLA_QK_W), F32)
    upf_pad = zpad.at[:GLA_GATE_RANK].set(up_f)
    upb_pad = zpad.at[GLA_GATE_RANK:2 * GLA_GATE_RANK].set(up_b)
    rc = min(256, seq)
    o_fwd = _gla(h, upf_pad, bias_f.reshape(1, -1), None, None, bsz, seq, rc, reverse=False)
    oa = _gla(h, upb_pad, bias_b.reshape(1, -1), o_fwd, out_g.reshape(1, -1), bsz, seq, rc, reverse=True)

    cos, sn1, sn2 = _rope_tables(seq)
    qn, kn, vn = _qk_prep(h, q_norm_g.reshape(1, -1), k_norm_g.reshape(1, -1), cos, sn1, sn2, bsz, seq, rc=256)
    outs, lses = [], []
    for window, dilation in DIL_PATTERNS:
        o_p, lse_p = _dilated(qn, kn, vn, bsz, seq, window, dilation, tq=256)
        outs.append(o_p)
        lses.append(lse_p)
    mixed = _combine(oa, outs, lses, rc=512)

    x2, xnt = _outproj(mixed, w_out.astype(BF16), xf, norm2_g.reshape(1, d), tm=512)

    s1, s2, w1, v2, tau = _peer_route(w_query.T.astype(BF16), xnt, sub_keys, tb=256)
    out = _peer_experts(xnt, down.astype(BF16), up.T.astype(BF16), s1, s2, w1, v2, tau, x2, tb=256, eb=1024)
    return out.reshape(bsz, seq, d)


def kernel(x, norm1_g, w_in, gla_up_f, gla_bias_f, gla_up_b, gla_bias_b, gla_out_g, q_norm_g, k_norm_g, w_out,
           norm2_g, peer_w_query, peer_sub_keys, peer_down, peer_up):
    for l in range(norm1_g.shape[0]):
        x = _layer(x, norm1_g[l], w_in[l], gla_up_f[l], gla_bias_f[l], gla_up_b[l], gla_bias_b[l], gla_out_g[l],
                   q_norm_g[l], k_norm_g[l], w_out[l], norm2_g[l], peer_w_query[l], peer_sub_keys[l],
                   peer_down[l], peer_up[l])
    return x
```

```python
import functools

import jax
import jax.numpy as jnp
import numpy as np
from jax import lax
from jax.experimental import pallas as pl
from jax.experimental.pallas import tpu as pltpu

F32 = jnp.float32
BF16 = jnp.bfloat16

EPS = 1e-6
HEAD_DIM = 128

GLA_HEADS = 4
GLA_DK = 64
GLA_DV = 128
GLA_GATE_RANK = 16
GLA_TAU = 16.0
GLA_CHUNK = 64
GLA_QK_W = GLA_HEADS * GLA_DK
GLA_V_W = GLA_HEADS * GLA_DV

DIL_HEADS = 12
DIL_PATTERNS = ((128, 1), (512, 4), (2048, 16))
DIL_W = DIL_HEADS * HEAD_DIM
ROPE_THETA = 500000.0
ROPE_DIM = HEAD_DIM // 4
MASK_VALUE = -1e30

PEER_HEADS = 8
PEER_NKEYS = 128
PEER_QDIM = 256
PEER_TOPK = 16

MIX_W = GLA_V_W + DIL_W
IN_SPLITS = (GLA_QK_W, GLA_QK_W, GLA_V_W, GLA_GATE_RANK, GLA_GATE_RANK, GLA_V_W, DIL_W, DIL_W, DIL_W)

LANES = 128
COL_QD = 0
COL_KD = COL_QD + DIL_W
COL_VD = COL_KD + DIL_W
COL_QA = COL_VD + DIL_W
COL_KA = COL_QA + GLA_QK_W
COL_VA = COL_KA + GLA_QK_W
COL_RA = COL_VA + GLA_V_W
COL_GD = COL_RA + GLA_V_W
IN_W_PAD = COL_GD + LANES

VMEM_LIMIT = 56 * 1024 * 1024


def _cparams(sem):
    return pltpu.CompilerParams(dimension_semantics=sem, vmem_limit_bytes=VMEM_LIMIT)


def _split_bf16(a):
    hi = a.astype(BF16)
    lo = (a - hi.astype(F32)).astype(BF16)
    return hi, lo


def _dot(a, b):
    return jnp.dot(a, b, preferred_element_type=F32)


def _dot_nt(a, b):
    return lax.dot_general(a, b, (((1,), (1,)), ((), ())), preferred_element_type=F32)


def _dot_tn(a, b):
    return lax.dot_general(a, b, (((0,), (0,)), ((), ())), preferred_element_type=F32)


def _dot3(a, b):
    ah, al = _split_bf16(a)
    bh, bl = _split_bf16(b)
    return _dot(ah, bh) + (_dot(ah, bl) + _dot(al, bh))


def _rms_matmul_kernel(x_ref, g_ref, w_ref, o_ref, xn_ref):
    @pl.when(pl.program_id(1) == 0)
    def _():
        x = x_ref[...]
        ms = jnp.mean(x * x, axis=-1, keepdims=True)
        xn_ref[...] = (x * lax.rsqrt(ms + EPS) * g_ref[...]).astype(BF16)

    o_ref[...] = _dot(xn_ref[...], w_ref[...])


def _rms_matmul(x, g, w, tm, tn):
    m, k = x.shape
    n = w.shape[1]
    return pl.pallas_call(
        _rms_matmul_kernel,
        grid=(m // tm, n // tn),
        in_specs=[
            pl.BlockSpec((tm, k), lambda i, j: (i, 0)),
            pl.BlockSpec((1, k), lambda i, j: (0, 0)),
            pl.BlockSpec((k, tn), lambda i, j: (0, j)),
        ],
        out_specs=pl.BlockSpec((tm, tn), lambda i, j: (i, j)),
        out_shape=jax.ShapeDtypeStruct((m, n), F32),
        scratch_shapes=[pltpu.VMEM((tm, k), BF16)],
        compiler_params=_cparams(("parallel", "arbitrary")),
        name="rms_inproj",
    )(x, g, w)


def _log_sigmoid(z):
    return -(jnp.maximum(-z, 0.0) + jnp.log1p(jnp.exp(-jnp.abs(z))))


def _gla_kernel(*refs, reverse, finalize):
    if finalize:
        q_ref, k_ref, v_ref, gd_ref, up_ref, bias_ref, ofwd_ref, r_ref, og_ref, o_ref, st_ref = refs
    else:
        q_ref, k_ref, v_ref, gd_ref, up_ref, bias_ref, o_ref, st_ref = refs
    rc = q_ref.shape[0]
    nchunk = rc // GLA_CHUNK

    @pl.when(pl.program_id(1) == 0)
    def _():
        st_ref[...] = jnp.zeros_like(st_ref)

    z = _dot3(gd_ref[...], up_ref[...]) + bias_ref[...]
    la = _log_sigmoid(z) * (1.0 / GLA_TAU)
    row = lax.broadcasted_iota(jnp.int32, (rc, rc), 0)
    col = lax.broadcasted_iota(jnp.int32, (rc, rc), 1)
    same = (row // GLA_CHUNK) == (col // GLA_CHUNK)
    tri = same & ((col >= row) if reverse else (col <= row))
    la_hi, la_lo = _split_bf16(la)
    tri_b = tri.astype(BF16)
    same_b = same.astype(BF16)
    b = _dot(tri_b, la_hi) + _dot(tri_b, la_lo)
    tot = _dot(same_b, la_hi) + _dot(same_b, la_lo)
    q = q_ref[...]
    k = k_ref[...]
    q_in = (q * jnp.exp(b) * (GLA_DK ** -0.5)).astype(BF16)
    k_in = (k * jnp.exp(-b)).astype(BF16)
    k_st = (k * jnp.exp(tot - b)).astype(BF16)
    decay = jnp.exp(tot)
    v = v_ref[...].astype(BF16)

    r64 = lax.broadcasted_iota(jnp.int32, (GLA_CHUNK, GLA_CHUNK), 0)
    c64 = lax.broadcasted_iota(jnp.int32, (GLA_CHUNK, GLA_CHUNK), 1)
    mask = (c64 > r64) if reverse else (c64 <= r64)

    chunks = range(nchunk - 1, -1, -1) if reverse else range(nchunk)
    for c in chunks:
        rs = slice(c * GLA_CHUNK, (c + 1) * GLA_CHUNK)
        for h in range(GLA_HEADS):
            ks = slice(h * GLA_DK, (h + 1) * GLA_DK)
            vs = slice(h * GLA_DV, (h + 1) * GLA_DV)
            qc = q_in[rs, ks]
            vc = v[rs, vs]
            sc = _dot_nt(qc, k_in[rs, ks])
            p = jnp.where(mask, sc, 0.0).astype(BF16)
            st = st_ref[h]
            o = _dot(p, vc) + _dot_nt(qc, st.astype(BF16))
            inc = _dot_tn(vc, k_st[rs, ks])
            st_ref[h] = decay[c * GLA_CHUNK:c * GLA_CHUNK + 1, ks] * st + inc
            if finalize:
                o = o + ofwd_ref[rs, vs]
                y = o * lax.rsqrt(jnp.mean(o * o, axis=-1, keepdims=True) + EPS) * og_ref[:, vs]
                rr = r_ref[rs, vs]
                o = y * (rr * jax.nn.sigmoid(rr))
            o_ref[rs, vs] = o


def _gla(h, up_pad, bias, ofwd, out_g, bsz, seq, rc, reverse):
    nsteps = seq // rc
    finalize = ofwd is not None

    def rowblk(b, n):
        return b * nsteps + ((nsteps - 1 - n) if reverse else n)

    in_specs = [
        pl.BlockSpec((rc, GLA_QK_W), lambda b, n: (rowblk(b, n), COL_QA // GLA_QK_W)),
        pl.BlockSpec((rc, GLA_QK_W), lambda b, n: (rowblk(b, n), COL_KA // GLA_QK_W)),
        pl.BlockSpec((rc, GLA_V_W), lambda b, n: (rowblk(b, n), COL_VA // GLA_V_W)),
        pl.BlockSpec((rc, LANES), lambda b, n: (rowblk(b, n), COL_GD // LANES)),
        pl.BlockSpec((LANES, GLA_QK_W), lambda b, n: (0, 0)),
        pl.BlockSpec((1, GLA_QK_W), lambda b, n: (0, 0)),
    ]
    args = [h, h, h, h, up_pad, bias]
    if finalize:
        in_specs += [
            pl.BlockSpec((rc, GLA_V_W), lambda b, n: (rowblk(b, n), 0)),
            pl.BlockSpec((rc, GLA_V_W), lambda b, n: (rowblk(b, n), COL_RA // GLA_V_W)),
            pl.BlockSpec((1, GLA_V_W), lambda b, n: (0, 0)),
        ]
        args += [ofwd, h, out_g]
    return pl.pallas_call(
        functools.partial(_gla_kernel, reverse=reverse, finalize=finalize),
        grid=(bsz, nsteps),
        in_specs=in_specs,
        out_specs=pl.BlockSpec((rc, GLA_V_W), lambda b, n: (rowblk(b, n), 0)),
        out_shape=jax.ShapeDtypeStruct((bsz * seq, GLA_V_W), F32),
        scratch_shapes=[pltpu.VMEM((GLA_HEADS, GLA_DV, GLA_DK), F32)],
        compiler_params=_cparams(("parallel", "arbitrary")),
        name="gla_bwd" if reverse else "gla_fwd",
    )(*args)


def _qk_prep_kernel(q_ref, k_ref, v_ref, qg_ref, kg_ref, c_ref, s1_ref, s2_ref, qo_ref, ko_ref, vo_ref):
    cos = c_ref[...]
    sn1 = s1_ref[...]
    sn2 = s2_ref[...]

    def norm_rope(x, g):
        y = x * lax.rsqrt(jnp.mean(x * x, axis=-1, keepdims=True) + EPS) * g
        return (y * cos
                + pltpu.roll(y, HEAD_DIM - ROPE_DIM // 2, axis=1) * sn1
                + pltpu.roll(y, ROPE_DIM // 2, axis=1) * sn2)

    for h in range(DIL_HEADS):
        hs = slice(h * HEAD_DIM, (h + 1) * HEAD_DIM)
        qo_ref[:, hs] = (norm_rope(q_ref[:, hs], qg_ref[...]) * (HEAD_DIM ** -0.5)).astype(BF16)
        ko_ref[:, hs] = norm_rope(k_ref[:, hs], kg_ref[...]).astype(BF16)
    vo_ref[...] = v_ref[...].astype(BF16)


def _qk_prep(h, qg, kg, cos, sn1, sn2, bsz, seq, rc):
    nsteps = seq // rc
    t = bsz * seq
    hblk = lambda c: pl.BlockSpec((rc, DIL_W), lambda b, n: (b * nsteps + n, c // DIL_W))
    tab = pl.BlockSpec((rc, HEAD_DIM), lambda b, n: (n, 0))
    gsp = pl.BlockSpec((1, HEAD_DIM), lambda b, n: (0, 0))
    osp = pl.BlockSpec((rc, DIL_W), lambda b, n: (b * nsteps + n, 0))
    oshape = jax.ShapeDtypeStruct((t, DIL_W), BF16)
    return pl.pallas_call(
        _qk_prep_kernel,
        grid=(bsz, nsteps),
        in_specs=[hblk(COL_QD), hblk(COL_KD), hblk(COL_VD), gsp, gsp, tab, tab, tab],
        out_specs=[osp, osp, osp],
        out_shape=[oshape, oshape, oshape],
        compiler_params=_cparams(("parallel", "parallel")),
        name="qk_prep",
    )(h, h, h, qg, kg, cos, sn1, sn2)


def _dilated_kernel(q_ref, km_ref, kl_ref, kr_ref, vm_ref, vl_ref, vr_ref, o_ref, lse_ref, *, half):
    n = pl.program_id(2)
    last = pl.num_programs(2) - 1
    tq = q_ref.shape[1]
    r_m = lax.broadcasted_iota(jnp.int32, (tq, tq), 0)
    c_m = lax.broadcasted_iota(jnp.int32, (tq, tq), 1)
    valid_m = jnp.abs(c_m - r_m) <= half
    r_h = lax.broadcasted_iota(jnp.int32, (tq, half), 0)
    c_h = lax.broadcasted_iota(jnp.int32, (tq, half), 1)
    valid_l = (c_h >= r_h) & (n > 0)
    valid_r = (c_h <= r_h - (tq - half)) & (n < last)
    lane = lax.broadcasted_iota(jnp.int32, (tq, LANES), 1)
    lse_tile = jnp.zeros((tq, LANES), F32)
    for h in range(DIL_HEADS):
        hs = slice(h * HEAD_DIM, (h + 1) * HEAD_DIM)
        q = q_ref[0, :, hs]
        s_m = jnp.where(valid_m, _dot_nt(q, km_ref[0, :, hs]), MASK_VALUE)
        s_l = jnp.where(valid_l, _dot_nt(q, kl_ref[0, :, hs]), MASK_VALUE)
        s_r = jnp.where(valid_r, _dot_nt(q, kr_ref[0, :, hs]), MASK_VALUE)
        m = jnp.maximum(jnp.max(s_m, axis=-1, keepdims=True),
                        jnp.maximum(jnp.max(s_l, axis=-1, keepdims=True), jnp.max(s_r, axis=-1, keepdims=True)))
        p_m = jnp.exp(s_m - m)
        p_l = jnp.exp(s_l - m)
        p_r = jnp.exp(s_r - m)
        den = (jnp.sum(p_m, axis=-1, keepdims=True) + jnp.sum(p_l, axis=-1, keepdims=True)
               + jnp.sum(p_r, axis=-1, keepdims=True))
        o = (_dot(p_m.astype(BF16), vm_ref[0, :, hs]) + _dot(p_l.astype(BF16), vl_ref[0, :, hs])
             + _dot(p_r.astype(BF16), vr_ref[0, :, hs]))
        o_ref[0, :, hs] = (o / den).astype(o_ref.dtype)
        lse_tile = jnp.where(lane == h, m + jnp.log(den), lse_tile)
    lse_ref[0] = lse_tile


def _dilated(q, k, v, bsz, seq, window, dilation, tq):
    half = window // (2 * dilation)
    ln = seq // dilation
    nt = ln // tq
    per = tq // half
    nh = ln // half
    qv = q.reshape(bsz, ln, dilation * DIL_W)
    kv = k.reshape(bsz, ln, dilation * DIL_W)
    vv = v.reshape(bsz, ln, dilation * DIL_W)
    main = pl.BlockSpec((1, tq, DIL_W), lambda b, r, n: (b, n, r))
    left = pl.BlockSpec((1, half, DIL_W), lambda b, r, n: (b, jnp.maximum(n * per - 1, 0), r))
    right = pl.BlockSpec((1, half, DIL_W), lambda b, r, n: (b, jnp.minimum((n + 1) * per, nh - 1), r))
    o, lse = pl.pallas_call(
        functools.partial(_dilated_kernel, half=half),
        grid=(bsz, dilation, nt),
        in_specs=[main, main, left, right, main, left, right],
        out_specs=[main, pl.BlockSpec((1, tq, LANES), lambda b, r, n: (b, n, r))],
        out_shape=[jax.ShapeDtypeStruct((bsz, ln, dilation * DIL_W), BF16),
                   jax.ShapeDtypeStruct((bsz, ln, dilation * LANES), F32)],
        compiler_params=_cparams(("parallel", "parallel", "parallel")),
        name=f"dilated_d{dilation}",
    )(qv, kv, kv, kv, vv, vv, vv)
    return o.reshape(bsz * seq, DIL_W), lse.reshape(bsz * seq, LANES)


def _combine_kernel(oa_ref, o1_ref, o2_ref, o3_ref, l1_ref, l2_ref, l3_ref, out_ref):
    out_ref[:, :GLA_V_W] = oa_ref[...].astype(BF16)
    l1 = l1_ref[...]
    l2 = l2_ref[...]
    l3 = l3_ref[...]
    mx = jnp.maximum(l1, jnp.maximum(l2, l3))
    e1 = jnp.exp(l1 - mx)
    e2 = jnp.exp(l2 - mx)
    e3 = jnp.exp(l3 - mx)
    inv = 1.0 / (e1 + e2 + e3)
    w1 = e1 * inv
    w2 = e2 * inv
    w3 = e3 * inv
    rows = oa_ref.shape[0]
    for h in range(DIL_HEADS):
        hs = slice(h * HEAD_DIM, (h + 1) * HEAD_DIM)
        bc = lambda w: jnp.broadcast_to(w[:, h:h + 1], (rows, HEAD_DIM))
        o = (bc(w1) * o1_ref[:, hs].astype(F32) + bc(w2) * o2_ref[:, hs].astype(F32)
             + bc(w3) * o3_ref[:, hs].astype(F32))
        out_ref[:, GLA_V_W + h * HEAD_DIM:GLA_V_W + (h + 1) * HEAD_DIM] = o.astype(BF16)


def _combine(oa, os, ls, rc):
    t = oa.shape[0]
    rb = lambda w: pl.BlockSpec((rc, w), lambda i: (i, 0))
    return pl.pallas_call(
        _combine_kernel,
        grid=(t // rc,),
        in_specs=[rb(GLA_V_W), rb(DIL_W), rb(DIL_W), rb(DIL_W), rb(LANES), rb(LANES), rb(LANES)],
        out_specs=rb(MIX_W),
        out_shape=jax.ShapeDtypeStruct((t, MIX_W), BF16),
        compiler_params=_cparams(("parallel",)),
        name="combine",
    )(oa, *os, *ls)


def _outproj_kernel(mix_ref, w_ref, x_ref, g_ref, x2_ref, xnt_ref):
    x2 = x_ref[...] + _dot(mix_ref[...], w_ref[...])
    x2_ref[...] = x2
    xn = x2 * lax.rsqrt(jnp.mean(x2 * x2, axis=-1, keepdims=True) + EPS) * g_ref[...]
    xnt_ref[...] = xn.T.astype(BF16)


def _outproj(mixed, w_out, x, g2, tm):
    t, d = x.shape
    return pl.pallas_call(
        _outproj_kernel,
        grid=(t // tm,),
        in_specs=[
            pl.BlockSpec((tm, MIX_W), lambda i: (i, 0)),
            pl.BlockSpec((MIX_W, d), lambda i: (0, 0)),
            pl.BlockSpec((tm, d), lambda i: (i, 0)),
            pl.BlockSpec((1, d), lambda i: (0, 0)),
        ],
        out_specs=[pl.BlockSpec((tm, d), lambda i: (i, 0)), pl.BlockSpec((d, tm), lambda i: (0, i))],
        out_shape=[jax.ShapeDtypeStruct((t, d), F32), jax.ShapeDtypeStruct((d, t), BF16)],
        compiler_params=_cparams(("parallel",)),
        name="outproj",
    )(mixed, w_out, x, g2)


_NTOP = PEER_TOPK + 1
_CAND = [(a, b) for a in range(_NTOP) for b in range(_NTOP) if (a + 1) * (b + 1) <= _NTOP]
_NCAND_PAD = -(-len(_CAND) // 8) * 8


def _top_values(sc, count):
    vals = []
    for _ in range(count):
        m = jnp.max(sc, axis=0, keepdims=True)
        vals.append(m)
        sc = jnp.where(sc >= m, -jnp.inf, sc)
    return vals


def _peer_route_kernel(wq_ref, xnt_ref, sk_ref, thr_ref, w1_ref, s2_ref, v2_ref, q_ref, cand_ref):
    q_ref[...] = _dot(wq_ref[...], xnt_ref[...])
    half = PEER_QDIM // 2
    cand_ref[...] = jnp.full(cand_ref.shape, -jnp.inf, F32)

    def head(h, carry):
        sc_a = _dot3(sk_ref[h, 0], q_ref[pl.ds(pl.multiple_of(h * PEER_QDIM, PEER_QDIM), half), :])
        sc_b = _dot3(sk_ref[h, 1], q_ref[pl.ds(pl.multiple_of(h * PEER_QDIM + half, half), half), :])
        top_a = _top_values(sc_a, _NTOP)
        top_b = _top_values(sc_b, _NTOP)
        for i, (a, b) in enumerate(_CAND):
            cand_ref[i:i + 1, :] = top_a[a] + top_b[b]
        cand = cand_ref[...]
        top_c = _top_values(cand, _NTOP)
        tau = 0.5 * (top_c[PEER_TOPK - 1] + top_c[PEER_TOPK])
        best = top_a[0] + top_b[0]
        zsum = jnp.sum(jnp.where(cand > tau, jnp.exp(cand - best), 0.0), axis=0, keepdims=True)
        thr_ref[h] = tau - sc_a
        w1_ref[h] = jnp.exp(sc_a - top_a[0]) / zsum
        s2_ref[h] = sc_b
        v2_ref[h] = jnp.exp(sc_b - top_b[0])
        return carry

    lax.fori_loop(0, PEER_HEADS, head, 0)


def _peer_route(wq_t, xnt, sub_keys, tb):
    d, t = xnt.shape
    hq = wq_t.shape[0]
    key_blk = pl.BlockSpec((PEER_HEADS, PEER_NKEYS, tb), lambda i: (0, 0, i))
    key_shape = jax.ShapeDtypeStruct((PEER_HEADS, PEER_NKEYS, t), F32)
    return pl.pallas_call(
        _peer_route_kernel,
        grid=(t // tb,),
        in_specs=[
            pl.BlockSpec((hq, d), lambda i: (0, 0)),
            pl.BlockSpec((d, tb), lambda i: (0, i)),
            pl.BlockSpec(sub_keys.shape, lambda i: (0, 0, 0, 0)),
        ],
        out_specs=[key_blk, key_blk, key_blk, key_blk],
        out_shape=[key_shape, key_shape, key_shape, key_shape],
        scratch_shapes=[pltpu.VMEM((hq, tb), F32), pltpu.VMEM((_NCAND_PAD, tb), F32)],
        compiler_params=_cparams(("parallel",)),
        name="peer_route",
    )(wq_t, xnt, sub_keys)


SUBLANES = 8


def _peer_expert_kernel(xnt_ref, down_ref, upt_ref, thr_ref, w1_ref, s2_ref, v2_ref, x2_ref,
                        o_ref, acc_ref, a_ref, h_ref):
    e = pl.program_id(1)
    eb, tb = a_ref.shape
    rows_i = eb // PEER_NKEYS

    @pl.when(e == 0)
    def _():
        acc_ref[...] = jnp.zeros_like(acc_ref)

    a_ref[...] = _dot(down_ref[...], xnt_ref[...])

    for ii in range(rows_i):
        for lg in range(tb // LANES):
            ls = slice(lg * LANES, (lg + 1) * LANES)
            thr = [jnp.broadcast_to(thr_ref[h, ii:ii + 1, ls], (SUBLANES, LANES)) for h in range(PEER_HEADS)]
            w1 = [jnp.broadcast_to(w1_ref[h, ii:ii + 1, ls], (SUBLANES, LANES)) for h in range(PEER_HEADS)]
            for jg in range(PEER_NKEYS // (2 * SUBLANES)):
                halves = []
                for j0 in (2 * jg * SUBLANES, (2 * jg + 1) * SUBLANES):
                    js = slice(j0, j0 + SUBLANES)
                    gate = jnp.zeros((SUBLANES, LANES), F32)
                    for h in range(PEER_HEADS):
                        gate = gate + jnp.where(s2_ref[h, js, ls] >= thr[h], w1[h] * v2_ref[h, js, ls], 0.0)
                    rs = slice(ii * PEER_NKEYS + j0, ii * PEER_NKEYS + j0 + SUBLANES)
                    halves.append(gate * jax.nn.gelu(a_ref[rs, ls]))
                rs = slice(ii * PEER_NKEYS + 2 * jg * SUBLANES, ii * PEER_NKEYS + (2 * jg + 2) * SUBLANES)
                h_ref[rs, ls] = jnp.concatenate(halves, axis=0).astype(BF16)
    acc_ref[...] += _dot(upt_ref[...], h_ref[...])

    @pl.when(e == pl.num_programs(1) - 1)
    def _():
        o_ref[...] = x2_ref[...] + acc_ref[...].T


def _peer_experts(xnt, down, up_t, thr, w1, s2, v2, x2, tb, eb):
    d, t = xnt.shape
    ne = down.shape[0]
    rows_i = eb // PEER_NKEYS
    row_blk = pl.BlockSpec((PEER_HEADS, rows_i, tb), lambda i, e: (0, e, i))
    key_blk = pl.BlockSpec((PEER_HEADS, PEER_NKEYS, tb), lambda i, e: (0, 0, i))
    return pl.pallas_call(
        _peer_expert_kernel,
        grid=(t // tb, ne // eb),
        in_specs=[
            pl.BlockSpec((d, tb), lambda i, e: (0, i)),
            pl.BlockSpec((eb, d), lambda i, e: (e, 0)),
            pl.BlockSpec((d, eb), lambda i, e: (0, e)),
            row_blk, row_blk, key_blk, key_blk,
            pl.BlockSpec((tb, d), lambda i, e: (i, 0), pipeline_mode=pl.Buffered(1)),
        ],
        out_specs=pl.BlockSpec((tb, d), lambda i, e: (i, 0)),
        out_shape=jax.ShapeDtypeStruct((t, d), F32),
        scratch_shapes=[pltpu.VMEM((d, tb), F32), pltpu.VMEM((eb, tb), F32), pltpu.VMEM((eb, tb), BF16)],
        compiler_params=_cparams(("parallel", "arbitrary")),
        name="peer_experts",
    )(xnt, down, up_t, thr, w1, s2, v2, x2)


def _reorder_in_proj(w_in):
    qa, ka, va, gdf, gdb, ra, qd, kd, vd = jnp.split(w_in, [int(c) for c in np.cumsum(IN_SPLITS)[:-1]], axis=1)
    pad = jnp.zeros((w_in.shape[0], LANES - 2 * GLA_GATE_RANK), w_in.dtype)
    return jnp.concatenate([qd, kd, vd, qa, ka, va, ra, gdf, gdb, pad], axis=1).astype(BF16)


def _rope_tables(seq):
    half = ROPE_DIM // 2
    inv_freq = ROPE_THETA ** (-jnp.arange(half, dtype=F32) / half)
    ang = jnp.arange(seq, dtype=jnp.int32).astype(F32)[:, None] * inv_freq[None, :]
    cos = jnp.cos(ang)
    sin = jnp.sin(ang)
    zeros = jnp.zeros((seq, half), F32)
    rest = HEAD_DIM - ROPE_DIM
    c = jnp.concatenate([cos, cos, jnp.ones((seq, rest), F32)], axis=1)
    s1 = jnp.concatenate([-sin, zeros, jnp.zeros((seq, rest), F32)], axis=1)
    s2 = jnp.concatenate([zeros, sin, jnp.zeros((seq, rest), F32)], axis=1)
    return c, s1, s2


def _layer(x, norm1_g, w_in, up_f, bias_f, up_b, bias_b, out_g, q_norm_g, k_norm_g, w_out, norm2_g,
           w_query, sub_keys, down, up):
    bsz, seq, d = x.shape
    t = bsz * seq
    xf = x.reshape(t, d)

    h = _rms_matmul(xf, norm1_g.reshape(1, d), _reorder_in_proj(w_in), tm=512, tn=IN_W_PAD // 7)

    zpad = jnp.zeros((LANES, GLA_QK_W), F32)
    upf_pad = zpad.at[:GLA_GATE_RANK].set(up_f)
    upb_pad = zpad.at[GLA_GATE_RANK:2 * GLA_GATE_RANK].set(up_b)
    rc = min(256, seq)
    o_fwd = _gla(h, upf_pad, bias_f.reshape(1, -1), None, None, bsz, seq, rc, reverse=False)
    oa = _gla(h, upb_pad, bias_b.reshape(1, -1), o_fwd, out_g.reshape(1, -1), bsz, seq, rc, reverse=True)

    cos, sn1, sn2 = _rope_tables(seq)
    qn, kn, vn = _qk_prep(h, q_norm_g.reshape(1, -1), k_norm_g.reshape(1, -1), cos, sn1, sn2, bsz, seq, rc=256)
    outs, lses = [], []
    for window, dilation in DIL_PATTERNS:
        o_p, lse_p = _dilated(qn, kn, vn, bsz, seq, window, dilation, tq=256)
        outs.append(o_p)
        lses.append(lse_p)
    mixed = _combine(oa, outs, lses, rc=512)

    x2, xnt = _outproj(mixed, w_out.astype(BF16), xf, norm2_g.reshape(1, d), tm=512)

    thr, w1, s2, v2 = _peer_route(w_query.T.astype(BF16), xnt, sub_keys, tb=256)
    out = _peer_experts(xnt, down.astype(BF16), up.T.astype(BF16), thr, w1, s2, v2, x2, tb=512, eb=1024)
    return out.reshape(bsz, seq, d)


def kernel(x, norm1_g, w_in, gla_up_f, gla_bias_f, gla_up_b, gla_bias_b, gla_out_g, q_norm_g, k_norm_g, w_out,
           norm2_g, peer_w_query, peer_sub_keys, peer_down, peer_up):
    for l in range(norm1_g.shape[0]):
        x = _layer(x, norm1_g[l], w_in[l], gla_up_f[l], gla_bias_f[l], gla_up_b[l], gla_bias_b[l], gla_out_g[l],
                   q_norm_g[l], k_norm_g[l], w_out[l], norm2_g[l], peer_w_query[l], peer_sub_keys[l],
                   peer_down[l], peer_up[l])
    return x
```

```python
import functools

import jax
import jax.numpy as jnp
import numpy as np
from jax import lax
from jax.experimental import pallas as pl
from jax.experimental.pallas import tpu as pltpu

F32 = jnp.float32
BF16 = jnp.bfloat16

EPS = 1e-6
HEAD_DIM = 128

GLA_HEADS = 4
GLA_DK = 64
GLA_DV = 128
GLA_GATE_RANK = 16
GLA_TAU = 16.0
GLA_CHUNK = 64
GLA_QK_W = GLA_HEADS * GLA_DK
GLA_V_W = GLA_HEADS * GLA_DV

DIL_HEADS = 12
DIL_PATTERNS = ((128, 1), (512, 4), (2048, 16))
DIL_W = DIL_HEADS * HEAD_DIM
ROPE_THETA = 500000.0
ROPE_DIM = HEAD_DIM // 4
MASK_VALUE = -1e30

PEER_HEADS = 8
PEER_NKEYS = 128
PEER_QDIM = 256
PEER_TOPK = 16

MIX_W = GLA_V_W + DIL_W
IN_SPLITS = (GLA_QK_W, GLA_QK_W, GLA_V_W, GLA_GATE_RANK, GLA_GATE_RANK, GLA_V_W, DIL_W, DIL_W, DIL_W)

LANES = 128
COL_QD = 0
COL_KD = COL_QD + DIL_W
COL_VD = COL_KD + DIL_W
COL_QA = COL_VD + DIL_W
COL_KA = COL_QA + GLA_QK_W
COL_VA = COL_KA + GLA_QK_W
COL_RA = COL_VA + GLA_V_W
COL_GD = COL_RA + GLA_V_W
IN_W_PAD = COL_GD + LANES

VMEM_LIMIT = 56 * 1024 * 1024


def _cparams(sem):
    return pltpu.CompilerParams(dimension_semantics=sem, vmem_limit_bytes=VMEM_LIMIT)


def _split_bf16(a):
    hi = a.astype(BF16)
    lo = (a - hi.astype(F32)).astype(BF16)
    return hi, lo


def _dot(a, b):
    return jnp.dot(a, b, preferred_element_type=F32)


def _dot_nt(a, b):
    return lax.dot_general(a, b, (((1,), (1,)), ((), ())), preferred_element_type=F32)


def _dot_tn(a, b):
    return lax.dot_general(a, b, (((0,), (0,)), ((), ())), preferred_element_type=F32)


def _dot3(a, b):
    ah, al = _split_bf16(a)
    bh, bl = _split_bf16(b)
    return _dot(ah, bh) + (_dot(ah, bl) + _dot(al, bh))


def _rms_matmul_kernel(x_ref, g_ref, w_ref, o_ref, xn_ref):
    @pl.when(pl.program_id(1) == 0)
    def _():
        x = x_ref[...]
        ms = jnp.mean(x * x, axis=-1, keepdims=True)
        xn_ref[...] = (x * lax.rsqrt(ms + EPS) * g_ref[...]).astype(BF16)

    o_ref[...] = _dot(xn_ref[...], w_ref[...]).astype(o_ref.dtype)


def _rms_matmul(x, g, w, tm, tn):
    m, k = x.shape
    n = w.shape[1]
    return pl.pallas_call(
        _rms_matmul_kernel,
        grid=(m // tm, n // tn),
        in_specs=[
            pl.BlockSpec((tm, k), lambda i, j: (i, 0)),
            pl.BlockSpec((1, k), lambda i, j: (0, 0)),
            pl.BlockSpec((k, tn), lambda i, j: (0, j)),
        ],
        out_specs=pl.BlockSpec((tm, tn), lambda i, j: (i, j)),
        out_shape=jax.ShapeDtypeStruct((m, n), BF16),
        scratch_shapes=[pltpu.VMEM((tm, k), BF16)],
        compiler_params=_cparams(("parallel", "arbitrary")),
        name="rms_inproj",
    )(x, g, w)


def _log_sigmoid(z):
    return -(jnp.maximum(-z, 0.0) + jnp.log1p(jnp.exp(-jnp.abs(z))))


def _gla_kernel(*refs, reverse, finalize):
    if finalize:
        q_ref, k_ref, v_ref, gd_ref, up_ref, bias_ref, ofwd_ref, r_ref, og_ref, o_ref, st_ref = refs
    else:
        q_ref, k_ref, v_ref, gd_ref, up_ref, bias_ref, o_ref, st_ref = refs
    rc = q_ref.shape[0]
    nchunk = rc // GLA_CHUNK

    @pl.when(pl.program_id(1) == 0)
    def _():
        st_ref[...] = jnp.zeros_like(st_ref)

    up_hi, up_lo = _split_bf16(up_ref[...])
    z = _dot(gd_ref[...], up_hi) + _dot(gd_ref[...], up_lo) + bias_ref[...]
    la = _log_sigmoid(z) * (1.0 / GLA_TAU)
    row = lax.broadcasted_iota(jnp.int32, (rc, rc), 0)
    col = lax.broadcasted_iota(jnp.int32, (rc, rc), 1)
    same = (row // GLA_CHUNK) == (col // GLA_CHUNK)
    tri = same & ((col >= row) if reverse else (col <= row))
    la_hi, la_lo = _split_bf16(la)
    tri_b = tri.astype(BF16)
    same_b = same.astype(BF16)
    b = _dot(tri_b, la_hi) + _dot(tri_b, la_lo)
    tot = _dot(same_b, la_hi) + _dot(same_b, la_lo)
    q = q_ref[...].astype(F32)
    k = k_ref[...].astype(F32)
    q_in = (q * jnp.exp(b) * (GLA_DK ** -0.5)).astype(BF16)
    k_in = (k * jnp.exp(-b)).astype(BF16)
    k_st = (k * jnp.exp(tot - b)).astype(BF16)
    decay = jnp.exp(tot)
    v = v_ref[...].astype(BF16)

    r64 = lax.broadcasted_iota(jnp.int32, (GLA_CHUNK, GLA_CHUNK), 0)
    c64 = lax.broadcasted_iota(jnp.int32, (GLA_CHUNK, GLA_CHUNK), 1)
    mask = (c64 > r64) if reverse else (c64 <= r64)

    chunks = range(nchunk - 1, -1, -1) if reverse else range(nchunk)
    for c in chunks:
        rs = slice(c * GLA_CHUNK, (c + 1) * GLA_CHUNK)
        for h in range(GLA_HEADS):
            ks = slice(h * GLA_DK, (h + 1) * GLA_DK)
            vs = slice(h * GLA_DV, (h + 1) * GLA_DV)
            qc = q_in[rs, ks]
            vc = v[rs, vs]
            sc = _dot_nt(qc, k_in[rs, ks])
            p = jnp.where(mask, sc, 0.0).astype(BF16)
            st = st_ref[h]
            o = _dot(p, vc) + _dot_nt(qc, st.astype(BF16))
            inc = _dot_tn(vc, k_st[rs, ks])
            st_ref[h] = decay[c * GLA_CHUNK:c * GLA_CHUNK + 1, ks] * st + inc
            if finalize:
                o = o + ofwd_ref[rs, vs]
                y = o * lax.rsqrt(jnp.mean(o * o, axis=-1, keepdims=True) + EPS) * og_ref[:, vs]
                rr = r_ref[rs, vs].astype(F32)
                o = y * (rr * jax.nn.sigmoid(rr))
            o_ref[rs, vs] = o


def _gla(h, up_pad, bias, ofwd, out_g, bsz, seq, rc, reverse):
    nsteps = seq // rc
    finalize = ofwd is not None

    def rowblk(b, n):
        return b * nsteps + ((nsteps - 1 - n) if reverse else n)

    in_specs = [
        pl.BlockSpec((rc, GLA_QK_W), lambda b, n: (rowblk(b, n), COL_QA // GLA_QK_W)),
        pl.BlockSpec((rc, GLA_QK_W), lambda b, n: (rowblk(b, n), COL_KA // GLA_QK_W)),
        pl.BlockSpec((rc, GLA_V_W), lambda b, n: (rowblk(b, n), COL_VA // GLA_V_W)),
        pl.BlockSpec((rc, LANES), lambda b, n: (rowblk(b, n), COL_GD // LANES)),
        pl.BlockSpec((LANES, GLA_QK_W), lambda b, n: (0, 0)),
        pl.BlockSpec((1, GLA_QK_W), lambda b, n: (0, 0)),
    ]
    args = [h, h, h, h, up_pad, bias]
    if finalize:
        in_specs += [
            pl.BlockSpec((rc, GLA_V_W), lambda b, n: (rowblk(b, n), 0)),
            pl.BlockSpec((rc, GLA_V_W), lambda b, n: (rowblk(b, n), COL_RA // GLA_V_W)),
            pl.BlockSpec((1, GLA_V_W), lambda b, n: (0, 0)),
        ]
        args += [ofwd, h, out_g]
    return pl.pallas_call(
        functools.partial(_gla_kernel, reverse=reverse, finalize=finalize),
        grid=(bsz, nsteps),
        in_specs=in_specs,
        out_specs=pl.BlockSpec((rc, GLA_V_W), lambda b, n: (rowblk(b, n), 0)),
        out_shape=jax.ShapeDtypeStruct((bsz * seq, GLA_V_W), F32),
        scratch_shapes=[pltpu.VMEM((GLA_HEADS, GLA_DV, GLA_DK), F32)],
        compiler_params=_cparams(("parallel", "arbitrary")),
        name="gla_bwd" if reverse else "gla_fwd",
    )(*args)


def _qk_prep_kernel(q_ref, k_ref, v_ref, qg_ref, kg_ref, c_ref, s1_ref, s2_ref, qo_ref, ko_ref, vo_ref):
    cos = c_ref[...]
    sn1 = s1_ref[...]
    sn2 = s2_ref[...]

    def norm_rope(x, g):
        y = x * lax.rsqrt(jnp.mean(x * x, axis=-1, keepdims=True) + EPS) * g
        return (y * cos
                + pltpu.roll(y, HEAD_DIM - ROPE_DIM // 2, axis=1) * sn1
                + pltpu.roll(y, ROPE_DIM // 2, axis=1) * sn2)

    for h in range(DIL_HEADS):
        hs = slice(h * HEAD_DIM, (h + 1) * HEAD_DIM)
        qo_ref[:, hs] = (norm_rope(q_ref[:, hs].astype(F32), qg_ref[...]) * (HEAD_DIM ** -0.5)).astype(BF16)
        ko_ref[:, hs] = norm_rope(k_ref[:, hs].astype(F32), kg_ref[...]).astype(BF16)
    vo_ref[...] = v_ref[...].astype(BF16)


def _qk_prep(h, qg, kg, cos, sn1, sn2, bsz, seq, rc):
    nsteps = seq // rc
    t = bsz * seq
    hblk = lambda c: pl.BlockSpec((rc, DIL_W), lambda b, n: (b * nsteps + n, c // DIL_W))
    tab = pl.BlockSpec((rc, HEAD_DIM), lambda b, n: (n, 0))
    gsp = pl.BlockSpec((1, HEAD_DIM), lambda b, n: (0, 0))
    osp = pl.BlockSpec((rc, DIL_W), lambda b, n: (b * nsteps + n, 0))
    oshape = jax.ShapeDtypeStruct((t, DIL_W), BF16)
    return pl.pallas_call(
        _qk_prep_kernel,
        grid=(bsz, nsteps),
        in_specs=[hblk(COL_QD), hblk(COL_KD), hblk(COL_VD), gsp, gsp, tab, tab, tab],
        out_specs=[osp, osp, osp],
        out_shape=[oshape, oshape, oshape],
        compiler_params=_cparams(("parallel", "parallel")),
        name="qk_prep",
    )(h, h, h, qg, kg, cos, sn1, sn2)


def _dilated_kernel(q_ref, km_ref, kl_ref, kr_ref, vm_ref, vl_ref, vr_ref, o_ref, lse_ref, *, half):
    n = pl.program_id(2)
    last = pl.num_programs(2) - 1
    tq = q_ref.shape[1]
    r_m = lax.broadcasted_iota(jnp.int32, (tq, tq), 0)
    c_m = lax.broadcasted_iota(jnp.int32, (tq, tq), 1)
    valid_m = jnp.abs(c_m - r_m) <= half
    r_h = lax.broadcasted_iota(jnp.int32, (tq, half), 0)
    c_h = lax.broadcasted_iota(jnp.int32, (tq, half), 1)
    valid_l = (c_h >= r_h) & (n > 0)
    valid_r = (c_h <= r_h - (tq - half)) & (n < last)
    lane = lax.broadcasted_iota(jnp.int32, (tq, LANES), 1)
    lse_tile = jnp.zeros((tq, LANES), F32)
    for h in range(DIL_HEADS):
        hs = slice(h * HEAD_DIM, (h + 1) * HEAD_DIM)
        q = q_ref[0, :, hs]
        s_m = jnp.where(valid_m, _dot_nt(q, km_ref[0, :, hs]), MASK_VALUE)
        s_l = jnp.where(valid_l, _dot_nt(q, kl_ref[0, :, hs]), MASK_VALUE)
        s_r = jnp.where(valid_r, _dot_nt(q, kr_ref[0, :, hs]), MASK_VALUE)
        m = jnp.maximum(jnp.max(s_m, axis=-1, keepdims=True),
                        jnp.maximum(jnp.max(s_l, axis=-1, keepdims=True), jnp.max(s_r, axis=-1, keepdims=True)))
        p_m = jnp.exp(s_m - m)
        p_l = jnp.exp(s_l - m)
        p_r = jnp.exp(s_r - m)
        den = (jnp.sum(p_m, axis=-1, keepdims=True) + jnp.sum(p_l, axis=-1, keepdims=True)
               + jnp.sum(p_r, axis=-1, keepdims=True))
        o = (_dot(p_m.astype(BF16), vm_ref[0, :, hs]) + _dot(p_l.astype(BF16), vl_ref[0, :, hs])
             + _dot(p_r.astype(BF16), vr_ref[0, :, hs]))
        o_ref[0, :, hs] = (o / den).astype(o_ref.dtype)
        lse_tile = jnp.where(lane == h, m + jnp.log(den), lse_tile)
    lse_ref[0] = lse_tile


def _dilated(q, k, v, bsz, seq, window, dilation, tq):
    half = window // (2 * dilation)
    ln = seq // dilation
    nt = ln // tq
    per = tq // half
    nh = ln // half
    qv = q.reshape(bsz, ln, dilation * DIL_W)
    kv = k.reshape(bsz, ln, dilation * DIL_W)
    vv = v.reshape(bsz, ln, dilation * DIL_W)
    main = pl.BlockSpec((1, tq, DIL_W), lambda b, r, n: (b, n, r))
    left = pl.BlockSpec((1, half, DIL_W), lambda b, r, n: (b, jnp.maximum(n * per - 1, 0), r))
    right = pl.BlockSpec((1, half, DIL_W), lambda b, r, n: (b, jnp.minimum((n + 1) * per, nh - 1), r))
    o, lse = pl.pallas_call(
        functools.partial(_dilated_kernel, half=half),
        grid=(bsz, dilation, nt),
        in_specs=[main, main, left, right, main, left, right],
        out_specs=[main, pl.BlockSpec((1, tq, LANES), lambda b, r, n: (b, n, r))],
        out_shape=[jax.ShapeDtypeStruct((bsz, ln, dilation * DIL_W), BF16),
                   jax.ShapeDtypeStruct((bsz, ln, dilation * LANES), F32)],
        compiler_params=_cparams(("parallel", "parallel", "parallel")),
        name=f"dilated_d{dilation}",
    )(qv, kv, kv, kv, vv, vv, vv)
    return o.reshape(bsz * seq, DIL_W), lse.reshape(bsz * seq, LANES)


def _combine_kernel(oa_ref, o1_ref, o2_ref, o3_ref, l1_ref, l2_ref, l3_ref, out_ref):
    out_ref[:, :GLA_V_W] = oa_ref[...].astype(BF16)
    l1 = l1_ref[...]
    l2 = l2_ref[...]
    l3 = l3_ref[...]
    mx = jnp.maximum(l1, jnp.maximum(l2, l3))
    e1 = jnp.exp(l1 - mx)
    e2 = jnp.exp(l2 - mx)
    e3 = jnp.exp(l3 - mx)
    inv = 1.0 / (e1 + e2 + e3)
    w1 = e1 * inv
    w2 = e2 * inv
    w3 = e3 * inv
    rows = oa_ref.shape[0]
    for h in range(DIL_HEADS):
        hs = slice(h * HEAD_DIM, (h + 1) * HEAD_DIM)
        bc = lambda w: jnp.broadcast_to(w[:, h:h + 1], (rows, HEAD_DIM))
        o = (bc(w1) * o1_ref[:, hs].astype(F32) + bc(w2) * o2_ref[:, hs].astype(F32)
             + bc(w3) * o3_ref[:, hs].astype(F32))
        out_ref[:, GLA_V_W + h * HEAD_DIM:GLA_V_W + (h + 1) * HEAD_DIM] = o.astype(BF16)


def _combine(oa, os, ls, rc):
    t = oa.shape[0]
    rb = lambda w: pl.BlockSpec((rc, w), lambda i: (i, 0))
    return pl.pallas_call(
        _combine_kernel,
        grid=(t // rc,),
        in_specs=[rb(GLA_V_W), rb(DIL_W), rb(DIL_W), rb(DIL_W), rb(LANES), rb(LANES), rb(LANES)],
        out_specs=rb(MIX_W),
        out_shape=jax.ShapeDtypeStruct((t, MIX_W), BF16),
        compiler_params=_cparams(("parallel",)),
        name="combine",
    )(oa, *os, *ls)


def _outproj_kernel(mix_ref, w_ref, x_ref, g_ref, x2_ref, xnt_ref):
    x2 = x_ref[...] + _dot(mix_ref[...], w_ref[...])
    x2_ref[...] = x2
    xn = x2 * lax.rsqrt(jnp.mean(x2 * x2, axis=-1, keepdims=True) + EPS) * g_ref[...]
    xnt_ref[...] = xn.T.astype(BF16)


def _outproj(mixed, w_out, x, g2, tm):
    t, d = x.shape
    return pl.pallas_call(
        _outproj_kernel,
        grid=(t // tm,),
        in_specs=[
            pl.BlockSpec((tm, MIX_W), lambda i: (i, 0)),
            pl.BlockSpec((MIX_W, d), lambda i: (0, 0)),
            pl.BlockSpec((tm, d), lambda i: (i, 0)),
            pl.BlockSpec((1, d), lambda i: (0, 0)),
        ],
        out_specs=[pl.BlockSpec((tm, d), lambda i: (i, 0)), pl.BlockSpec((d, tm), lambda i: (0, i))],
        out_shape=[jax.ShapeDtypeStruct((t, d), F32), jax.ShapeDtypeStruct((d, t), BF16)],
        compiler_params=_cparams(("parallel",)),
        name="outproj",
    )(mixed, w_out, x, g2)


_NTOP = PEER_TOPK + 1
_CAND = [(a, b) for a in range(_NTOP) for b in range(_NTOP) if (a + 1) * (b + 1) <= _NTOP]
_NCAND_PAD = -(-len(_CAND) // 8) * 8


def _top_values(sc, count):
    vals = []
    for _ in range(count):
        m = jnp.max(sc, axis=0, keepdims=True)
        vals.append(m)
        sc = jnp.where(sc >= m, -jnp.inf, sc)
    return vals


def _peer_route_kernel(wq_ref, xnt_ref, sk_ref, thr_ref, w1_ref, v2_ref, q_ref, cand_ref):
    q_ref[...] = _dot(wq_ref[...], xnt_ref[...])
    half = PEER_QDIM // 2
    cand_ref[...] = jnp.full(cand_ref.shape, -jnp.inf, F32)

    def head(h, carry):
        sc_a = _dot3(sk_ref[h, 0], q_ref[pl.ds(pl.multiple_of(h * PEER_QDIM, PEER_QDIM), half), :])
        sc_b = _dot3(sk_ref[h, 1], q_ref[pl.ds(pl.multiple_of(h * PEER_QDIM + half, half), half), :])
        top_a = _top_values(sc_a, _NTOP)
        top_b = _top_values(sc_b, _NTOP)
        for i, (a, b) in enumerate(_CAND):
            cand_ref[i:i + 1, :] = top_a[a] + top_b[b]
        cand = cand_ref[...]
        top_c = _top_values(cand, _NTOP)
        tau = 0.5 * (top_c[PEER_TOPK - 1] + top_c[PEER_TOPK])
        best = top_a[0] + top_b[0]
        zsum = jnp.sum(jnp.where(cand > tau, jnp.exp(cand - best), 0.0), axis=0, keepdims=True)
        thr_ref[h] = jnp.exp((tau - sc_a) - top_b[0])
        w1_ref[h] = jnp.exp(sc_a - top_a[0]) / zsum
        v2_ref[h] = jnp.exp(sc_b - top_b[0])
        return carry

    lax.fori_loop(0, PEER_HEADS, head, 0)


def _peer_route(wq_t, xnt, sub_keys, tb):
    d, t = xnt.shape
    hq = wq_t.shape[0]
    key_blk = pl.BlockSpec((PEER_HEADS, PEER_NKEYS, tb), lambda i: (0, 0, i))
    key_shape = jax.ShapeDtypeStruct((PEER_HEADS, PEER_NKEYS, t), F32)
    return pl.pallas_call(
        _peer_route_kernel,
        grid=(t // tb,),
        in_specs=[
            pl.BlockSpec((hq, d), lambda i: (0, 0)),
            pl.BlockSpec((d, tb), lambda i: (0, i)),
            pl.BlockSpec(sub_keys.shape, lambda i: (0, 0, 0, 0)),
        ],
        out_specs=[key_blk, key_blk, key_blk],
        out_shape=[key_shape, key_shape, key_shape],
        scratch_shapes=[pltpu.VMEM((hq, tb), F32), pltpu.VMEM((_NCAND_PAD, tb), F32)],
        compiler_params=_cparams(("parallel",)),
        name="peer_route",
    )(wq_t, xnt, sub_keys)


SUBLANES = 8


def _peer_expert_kernel(xnt_ref, down_ref, upt_ref, thr_ref, w1_ref, v2_ref, x2_ref,
                        o_ref, acc_ref, a_ref, g_ref, h_ref):
    e = pl.program_id(1)
    eb, tb = a_ref.shape
    rows_i = eb // PEER_NKEYS

    @pl.when(e == 0)
    def _():
        acc_ref[...] = jnp.zeros_like(acc_ref)

    chunk_i = 2
    for ii in range(rows_i):
        if ii % chunk_i == 0:
            cs = slice(ii * PEER_NKEYS, (ii + chunk_i) * PEER_NKEYS)
            a_ref[cs, :] = _dot(down_ref[cs, :], xnt_ref[...])
        for lg in range(tb // LANES):
            ls = slice(lg * LANES, (lg + 1) * LANES)
            thr = [jnp.broadcast_to(thr_ref[h, ii:ii + 1, ls], (SUBLANES, LANES)) for h in range(PEER_HEADS)]
            w1 = [jnp.broadcast_to(w1_ref[h, ii:ii + 1, ls], (SUBLANES, LANES)) for h in range(PEER_HEADS)]
            for j0 in range(0, PEER_NKEYS, SUBLANES):
                js = slice(j0, j0 + SUBLANES)
                gate = jnp.zeros((SUBLANES, LANES), F32)
                for h in range(PEER_HEADS):
                    v2 = v2_ref[h, js, ls]
                    gate = gate + jnp.where(v2 >= thr[h], w1[h] * v2, 0.0)
                g_ref[ii * PEER_NKEYS + j0:ii * PEER_NKEYS + j0 + SUBLANES, ls] = gate

    rows = 2 * SUBLANES
    for r0 in range(0, eb, rows):
        h_ref[r0:r0 + rows, :] = (g_ref[r0:r0 + rows, :] * jax.nn.gelu(a_ref[r0:r0 + rows, :])).astype(BF16)
    acc_ref[...] += _dot(upt_ref[...], h_ref[...])

    @pl.when(e == pl.num_programs(1) - 1)
    def _():
        o_ref[...] = x2_ref[...] + acc_ref[...].T


def _peer_experts(xnt, down, up_t, thr, w1, v2, x2, tb, eb):
    d, t = xnt.shape
    ne = down.shape[0]
    rows_i = eb // PEER_NKEYS
    row_blk = pl.BlockSpec((PEER_HEADS, rows_i, tb), lambda i, e: (0, e, i))
    key_blk = pl.BlockSpec((PEER_HEADS, PEER_NKEYS, tb), lambda i, e: (0, 0, i))
    return pl.pallas_call(
        _peer_expert_kernel,
        grid=(t // tb, ne // eb),
        in_specs=[
            pl.BlockSpec((d, tb), lambda i, e: (0, i)),
            pl.BlockSpec((eb, d), lambda i, e: (e, 0)),
            pl.BlockSpec((d, eb), lambda i, e: (0, e)),
            row_blk, row_blk, key_blk,
            pl.BlockSpec((tb, d), lambda i, e: (i, 0), pipeline_mode=pl.Buffered(1)),
        ],
        out_specs=pl.BlockSpec((tb, d), lambda i, e: (i, 0)),
        out_shape=jax.ShapeDtypeStruct((t, d), F32),
        scratch_shapes=[pltpu.VMEM((d, tb), F32), pltpu.VMEM((eb, tb), F32), pltpu.VMEM((eb, tb), F32),
                        pltpu.VMEM((eb, tb), BF16)],
        compiler_params=_cparams(("parallel", "arbitrary")),
        name="peer_experts",
    )(xnt, down, up_t, thr, w1, v2, x2)


def _reorder_in_proj(w_in):
    qa, ka, va, gdf, gdb, ra, qd, kd, vd = jnp.split(w_in, [int(c) for c in np.cumsum(IN_SPLITS)[:-1]], axis=1)
    pad = jnp.zeros((w_in.shape[0], LANES - 2 * GLA_GATE_RANK), w_in.dtype)
    return jnp.concatenate([qd, kd, vd, qa, ka, va, ra, gdf, gdb, pad], axis=1).astype(BF16)


def _rope_tables(seq):
    half = ROPE_DIM // 2
    inv_freq = ROPE_THETA ** (-jnp.arange(half, dtype=F32) / half)
    ang = jnp.arange(seq, dtype=jnp.int32).astype(F32)[:, None] * inv_freq[None, :]
    cos = jnp.cos(ang)
    sin = jnp.sin(ang)
    zeros = jnp.zeros((seq, half), F32)
    rest = HEAD_DIM - ROPE_DIM
    c = jnp.concatenate([cos, cos, jnp.ones((seq, rest), F32)], axis=1)
    s1 = jnp.concatenate([-sin, zeros, jnp.zeros((seq, rest), F32)], axis=1)
    s2 = jnp.concatenate([zeros, sin, jnp.zeros((seq, rest), F32)], axis=1)
    return c, s1, s2


def _layer(x, norm1_g, w_in, up_f, bias_f, up_b, bias_b, out_g, q_norm_g, k_norm_g, w_out, norm2_g,
           w_query, sub_keys, down, up):
    bsz, seq, d = x.shape
    t = bsz * seq
    xf = x.reshape(t, d)

    h = _rms_matmul(xf, norm1_g.reshape(1, d), _reorder_in_proj(w_in), tm=1024, tn=IN_W_PAD // 7)

    zpad = jnp.zeros((LANES, GLA_QK_W), F32)
    upf_pad = zpad.at[:GLA_GATE_RANK].set(up_f)
    upb_pad = zpad.at[GLA_GATE_RANK:2 * GLA_GATE_RANK].set(up_b)
    rc = min(256, seq)
    o_fwd = _gla(h, upf_pad, bias_f.reshape(1, -1), None, None, bsz, seq, rc, reverse=False)
    oa = _gla(h, upb_pad, bias_b.reshape(1, -1), o_fwd, out_g.reshape(1, -1), bsz, seq, rc, reverse=True)

    cos, sn1, sn2 = _rope_tables(seq)
    qn, kn, vn = _qk_prep(h, q_norm_g.reshape(1, -1), k_norm_g.reshape(1, -1), cos, sn1, sn2, bsz, seq, rc=256)
    outs, lses = [], []
    for window, dilation in DIL_PATTERNS:
        o_p, lse_p = _dilated(qn, kn, vn, bsz, seq, window, dilation, tq=256)
        outs.append(o_p)
        lses.append(lse_p)
    mixed = _combine(oa, outs, lses, rc=512)

    x2, xnt = _outproj(mixed, w_out.astype(BF16), xf, norm2_g.reshape(1, d), tm=512)

    thr, w1, v2 = _peer_route(w_query.T.astype(BF16), xnt, sub_keys, tb=256)
    out = _peer_experts(xnt, down.astype(BF16), up.T.astype(BF16), thr, w1, v2, x2, tb=512, eb=1024)
    return out.reshape(bsz, seq, d)


def kernel(x, norm1_g, w_in, gla_up_f, gla_bias_f, gla_up_b, gla_bias_b, gla_out_g, q_norm_g, k_norm_g, w_out,
           norm2_g, peer_w_query, peer_sub_keys, peer_down, peer_up):
    for l in range(norm1_g.shape[0]):
        x = _layer(x, norm1_g[l], w_in[l], gla_up_f[l], gla_bias_f[l], gla_up_b[l], gla_bias_b[l], gla_out_g[l],
                   q_norm_g[l], k_norm_g[l], w_out[l], norm2_g[l], peer_w_query[l], peer_sub_keys[l],
                   peer_down[l], peer_up[l])
    return x
```

```python
import functools

import jax
import jax.numpy as jnp
import numpy as np
from jax import lax
from jax.experimental import pallas as pl
from jax.experimental.pallas import tpu as pltpu

F32 = jnp.float32
BF16 = jnp.bfloat16

EPS = 1e-6
HEAD_DIM = 128

GLA_HEADS = 4
GLA_DK = 64
GLA_DV = 128
GLA_GATE_RANK = 16
GLA_TAU = 16.0
GLA_CHUNK = 64
GLA_QK_W = GLA_HEADS * GLA_DK
GLA_V_W = GLA_HEADS * GLA_DV

DIL_HEADS = 12
DIL_PATTERNS = ((128, 1), (512, 4), (2048, 16))
DIL_W = DIL_HEADS * HEAD_DIM
ROPE_THETA = 500000.0
ROPE_DIM = HEAD_DIM // 4
MASK_VALUE = -1e30

PEER_HEADS = 8
PEER_NKEYS = 128
PEER_QDIM = 256
PEER_TOPK = 16

MIX_W = GLA_V_W + DIL_W
IN_SPLITS = (GLA_QK_W, GLA_QK_W, GLA_V_W, GLA_GATE_RANK, GLA_GATE_RANK, GLA_V_W, DIL_W, DIL_W, DIL_W)

LANES = 128
COL_QD = 0
COL_KD = COL_QD + DIL_W
COL_VD = COL_KD + DIL_W
COL_QA = COL_VD + DIL_W
COL_KA = COL_QA + GLA_QK_W
COL_VA = COL_KA + GLA_QK_W
COL_RA = COL_VA + GLA_V_W
COL_GD = COL_RA + GLA_V_W
IN_W_PAD = COL_GD + LANES

VMEM_LIMIT = 56 * 1024 * 1024


def _cparams(sem):
    return pltpu.CompilerParams(dimension_semantics=sem, vmem_limit_bytes=VMEM_LIMIT)


def _split_bf16(a):
    hi = a.astype(BF16)
    lo = (a - hi.astype(F32)).astype(BF16)
    return hi, lo


def _dot(a, b):
    return jnp.dot(a, b, preferred_element_type=F32)


def _dot_nt(a, b):
    return lax.dot_general(a, b, (((1,), (1,)), ((), ())), preferred_element_type=F32)


def _dot_tn(a, b):
    return lax.dot_general(a, b, (((0,), (0,)), ((), ())), preferred_element_type=F32)


def _dot3(a, b):
    ah, al = _split_bf16(a)
    bh, bl = _split_bf16(b)
    return _dot(ah, bh) + (_dot(ah, bl) + _dot(al, bh))


def _rms_matmul_kernel(x_ref, g_ref, w_ref, o_ref, xn_ref):
    @pl.when(pl.program_id(1) == 0)
    def _():
        x = x_ref[...]
        ms = jnp.mean(x * x, axis=-1, keepdims=True)
        xn_ref[...] = (x * lax.rsqrt(ms + EPS) * g_ref[...]).astype(BF16)

    o_ref[...] = _dot(xn_ref[...], w_ref[...]).astype(o_ref.dtype)


def _rms_matmul(x, g, w, tm, tn):
    m, k = x.shape
    n = w.shape[1]
    return pl.pallas_call(
        _rms_matmul_kernel,
        grid=(m // tm, n // tn),
        in_specs=[
            pl.BlockSpec((tm, k), lambda i, j: (i, 0)),
            pl.BlockSpec((1, k), lambda i, j: (0, 0)),
            pl.BlockSpec((k, tn), lambda i, j: (0, j)),
        ],
        out_specs=pl.BlockSpec((tm, tn), lambda i, j: (i, j)),
        out_shape=jax.ShapeDtypeStruct((m, n), BF16),
        scratch_shapes=[pltpu.VMEM((tm, k), BF16)],
        compiler_params=_cparams(("parallel", "arbitrary")),
        name="rms_inproj",
    )(x, g, w)


def _log_sigmoid(z):
    return -(jnp.maximum(-z, 0.0) + jnp.log1p(jnp.exp(-jnp.abs(z))))


def _gla_kernel(*refs, reverse, finalize):
    if finalize:
        q_ref, k_ref, v_ref, gd_ref, up_ref, bias_ref, ofwd_ref, r_ref, og_ref, o_ref, st_ref = refs
    else:
        q_ref, k_ref, v_ref, gd_ref, up_ref, bias_ref, o_ref, st_ref = refs
    rc = q_ref.shape[0]
    nchunk = rc // GLA_CHUNK

    @pl.when(pl.program_id(1) == 0)
    def _():
        st_ref[...] = jnp.zeros_like(st_ref)

    up_hi, up_lo = _split_bf16(up_ref[...])
    z = _dot(gd_ref[...], up_hi) + _dot(gd_ref[...], up_lo) + bias_ref[...]
    la = _log_sigmoid(z) * (1.0 / GLA_TAU)
    row = lax.broadcasted_iota(jnp.int32, (rc, rc), 0)
    col = lax.broadcasted_iota(jnp.int32, (rc, rc), 1)
    same = (row // GLA_CHUNK) == (col // GLA_CHUNK)
    tri = same & ((col >= row) if reverse else (col <= row))
    la_hi, la_lo = _split_bf16(la)
    tri_b = tri.astype(BF16)
    same_b = same.astype(BF16)
    b = _dot(tri_b, la_hi) + _dot(tri_b, la_lo)
    tot = _dot(same_b, la_hi) + _dot(same_b, la_lo)
    q = q_ref[...].astype(F32)
    k = k_ref[...].astype(F32)
    q_in = (q * jnp.exp(b) * (GLA_DK ** -0.5)).astype(BF16)
    k_in = (k * jnp.exp(-b)).astype(BF16)
    k_st = (k * jnp.exp(tot - b)).astype(BF16)
    decay = jnp.exp(tot)
    v = v_ref[...].astype(BF16)

    r64 = lax.broadcasted_iota(jnp.int32, (GLA_CHUNK, GLA_CHUNK), 0)
    c64 = lax.broadcasted_iota(jnp.int32, (GLA_CHUNK, GLA_CHUNK), 1)
    mask = (c64 > r64) if reverse else (c64 <= r64)

    chunks = range(nchunk - 1, -1, -1) if reverse else range(nchunk)
    for c in chunks:
        rs = slice(c * GLA_CHUNK, (c + 1) * GLA_CHUNK)
        for h in range(GLA_HEADS):
            ks = slice(h * GLA_DK, (h + 1) * GLA_DK)
            vs = slice(h * GLA_DV, (h + 1) * GLA_DV)
            qc = q_in[rs, ks]
            vc = v[rs, vs]
            sc = _dot_nt(qc, k_in[rs, ks])
            p = jnp.where(mask, sc, 0.0).astype(BF16)
            st = st_ref[h]
            o = _dot(p, vc) + _dot_nt(qc, st.astype(BF16))
            inc = _dot_tn(vc, k_st[rs, ks])
            st_ref[h] = decay[c * GLA_CHUNK:c * GLA_CHUNK + 1, ks] * st + inc
            if finalize:
                o = o + ofwd_ref[rs, vs]
                y = o * lax.rsqrt(jnp.mean(o * o, axis=-1, keepdims=True) + EPS) * og_ref[:, vs]
                rr = r_ref[rs, vs].astype(F32)
                o = y * (rr * jax.nn.sigmoid(rr))
            o_ref[rs, vs] = o


def _gla(h, up_pad, bias, ofwd, out_g, bsz, seq, rc, reverse):
    nsteps = seq // rc
    finalize = ofwd is not None

    def rowblk(b, n):
        return b * nsteps + ((nsteps - 1 - n) if reverse else n)

    in_specs = [
        pl.BlockSpec((rc, GLA_QK_W), lambda b, n: (rowblk(b, n), COL_QA // GLA_QK_W)),
        pl.BlockSpec((rc, GLA_QK_W), lambda b, n: (rowblk(b, n), COL_KA // GLA_QK_W)),
        pl.BlockSpec((rc, GLA_V_W), lambda b, n: (rowblk(b, n), COL_VA // GLA_V_W)),
        pl.BlockSpec((rc, LANES), lambda b, n: (rowblk(b, n), COL_GD // LANES)),
        pl.BlockSpec((LANES, GLA_QK_W), lambda b, n: (0, 0)),
        pl.BlockSpec((1, GLA_QK_W), lambda b, n: (0, 0)),
    ]
    args = [h, h, h, h, up_pad, bias]
    if finalize:
        in_specs += [
            pl.BlockSpec((rc, GLA_V_W), lambda b, n: (rowblk(b, n), 0)),
            pl.BlockSpec((rc, GLA_V_W), lambda b, n: (rowblk(b, n), COL_RA // GLA_V_W)),
            pl.BlockSpec((1, GLA_V_W), lambda b, n: (0, 0)),
        ]
        args += [ofwd, h, out_g]
    return pl.pallas_call(
        functools.partial(_gla_kernel, reverse=reverse, finalize=finalize),
        grid=(bsz, nsteps),
        in_specs=in_specs,
        out_specs=pl.BlockSpec((rc, GLA_V_W), lambda b, n: (rowblk(b, n), 0)),
        out_shape=jax.ShapeDtypeStruct((bsz * seq, GLA_V_W), F32),
        scratch_shapes=[pltpu.VMEM((GLA_HEADS, GLA_DV, GLA_DK), F32)],
        compiler_params=_cparams(("parallel", "arbitrary")),
        name="gla_bwd" if reverse else "gla_fwd",
    )(*args)


DILATIONS = tuple(d for _, d in DIL_PATTERNS)


def _qk_prep_kernel(q_ref, k_ref, v_ref, qg_ref, kg_ref, c_ref, s1_ref, s2_ref, *rest):
    nd = len(DILATIONS)
    outs = rest[:3 * nd]
    scr = rest[3 * nd:]
    rc = q_ref.shape[0]
    cos = c_ref[...]
    sn1 = s1_ref[...]
    sn2 = s2_ref[...]

    def norm_rope(x, g):
        y = x * lax.rsqrt(jnp.mean(x * x, axis=-1, keepdims=True) + EPS) * g
        return (y * cos
                + pltpu.roll(y, HEAD_DIM - ROPE_DIM // 2, axis=1) * sn1
                + pltpu.roll(y, ROPE_DIM // 2, axis=1) * sn2)

    for h in range(DIL_HEADS):
        hs = slice(h * HEAD_DIM, (h + 1) * HEAD_DIM)
        scr[0][h] = norm_rope(q_ref[:, hs].astype(F32), qg_ref[...]) * (HEAD_DIM ** -0.5)
        scr[1][h] = norm_rope(k_ref[:, hs].astype(F32), kg_ref[...])
        scr[2][h] = v_ref[:, hs].astype(F32)
    for which in range(3):
        for di, d in enumerate(DILATIONS):
            o_ref = outs[which * nd + di]
            for h in range(DIL_HEADS):
                for r in range(d):
                    rows = pl.ds(r, rc // d, stride=d) if d > 1 else slice(None)
                    c0 = r * DIL_W + h * HEAD_DIM
                    o_ref[0, :, c0:c0 + HEAD_DIM] = scr[which].at[h][rows, :].astype(BF16)


def _qk_prep(h, qg, kg, cos, sn1, sn2, bsz, seq, rc):
    nsteps = seq // rc
    hblk = lambda c: pl.BlockSpec((rc, DIL_W), lambda b, n: (b * nsteps + n, c // DIL_W))
    tab = pl.BlockSpec((rc, HEAD_DIM), lambda b, n: (n, 0))
    gsp = pl.BlockSpec((1, HEAD_DIM), lambda b, n: (0, 0))
    osp = [pl.BlockSpec((1, rc // d, d * DIL_W), lambda b, n: (b, n, 0)) for d in DILATIONS]
    oshape = [jax.ShapeDtypeStruct((bsz, seq // d, d * DIL_W), BF16) for d in DILATIONS]
    outs = pl.pallas_call(
        _qk_prep_kernel,
        grid=(bsz, nsteps),
        in_specs=[hblk(COL_QD), hblk(COL_KD), hblk(COL_VD), gsp, gsp, tab, tab, tab],
        out_specs=osp * 3,
        out_shape=oshape * 3,
        scratch_shapes=[pltpu.VMEM((DIL_HEADS, rc, HEAD_DIM), F32)] * 3,
        compiler_params=_cparams(("parallel", "parallel")),
        name="qk_prep",
    )(h, h, h, qg, kg, cos, sn1, sn2)
    nd = len(DILATIONS)
    return outs[:nd], outs[nd:2 * nd], outs[2 * nd:]


def _dilated_kernel(q_ref, km_ref, kl_ref, kr_ref, vm_ref, vl_ref, vr_ref, o_ref, lse_ref, *, half):
    n = pl.program_id(2)
    last = pl.num_programs(2) - 1
    tq = q_ref.shape[1]
    r_m = lax.broadcasted_iota(jnp.int32, (tq, tq), 0)
    c_m = lax.broadcasted_iota(jnp.int32, (tq, tq), 1)
    valid_m = jnp.abs(c_m - r_m) <= half
    r_h = lax.broadcasted_iota(jnp.int32, (tq, half), 0)
    c_h = lax.broadcasted_iota(jnp.int32, (tq, half), 1)
    valid_l = (c_h >= r_h) & (n > 0)
    valid_r = (c_h <= r_h - (tq - half)) & (n < last)
    lane = lax.broadcasted_iota(jnp.int32, (tq, LANES), 1)
    lse_tile = jnp.zeros((tq, LANES), F32)
    for h in range(DIL_HEADS):
        hs = slice(h * HEAD_DIM, (h + 1) * HEAD_DIM)
        q = q_ref[0, :, hs]
        s_m = jnp.where(valid_m, _dot_nt(q, km_ref[0, :, hs]), MASK_VALUE)
        s_l = jnp.where(valid_l, _dot_nt(q, kl_ref[0, :, hs]), MASK_VALUE)
        s_r = jnp.where(valid_r, _dot_nt(q, kr_ref[0, :, hs]), MASK_VALUE)
        m = jnp.maximum(jnp.max(s_m, axis=-1, keepdims=True),
                        jnp.maximum(jnp.max(s_l, axis=-1, keepdims=True), jnp.max(s_r, axis=-1, keepdims=True)))
        p_m = jnp.exp(s_m - m)
        p_l = jnp.exp(s_l - m)
        p_r = jnp.exp(s_r - m)
        den = (jnp.sum(p_m, axis=-1, keepdims=True) + jnp.sum(p_l, axis=-1, keepdims=True)
               + jnp.sum(p_r, axis=-1, keepdims=True))
        o = (_dot(p_m.astype(BF16), vm_ref[0, :, hs]) + _dot(p_l.astype(BF16), vl_ref[0, :, hs])
             + _dot(p_r.astype(BF16), vr_ref[0, :, hs]))
        o_ref[0, :, hs] = (o / den).astype(o_ref.dtype)
        lse_tile = jnp.where(lane == h, m + jnp.log(den), lse_tile)
    lse_ref[0] = lse_tile


def _dilated(qv, kv, vv, window, dilation, tq):
    bsz, ln, _ = qv.shape
    half = window // (2 * dilation)
    nt = ln // tq
    per = tq // half
    nh = ln // half
    main = pl.BlockSpec((1, tq, DIL_W), lambda b, r, n: (b, n, r))
    left = pl.BlockSpec((1, half, DIL_W), lambda b, r, n: (b, jnp.maximum(n * per - 1, 0), r))
    right = pl.BlockSpec((1, half, DIL_W), lambda b, r, n: (b, jnp.minimum((n + 1) * per, nh - 1), r))
    return pl.pallas_call(
        functools.partial(_dilated_kernel, half=half),
        grid=(bsz, dilation, nt),
        in_specs=[main, main, left, right, main, left, right],
        out_specs=[main, pl.BlockSpec((1, tq, LANES), lambda b, r, n: (b, n, r))],
        out_shape=[jax.ShapeDtypeStruct((bsz, ln, dilation * DIL_W), BF16),
                   jax.ShapeDtypeStruct((bsz, ln, dilation * LANES), F32)],
        compiler_params=_cparams(("parallel", "parallel", "parallel")),
        name=f"dilated_d{dilation}",
    )(qv, kv, kv, kv, vv, vv, vv)


def _combine_kernel(oa_ref, o1_ref, o2_ref, o3_ref, l1_ref, l2_ref, l3_ref, out_ref, on_ref, ln_ref):
    out_ref[:, :GLA_V_W] = oa_ref[...].astype(BF16)
    rc = oa_ref.shape[0]
    for p, (d, o_ref, l_ref) in enumerate(zip(DILATIONS, (o1_ref, o2_ref, o3_ref), (l1_ref, l2_ref, l3_ref))):
        for r in range(d):
            rows = pl.ds(r, rc // d, stride=d) if d > 1 else slice(None)
            ln_ref.at[p][rows, :] = l_ref[0, :, r * LANES:(r + 1) * LANES]
            for h in range(DIL_HEADS):
                c0 = r * DIL_W + h * HEAD_DIM
                on_ref.at[p * DIL_HEADS + h][rows, :] = o_ref[0, :, c0:c0 + HEAD_DIM].astype(F32)
    l1 = ln_ref[0]
    l2 = ln_ref[1]
    l3 = ln_ref[2]
    mx = jnp.maximum(l1, jnp.maximum(l2, l3))
    e1 = jnp.exp(l1 - mx)
    e2 = jnp.exp(l2 - mx)
    e3 = jnp.exp(l3 - mx)
    inv = 1.0 / (e1 + e2 + e3)
    w1 = e1 * inv
    w2 = e2 * inv
    w3 = e3 * inv
    rows = oa_ref.shape[0]
    for h in range(DIL_HEADS):
        hs = slice(h * HEAD_DIM, (h + 1) * HEAD_DIM)
        bc = lambda w: jnp.broadcast_to(w[:, h:h + 1], (rows, HEAD_DIM))
        o = bc(w1) * on_ref[h] + bc(w2) * on_ref[DIL_HEADS + h] + bc(w3) * on_ref[2 * DIL_HEADS + h]
        out_ref[:, GLA_V_W + h * HEAD_DIM:GLA_V_W + (h + 1) * HEAD_DIM] = o.astype(BF16)


def _combine(oa, os, ls, bsz, seq, rc):
    t = oa.shape[0]
    nsteps = seq // rc
    rb = lambda w: pl.BlockSpec((rc, w), lambda b, n: (b * nsteps + n, 0))
    pat = lambda w: [pl.BlockSpec((1, rc // d, d * w), lambda b, n: (b, n, 0)) for d in DILATIONS]
    return pl.pallas_call(
        _combine_kernel,
        grid=(bsz, nsteps),
        in_specs=[rb(GLA_V_W)] + pat(DIL_W) + pat(LANES),
        out_specs=rb(MIX_W),
        out_shape=jax.ShapeDtypeStruct((t, MIX_W), BF16),
        scratch_shapes=[pltpu.VMEM((len(DILATIONS) * DIL_HEADS, rc, HEAD_DIM), F32),
                        pltpu.VMEM((len(DILATIONS), rc, LANES), F32)],
        compiler_params=_cparams(("parallel", "parallel")),
        name="combine",
    )(oa, *os, *ls)


def _outproj_kernel(mix_ref, w_ref, x_ref, g_ref, x2_ref, xnt_ref):
    x2 = x_ref[...] + _dot(mix_ref[...], w_ref[...])
    x2_ref[...] = x2
    xn = x2 * lax.rsqrt(jnp.mean(x2 * x2, axis=-1, keepdims=True) + EPS) * g_ref[...]
    xnt_ref[...] = xn.T.astype(BF16)


def _outproj(mixed, w_out, x, g2, tm):
    t, d = x.shape
    return pl.pallas_call(
        _outproj_kernel,
        grid=(t // tm,),
        in_specs=[
            pl.BlockSpec((tm, MIX_W), lambda i: (i, 0)),
            pl.BlockSpec((MIX_W, d), lambda i: (0, 0)),
            pl.BlockSpec((tm, d), lambda i: (i, 0)),
            pl.BlockSpec((1, d), lambda i: (0, 0)),
        ],
        out_specs=[pl.BlockSpec((tm, d), lambda i: (i, 0)), pl.BlockSpec((d, tm), lambda i: (0, i))],
        out_shape=[jax.ShapeDtypeStruct((t, d), F32), jax.ShapeDtypeStruct((d, t), BF16)],
        compiler_params=_cparams(("parallel",)),
        name="outproj",
    )(mixed, w_out, x, g2)


_NTOP = PEER_TOPK + 1
_CAND = [(a, b) for a in range(_NTOP) for b in range(_NTOP) if (a + 1) * (b + 1) <= _NTOP]
_NCAND_PAD = -(-len(_CAND) // 8) * 8


def _top_values(sc, count):
    vals = []
    for _ in range(count):
        m = jnp.max(sc, axis=0, keepdims=True)
        vals.append(m)
        sc = jnp.where(sc >= m, -jnp.inf, sc)
    return vals


def _peer_route_kernel(wq_ref, xnt_ref, sk_ref, thr_ref, w1_ref, v2_ref, q_ref, cand_ref):
    q_ref[...] = _dot(wq_ref[...], xnt_ref[...])
    half = PEER_QDIM // 2
    cand_ref[...] = jnp.full(cand_ref.shape, -jnp.inf, F32)

    def head(h, carry):
        sc_a = _dot3(sk_ref[h, 0], q_ref[pl.ds(pl.multiple_of(h * PEER_QDIM, PEER_QDIM), half), :])
        sc_b = _dot3(sk_ref[h, 1], q_ref[pl.ds(pl.multiple_of(h * PEER_QDIM + half, half), half), :])
        top_a = _top_values(sc_a, _NTOP)
        top_b = _top_values(sc_b, _NTOP)
        for i, (a, b) in enumerate(_CAND):
            cand_ref[i:i + 1, :] = top_a[a] + top_b[b]
        cand = cand_ref[...]
        top_c = _top_values(cand, _NTOP)
        tau = 0.5 * (top_c[PEER_TOPK - 1] + top_c[PEER_TOPK])
        best = top_a[0] + top_b[0]
        zsum = jnp.sum(jnp.where(cand > tau, jnp.exp(cand - best), 0.0), axis=0, keepdims=True)
        thr_ref[h] = jnp.exp((tau - sc_a) - top_b[0])
        w1_ref[h] = jnp.exp(sc_a - top_a[0]) / zsum
        v2_ref[h] = jnp.exp(sc_b - top_b[0])
        return carry

    lax.fori_loop(0, PEER_HEADS, head, 0)


def _peer_route(wq_t, xnt, sub_keys, tb):
    d, t = xnt.shape
    hq = wq_t.shape[0]
    key_blk = pl.BlockSpec((PEER_HEADS, PEER_NKEYS, tb), lambda i: (0, 0, i))
    key_shape = jax.ShapeDtypeStruct((PEER_HEADS, PEER_NKEYS, t), F32)
    return pl.pallas_call(
        _peer_route_kernel,
        grid=(t // tb,),
        in_specs=[
            pl.BlockSpec((hq, d), lambda i: (0, 0)),
            pl.BlockSpec((d, tb), lambda i: (0, i)),
            pl.BlockSpec(sub_keys.shape, lambda i: (0, 0, 0, 0)),
        ],
        out_specs=[key_blk, key_blk, key_blk],
        out_shape=[key_shape, key_shape, key_shape],
        scratch_shapes=[pltpu.VMEM((hq, tb), F32), pltpu.VMEM((_NCAND_PAD, tb), F32)],
        compiler_params=_cparams(("parallel",)),
        name="peer_route",
    )(wq_t, xnt, sub_keys)


SUBLANES = 8


def _peer_expert_kernel(xnt_ref, down_ref, upt_ref, thr_ref, w1_ref, v2_ref, x2_ref,
                        o_ref, acc_ref, a_ref, g_ref, h_ref):
    e = pl.program_id(1)
    eb, tb = a_ref.shape
    rows_i = eb // PEER_NKEYS

    @pl.when(e == 0)
    def _():
        acc_ref[...] = jnp.zeros_like(acc_ref)

    chunk_i = 2
    for ii in range(rows_i):
        if ii % chunk_i == 0:
            cs = slice(ii * PEER_NKEYS, (ii + chunk_i) * PEER_NKEYS)
            a_ref[cs, :] = _dot(down_ref[cs, :], xnt_ref[...])
        for lg in range(tb // LANES):
            ls = slice(lg * LANES, (lg + 1) * LANES)
            thr = [jnp.broadcast_to(thr_ref[h, ii:ii + 1, ls], (SUBLANES, LANES)) for h in range(PEER_HEADS)]
            w1 = [jnp.broadcast_to(w1_ref[h, ii:ii + 1, ls], (SUBLANES, LANES)) for h in range(PEER_HEADS)]
            for j0 in range(0, PEER_NKEYS, SUBLANES):
                js = slice(j0, j0 + SUBLANES)
                gate = jnp.zeros((SUBLANES, LANES), F32)
                for h in range(PEER_HEADS):
                    v2 = v2_ref[h, js, ls]
                    gate = gate + jnp.where(v2 >= thr[h], w1[h] * v2, 0.0)
                g_ref[ii * PEER_NKEYS + j0:ii * PEER_NKEYS + j0 + SUBLANES, ls] = gate

    rows = 2 * SUBLANES
    for r0 in range(0, eb, rows):
        h_ref[r0:r0 + rows, :] = (g_ref[r0:r0 + rows, :] * jax.nn.gelu(a_ref[r0:r0 + rows, :])).astype(BF16)
    acc_ref[...] += _dot(upt_ref[...], h_ref[...])

    @pl.when(e == pl.num_programs(1) - 1)
    def _():
        o_ref[...] = x2_ref[...] + acc_ref[...].T


def _peer_experts(xnt, down, up_t, thr, w1, v2, x2, tb, eb):
    d, t = xnt.shape
    ne = down.shape[0]
    rows_i = eb // PEER_NKEYS
    row_blk = pl.BlockSpec((PEER_HEADS, rows_i, tb), lambda i, e: (0, e, i))
    key_blk = pl.BlockSpec((PEER_HEADS, PEER_NKEYS, tb), lambda i, e: (0, 0, i))
    return pl.pallas_call(
        _peer_expert_kernel,
        grid=(t // tb, ne // eb),
        in_specs=[
            pl.BlockSpec((d, tb), lambda i, e: (0, i)),
            pl.BlockSpec((eb, d), lambda i, e: (e, 0)),
            pl.BlockSpec((d, eb), lambda i, e: (0, e)),
            row_blk, row_blk, key_blk,
            pl.BlockSpec((tb, d), lambda i, e: (i, 0), pipeline_mode=pl.Buffered(1)),
        ],
        out_specs=pl.BlockSpec((tb, d), lambda i, e: (i, 0)),
        out_shape=jax.ShapeDtypeStruct((t, d), F32),
        scratch_shapes=[pltpu.VMEM((d, tb), F32), pltpu.VMEM((eb, tb), F32), pltpu.VMEM((eb, tb), F32),
                        pltpu.VMEM((eb, tb), BF16)],
        compiler_params=_cparams(("parallel", "arbitrary")),
        name="peer_experts",
    )(xnt, down, up_t, thr, w1, v2, x2)


def _reorder_in_proj(w_in):
    qa, ka, va, gdf, gdb, ra, qd, kd, vd = jnp.split(w_in, [int(c) for c in np.cumsum(IN_SPLITS)[:-1]], axis=1)
    pad = jnp.zeros((w_in.shape[0], LANES - 2 * GLA_GATE_RANK), w_in.dtype)
    return jnp.concatenate([qd, kd, vd, qa, ka, va, ra, gdf, gdb, pad], axis=1).astype(BF16)


def _rope_tables(seq):
    half = ROPE_DIM // 2
    inv_freq = ROPE_THETA ** (-jnp.arange(half, dtype=F32) / half)
    ang = jnp.arange(seq, dtype=jnp.int32).astype(F32)[:, None] * inv_freq[None, :]
    cos = jnp.cos(ang)
    sin = jnp.sin(ang)
    zeros = jnp.zeros((seq, half), F32)
    rest = HEAD_DIM - ROPE_DIM
    c = jnp.concatenate([cos, cos, jnp.ones((seq, rest), F32)], axis=1)
    s1 = jnp.concatenate([-sin, zeros, jnp.zeros((seq, rest), F32)], axis=1)
    s2 = jnp.concatenate([zeros, sin, jnp.zeros((seq, rest), F32)], axis=1)
    return c, s1, s2


def _layer(x, norm1_g, w_in, up_f, bias_f, up_b, bias_b, out_g, q_norm_g, k_norm_g, w_out, norm2_g,
           w_query, sub_keys, down, up):
    bsz, seq, d = x.shape
    t = bsz * seq
    xf = x.reshape(t, d)

    h = _rms_matmul(xf, norm1_g.reshape(1, d), _reorder_in_proj(w_in), tm=1024, tn=IN_W_PAD // 7)

    zpad = jnp.zeros((LANES, GLA_QK_W), F32)
    upf_pad = zpad.at[:GLA_GATE_RANK].set(up_f)
    upb_pad = zpad.at[GLA_GATE_RANK:2 * GLA_GATE_RANK].set(up_b)
    rc = min(256, seq)
    o_fwd = _gla(h, upf_pad, bias_f.reshape(1, -1), None, None, bsz, seq, rc, reverse=False)
    oa = _gla(h, upb_pad, bias_b.reshape(1, -1), o_fwd, out_g.reshape(1, -1), bsz, seq, rc, reverse=True)

    cos, sn1, sn2 = _rope_tables(seq)
    qs, ks, vs = _qk_prep(h, q_norm_g.reshape(1, -1), k_norm_g.reshape(1, -1), cos, sn1, sn2, bsz, seq, rc=256)
    outs, lses = [], []
    for p, (window, dilation) in enumerate(DIL_PATTERNS):
        o_p, lse_p = _dilated(qs[p], ks[p], vs[p], window, dilation, tq=min(256, seq // dilation))
        outs.append(o_p)
        lses.append(lse_p)
    mixed = _combine(oa, outs, lses, bsz, seq, rc=512)

    x2, xnt = _outproj(mixed, w_out.astype(BF16), xf, norm2_g.reshape(1, d), tm=512)

    thr, w1, v2 = _peer_route(w_query.T.astype(BF16), xnt, sub_keys, tb=512)
    out = _peer_experts(xnt, down.astype(BF16), up.T.astype(BF16), thr, w1, v2, x2, tb=512, eb=1024)
    return out.reshape(bsz, seq, d)


def kernel(x, norm1_g, w_in, gla_up_f, gla_bias_f, gla_up_b, gla_bias_b, gla_out_g, q_norm_g, k_norm_g, w_out,
           norm2_g, peer_w_query, peer_sub_keys, peer_down, peer_up):
    for l in range(norm1_g.shape[0]):
        x = _layer(x, norm1_g[l], w_in[l], gla_up_f[l], gla_bias_f[l], gla_up_b[l], gla_bias_b[l], gla_out_g[l],
                   q_norm_g[l], k_norm_g[l], w_out[l], norm2_g[l], peer_w_query[l], peer_sub_keys[l],
                   peer_down[l], peer_up[l])
    return x
```

```python
import functools

import jax
import jax.numpy as jnp
import numpy as np
from jax import lax
from jax.experimental import pallas as pl
from jax.experimental.pallas import tpu as pltpu

F32 = jnp.float32
BF16 = jnp.bfloat16

EPS = 1e-6
HEAD_DIM = 128

GLA_HEADS = 4
GLA_DK = 64
GLA_DV = 128
GLA_GATE_RANK = 16
GLA_TAU = 16.0
GLA_CHUNK = 64
GLA_QK_W = GLA_HEADS * GLA_DK
GLA_V_W = GLA_HEADS * GLA_DV

DIL_HEADS = 12
DIL_PATTERNS = ((128, 1), (512, 4), (2048, 16))
DIL_W = DIL_HEADS * HEAD_DIM
ROPE_THETA = 500000.0
ROPE_DIM = HEAD_DIM // 4
MASK_VALUE = -1e30

PEER_HEADS = 8
PEER_NKEYS = 128
PEER_QDIM = 256
PEER_TOPK = 16

MIX_W = GLA_V_W + DIL_W
IN_SPLITS = (GLA_QK_W, GLA_QK_W, GLA_V_W, GLA_GATE_RANK, GLA_GATE_RANK, GLA_V_W, DIL_W, DIL_W, DIL_W)

LANES = 128
COL_QD = 0
COL_KD = COL_QD + DIL_W
COL_VD = COL_KD + DIL_W
COL_QA = COL_VD + DIL_W
COL_KA = COL_QA + GLA_QK_W
COL_VA = COL_KA + GLA_QK_W
COL_RA = COL_VA + GLA_V_W
COL_GD = COL_RA + GLA_V_W
MXU_COLS = 256
IN_TN = 5 * MXU_COLS
IN_W_PAD = -(-(COL_GD + LANES) // IN_TN) * IN_TN

VMEM_LIMIT = 56 * 1024 * 1024


def _cparams(sem):
    return pltpu.CompilerParams(dimension_semantics=sem, vmem_limit_bytes=VMEM_LIMIT)


def _split_bf16(a):
    hi = a.astype(BF16)
    lo = (a - hi.astype(F32)).astype(BF16)
    return hi, lo


def _dot(a, b):
    return jnp.dot(a, b, preferred_element_type=F32)


def _dot_nt(a, b):
    return lax.dot_general(a, b, (((1,), (1,)), ((), ())), preferred_element_type=F32)


def _dot_tn(a, b):
    return lax.dot_general(a, b, (((0,), (0,)), ((), ())), preferred_element_type=F32)


def _dot3(a, b):
    ah, al = _split_bf16(a)
    bh, bl = _split_bf16(b)
    return _dot(ah, bh) + (_dot(ah, bl) + _dot(al, bh))


def _rms_matmul_kernel(x_ref, g_ref, w_ref, o_ref, xn_ref):
    @pl.when(pl.program_id(1) == 0)
    def _():
        x = x_ref[...]
        ms = jnp.mean(x * x, axis=-1, keepdims=True)
        xn_ref[...] = (x * lax.rsqrt(ms + EPS) * g_ref[...]).astype(BF16)

    o_ref[...] = _dot(xn_ref[...], w_ref[...]).astype(o_ref.dtype)


def _rms_matmul(x, g, w, tm, tn):
    m, k = x.shape
    n = w.shape[1]
    return pl.pallas_call(
        _rms_matmul_kernel,
        grid=(m // tm, n // tn),
        in_specs=[
            pl.BlockSpec((tm, k), lambda i, j: (i, 0)),
            pl.BlockSpec((1, k), lambda i, j: (0, 0)),
            pl.BlockSpec((k, tn), lambda i, j: (0, j)),
        ],
        out_specs=pl.BlockSpec((tm, tn), lambda i, j: (i, j)),
        out_shape=jax.ShapeDtypeStruct((m, n), BF16),
        scratch_shapes=[pltpu.VMEM((tm, k), BF16)],
        compiler_params=_cparams(("parallel", "arbitrary")),
        name="rms_inproj",
    )(x, g, w)


def _log_sigmoid(z):
    return -(jnp.maximum(-z, 0.0) + jnp.log1p(jnp.exp(-jnp.abs(z))))


def _gla_kernel(*refs, reverse, finalize):
    if finalize:
        q_ref, k_ref, v_ref, gd_ref, up_ref, bias_ref, ofwd_ref, r_ref, og_ref, o_ref, st_ref = refs
    else:
        q_ref, k_ref, v_ref, gd_ref, up_ref, bias_ref, o_ref, st_ref = refs
    rc = q_ref.shape[0]
    nchunk = rc // GLA_CHUNK

    @pl.when(pl.program_id(1) == 0)
    def _():
        st_ref[...] = jnp.zeros_like(st_ref)

    up_hi, up_lo = _split_bf16(up_ref[...])
    z = _dot(gd_ref[...], up_hi) + _dot(gd_ref[...], up_lo) + bias_ref[...]
    la = _log_sigmoid(z) * (1.0 / GLA_TAU)
    row = lax.broadcasted_iota(jnp.int32, (rc, rc), 0)
    col = lax.broadcasted_iota(jnp.int32, (rc, rc), 1)
    same = (row // GLA_CHUNK) == (col // GLA_CHUNK)
    tri = same & ((col >= row) if reverse else (col <= row))
    la_hi, la_lo = _split_bf16(la)
    tri_b = tri.astype(BF16)
    same_b = same.astype(BF16)
    b = _dot(tri_b, la_hi) + _dot(tri_b, la_lo)
    tot = _dot(same_b, la_hi) + _dot(same_b, la_lo)
    q = q_ref[...].astype(F32)
    k = k_ref[...].astype(F32)
    q_in = (q * jnp.exp(b) * (GLA_DK ** -0.5)).astype(BF16)
    k_in = (k * jnp.exp(-b)).astype(BF16)
    k_st = (k * jnp.exp(tot - b)).astype(BF16)
    decay = jnp.exp(tot)
    v = v_ref[...].astype(BF16)

    r64 = lax.broadcasted_iota(jnp.int32, (GLA_CHUNK, GLA_CHUNK), 0)
    c64 = lax.broadcasted_iota(jnp.int32, (GLA_CHUNK, GLA_CHUNK), 1)
    mask = (c64 > r64) if reverse else (c64 <= r64)

    chunks = range(nchunk - 1, -1, -1) if reverse else range(nchunk)
    for c in chunks:
        rs = slice(c * GLA_CHUNK, (c + 1) * GLA_CHUNK)
        for h in range(GLA_HEADS):
            ks = slice(h * GLA_DK, (h + 1) * GLA_DK)
            vs = slice(h * GLA_DV, (h + 1) * GLA_DV)
            qc = q_in[rs, ks]
            vc = v[rs, vs]
            sc = _dot_nt(qc, k_in[rs, ks])
            p = jnp.where(mask, sc, 0.0).astype(BF16)
            st = st_ref[h]
            o = _dot(p, vc) + _dot_nt(qc, st.astype(BF16))
            inc = _dot_tn(vc, k_st[rs, ks])
            st_ref[h] = decay[c * GLA_CHUNK:c * GLA_CHUNK + 1, ks] * st + inc
            if finalize:
                o = o + ofwd_ref[rs, vs]
                y = o * lax.rsqrt(jnp.mean(o * o, axis=-1, keepdims=True) + EPS) * og_ref[:, vs]
                rr = r_ref[rs, vs].astype(F32)
                o = y * (rr * jax.nn.sigmoid(rr))
            o_ref[rs, vs] = o


def _gla(h, up_pad, bias, ofwd, out_g, bsz, seq, rc, reverse):
    nsteps = seq // rc
    finalize = ofwd is not None

    def rowblk(b, n):
        return b * nsteps + ((nsteps - 1 - n) if reverse else n)

    in_specs = [
        pl.BlockSpec((rc, GLA_QK_W), lambda b, n: (rowblk(b, n), COL_QA // GLA_QK_W)),
        pl.BlockSpec((rc, GLA_QK_W), lambda b, n: (rowblk(b, n), COL_KA // GLA_QK_W)),
        pl.BlockSpec((rc, GLA_V_W), lambda b, n: (rowblk(b, n), COL_VA // GLA_V_W)),
        pl.BlockSpec((rc, LANES), lambda b, n: (rowblk(b, n), COL_GD // LANES)),
        pl.BlockSpec((LANES, GLA_QK_W), lambda b, n: (0, 0)),
        pl.BlockSpec((1, GLA_QK_W), lambda b, n: (0, 0)),
    ]
    args = [h, h, h, h, up_pad, bias]
    if finalize:
        in_specs += [
            pl.BlockSpec((rc, GLA_V_W), lambda b, n: (rowblk(b, n), 0)),
            pl.BlockSpec((rc, GLA_V_W), lambda b, n: (rowblk(b, n), COL_RA // GLA_V_W)),
            pl.BlockSpec((1, GLA_V_W), lambda b, n: (0, 0)),
        ]
        args += [ofwd, h, out_g]
    return pl.pallas_call(
        functools.partial(_gla_kernel, reverse=reverse, finalize=finalize),
        grid=(bsz, nsteps),
        in_specs=in_specs,
        out_specs=pl.BlockSpec((rc, GLA_V_W), lambda b, n: (rowblk(b, n), 0)),
        out_shape=jax.ShapeDtypeStruct((bsz * seq, GLA_V_W), F32),
        scratch_shapes=[pltpu.VMEM((GLA_HEADS, GLA_DV, GLA_DK), F32)],
        compiler_params=_cparams(("parallel", "arbitrary")),
        name="gla_bwd" if reverse else "gla_fwd",
    )(*args)


DILATIONS = tuple(d for _, d in DIL_PATTERNS)


def _qk_prep_kernel(q_ref, k_ref, v_ref, qg_ref, kg_ref, c_ref, s_ref, *rest):
    nd = len(DILATIONS)
    outs = rest[:3 * nd]
    scr = rest[3 * nd:]
    rc = q_ref.shape[0]
    cos = c_ref[...]
    sin = s_ref[...]
    src = lax.broadcasted_iota(jnp.int32, (HEAD_DIM, HEAD_DIM), 0)
    dst = lax.broadcasted_iota(jnp.int32, (HEAD_DIM, HEAD_DIM), 1)
    hr = ROPE_DIM // 2
    swap = (jnp.where((dst < hr) & (src == dst + hr), -1.0, 0.0)
            + jnp.where((dst >= hr) & (dst < ROPE_DIM) & (src == dst - hr), 1.0, 0.0)).astype(BF16)
    ones = jnp.ones((HEAD_DIM, HEAD_DIM), BF16)

    def dot2(a, b):
        hi, lo = _split_bf16(a)
        return _dot(hi, b) + _dot(lo, b)

    def norm_rope(x, g):
        ms = dot2(x * x, ones) * (1.0 / HEAD_DIM)
        y = x * lax.rsqrt(ms + EPS) * g
        return y * cos + dot2(y, swap) * sin

    for h in range(DIL_HEADS):
        hs = slice(h * HEAD_DIM, (h + 1) * HEAD_DIM)
        scr[0][h] = norm_rope(q_ref[:, hs].astype(F32), qg_ref[...]) * (HEAD_DIM ** -0.5)
        scr[1][h] = norm_rope(k_ref[:, hs].astype(F32), kg_ref[...])
        scr[2][h] = v_ref[:, hs].astype(F32)
    for which in range(3):
        for di, d in enumerate(DILATIONS):
            o_ref = outs[which * nd + di]
            for h in range(DIL_HEADS):
                for r in range(d):
                    rows = pl.ds(r, rc // d, stride=d) if d > 1 else slice(None)
                    c0 = r * DIL_W + h * HEAD_DIM
                    o_ref[0, :, c0:c0 + HEAD_DIM] = scr[which].at[h][rows, :].astype(BF16)


def _qk_prep(h, qg, kg, cos, sin, bsz, seq, rc):
    nsteps = seq // rc
    hblk = lambda c: pl.BlockSpec((rc, DIL_W), lambda b, n: (b * nsteps + n, c // DIL_W))
    tab = pl.BlockSpec((rc, HEAD_DIM), lambda b, n: (n, 0))
    gsp = pl.BlockSpec((1, HEAD_DIM), lambda b, n: (0, 0))
    osp = [pl.BlockSpec((1, rc // d, d * DIL_W), lambda b, n: (b, n, 0)) for d in DILATIONS]
    oshape = [jax.ShapeDtypeStruct((bsz, seq // d, d * DIL_W), BF16) for d in DILATIONS]
    outs = pl.pallas_call(
        _qk_prep_kernel,
        grid=(bsz, nsteps),
        in_specs=[hblk(COL_QD), hblk(COL_KD), hblk(COL_VD), gsp, gsp, tab, tab],
        out_specs=osp * 3,
        out_shape=oshape * 3,
        scratch_shapes=[pltpu.VMEM((DIL_HEADS, rc, HEAD_DIM), F32)] * 3,
        compiler_params=_cparams(("parallel", "parallel")),
        name="qk_prep",
    )(h, h, h, qg, kg, cos, sin)
    nd = len(DILATIONS)
    return outs[:nd], outs[nd:2 * nd], outs[2 * nd:]


def _dilated_kernel(q_ref, km_ref, kl_ref, kr_ref, vm_ref, vl_ref, vr_ref, o_ref, lse_ref, *, half):
    n = pl.program_id(2)
    last = pl.num_programs(2) - 1
    tq = q_ref.shape[1]
    qb = 2 * half
    nsb = tq // qb
    r_i = lax.broadcasted_iota(jnp.int32, (qb, 2 * qb), 0)
    c_i = lax.broadcasted_iota(jnp.int32, (qb, 2 * qb), 1)
    band = (c_i >= r_i) & (c_i <= r_i + 2 * half)
    valid = []
    for sb in range(nsb):
        v = band
        if sb == 0:
            v = v & ((c_i >= half) | (n > 0))
        if sb == nsb - 1:
            v = v & ((c_i < 2 * qb - half) | (n < last))
        valid.append(v)
    lane = lax.broadcasted_iota(jnp.int32, (qb, LANES), 1)
    lse_tiles = [jnp.zeros((qb, LANES), F32) for _ in range(nsb)]
    for h in range(DIL_HEADS):
        hs = slice(h * HEAD_DIM, (h + 1) * HEAD_DIM)
        k_h = jnp.concatenate([kl_ref[0, :, hs], km_ref[0, :, hs], kr_ref[0, :, hs]], axis=0)
        v_h = jnp.concatenate([vl_ref[0, :, hs], vm_ref[0, :, hs], vr_ref[0, :, hs]], axis=0)
        for sb in range(nsb):
            q = q_ref[0, sb * qb:(sb + 1) * qb, hs]
            s = jnp.where(valid[sb], _dot_nt(q, k_h[sb * qb:sb * qb + 2 * qb]), MASK_VALUE)
            m = jnp.max(jnp.maximum(s[:, :qb], s[:, qb:]), axis=-1, keepdims=True)
            p = jnp.exp(s - m)
            den = jnp.sum(p[:, :qb] + p[:, qb:], axis=-1, keepdims=True)
            o = _dot(p.astype(BF16), v_h[sb * qb:sb * qb + 2 * qb])
            o_ref[0, sb * qb:(sb + 1) * qb, hs] = (o / den).astype(o_ref.dtype)
            lse_tiles[sb] = jnp.where(lane == h, m + jnp.log(den), lse_tiles[sb])
    for sb in range(nsb):
        lse_ref[0, sb * qb:(sb + 1) * qb, :] = lse_tiles[sb]


def _dilated(qv, kv, vv, window, dilation, tq):
    bsz, ln, _ = qv.shape
    half = window // (2 * dilation)
    nt = ln // tq
    per = tq // half
    nh = ln // half
    main = pl.BlockSpec((1, tq, DIL_W), lambda b, r, n: (b, n, r))
    left = pl.BlockSpec((1, half, DIL_W), lambda b, r, n: (b, jnp.maximum(n * per - 1, 0), r))
    right = pl.BlockSpec((1, half, DIL_W), lambda b, r, n: (b, jnp.minimum((n + 1) * per, nh - 1), r))
    return pl.pallas_call(
        functools.partial(_dilated_kernel, half=half),
        grid=(bsz, dilation, nt),
        in_specs=[main, main, left, right, main, left, right],
        out_specs=[main, pl.BlockSpec((1, tq, LANES), lambda b, r, n: (b, n, r))],
        out_shape=[jax.ShapeDtypeStruct((bsz, ln, dilation * DIL_W), BF16),
                   jax.ShapeDtypeStruct((bsz, ln, dilation * LANES), F32)],
        compiler_params=_cparams(("parallel", "parallel", "parallel")),
        name=f"dilated_d{dilation}",
    )(qv, kv, kv, kv, vv, vv, vv)


def _combine_kernel(oa_ref, o1_ref, o2_ref, o3_ref, l1_ref, l2_ref, l3_ref, out_ref, on_ref, ln_ref):
    out_ref[:, :GLA_V_W] = oa_ref[...].astype(BF16)
    rc = oa_ref.shape[0]
    for p, (d, o_ref, l_ref) in enumerate(zip(DILATIONS, (o1_ref, o2_ref, o3_ref), (l1_ref, l2_ref, l3_ref))):
        for r in range(d):
            rows = pl.ds(r, rc // d, stride=d) if d > 1 else slice(None)
            ln_ref.at[p][rows, :] = l_ref[0, :, r * LANES:(r + 1) * LANES]
            for h in range(DIL_HEADS):
                c0 = r * DIL_W + h * HEAD_DIM
                on_ref.at[p * DIL_HEADS + h][rows, :] = o_ref[0, :, c0:c0 + HEAD_DIM].astype(F32)
    l1 = ln_ref[0]
    l2 = ln_ref[1]
    l3 = ln_ref[2]
    mx = jnp.maximum(l1, jnp.maximum(l2, l3))
    e1 = jnp.exp(l1 - mx)
    e2 = jnp.exp(l2 - mx)
    e3 = jnp.exp(l3 - mx)
    inv = 1.0 / (e1 + e2 + e3)
    w1 = e1 * inv
    w2 = e2 * inv
    w3 = e3 * inv
    rows = oa_ref.shape[0]
    for h in range(DIL_HEADS):
        hs = slice(h * HEAD_DIM, (h + 1) * HEAD_DIM)
        bc = lambda w: jnp.broadcast_to(w[:, h:h + 1], (rows, HEAD_DIM))
        o = bc(w1) * on_ref[h] + bc(w2) * on_ref[DIL_HEADS + h] + bc(w3) * on_ref[2 * DIL_HEADS + h]
        out_ref[:, GLA_V_W + h * HEAD_DIM:GLA_V_W + (h + 1) * HEAD_DIM] = o.astype(BF16)


def _combine(oa, os, ls, bsz, seq, rc):
    t = oa.shape[0]
    nsteps = seq // rc
    rb = lambda w: pl.BlockSpec((rc, w), lambda b, n: (b * nsteps + n, 0))
    pat = lambda w: [pl.BlockSpec((1, rc // d, d * w), lambda b, n: (b, n, 0)) for d in DILATIONS]
    return pl.pallas_call(
        _combine_kernel,
        grid=(bsz, nsteps),
        in_specs=[rb(GLA_V_W)] + pat(DIL_W) + pat(LANES),
        out_specs=rb(MIX_W),
        out_shape=jax.ShapeDtypeStruct((t, MIX_W), BF16),
        scratch_shapes=[pltpu.VMEM((len(DILATIONS) * DIL_HEADS, rc, HEAD_DIM), F32),
                        pltpu.VMEM((len(DILATIONS), rc, LANES), F32)],
        compiler_params=_cparams(("parallel", "parallel")),
        name="combine",
    )(oa, *os, *ls)


def _outproj_kernel(mix_ref, w_ref, x_ref, g_ref, x2_ref, xnt_ref):
    x2 = x_ref[...] + _dot(mix_ref[...], w_ref[...])
    x2_ref[...] = x2
    xn = x2 * lax.rsqrt(jnp.mean(x2 * x2, axis=-1, keepdims=True) + EPS) * g_ref[...]
    xnt_ref[...] = xn.T.astype(BF16)


def _outproj(mixed, w_out, x, g2, tm):
    t, d = x.shape
    return pl.pallas_call(
        _outproj_kernel,
        grid=(t // tm,),
        in_specs=[
            pl.BlockSpec((tm, MIX_W), lambda i: (i, 0)),
            pl.BlockSpec((MIX_W, d), lambda i: (0, 0)),
            pl.BlockSpec((tm, d), lambda i: (i, 0)),
            pl.BlockSpec((1, d), lambda i: (0, 0)),
        ],
        out_specs=[pl.BlockSpec((tm, d), lambda i: (i, 0)), pl.BlockSpec((d, tm), lambda i: (0, i))],
        out_shape=[jax.ShapeDtypeStruct((t, d), F32), jax.ShapeDtypeStruct((d, t), BF16)],
        compiler_params=_cparams(("parallel",)),
        name="outproj",
    )(mixed, w_out, x, g2)


_NTOP = PEER_TOPK + 1
_CAND = [(a, b) for a in range(_NTOP) for b in range(_NTOP) if (a + 1) * (b + 1) <= _NTOP]
_NCAND_PAD = -(-len(_CAND) // 8) * 8


def _top_values(sc, count):
    vals = []
    for _ in range(count):
        m = jnp.max(sc, axis=0, keepdims=True)
        vals.append(m)
        sc = jnp.where(sc >= m, -jnp.inf, sc)
    return vals


def _peer_route_kernel(wq_ref, xnt_ref, sk_ref, thr_ref, w1_ref, v2_ref, q_ref, cand_ref):
    q_ref[...] = _dot(wq_ref[...], xnt_ref[...])
    half = PEER_QDIM // 2
    cand_ref[...] = jnp.full(cand_ref.shape, -jnp.inf, F32)

    def head(h, carry):
        sc_a = _dot3(sk_ref[h, 0], q_ref[pl.ds(pl.multiple_of(h * PEER_QDIM, PEER_QDIM), half), :])
        sc_b = _dot3(sk_ref[h, 1], q_ref[pl.ds(pl.multiple_of(h * PEER_QDIM + half, half), half), :])
        top_a = _top_values(sc_a, _NTOP)
        top_b = _top_values(sc_b, _NTOP)
        for i, (a, b) in enumerate(_CAND):
            cand_ref[i:i + 1, :] = top_a[a] + top_b[b]
        cand = cand_ref[...]
        top_c = _top_values(cand, _NTOP)
        tau = 0.5 * (top_c[PEER_TOPK - 1] + top_c[PEER_TOPK])
        best = top_a[0] + top_b[0]
        zsum = jnp.sum(jnp.where(cand > tau, jnp.exp(cand - best), 0.0), axis=0, keepdims=True)
        thr_ref[h] = jnp.exp((tau - sc_a) - top_b[0])
        w1_ref[h] = jnp.exp(sc_a - top_a[0]) / zsum
        v2_ref[h] = jnp.exp(sc_b - top_b[0])
        return carry

    lax.fori_loop(0, PEER_HEADS, head, 0)


def _peer_route(wq_t, xnt, sub_keys, tb):
    d, t = xnt.shape
    hq = wq_t.shape[0]
    key_blk = pl.BlockSpec((PEER_HEADS, PEER_NKEYS, tb), lambda i: (0, 0, i))
    key_shape = jax.ShapeDtypeStruct((PEER_HEADS, PEER_NKEYS, t), F32)
    return pl.pallas_call(
        _peer_route_kernel,
        grid=(t // tb,),
        in_specs=[
            pl.BlockSpec((hq, d), lambda i: (0, 0)),
            pl.BlockSpec((d, tb), lambda i: (0, i)),
            pl.BlockSpec(sub_keys.shape, lambda i: (0, 0, 0, 0)),
        ],
        out_specs=[key_blk, key_blk, key_blk],
        out_shape=[key_shape, key_shape, key_shape],
        scratch_shapes=[pltpu.VMEM((hq, tb), F32), pltpu.VMEM((_NCAND_PAD, tb), F32)],
        compiler_params=_cparams(("parallel",)),
        name="peer_route",
    )(wq_t, xnt, sub_keys)


SUBLANES = 8


def _peer_expert_kernel(xnt_ref, down_ref, upt_ref, thr_ref, w1_ref, v2_ref, x2_ref,
                        o_ref, acc_ref, a_ref, g_ref, h_ref):
    e = pl.program_id(1)
    eb, tb = a_ref.shape
    rows_i = eb // PEER_NKEYS

    @pl.when(e == 0)
    def _():
        acc_ref[...] = jnp.zeros_like(acc_ref)

    chunk_i = 2
    for ii in range(rows_i):
        if ii % chunk_i == 0:
            cs = slice(ii * PEER_NKEYS, (ii + chunk_i) * PEER_NKEYS)
            a_ref[cs, :] = _dot(down_ref[cs, :], xnt_ref[...])
        for lg in range(tb // LANES):
            ls = slice(lg * LANES, (lg + 1) * LANES)
            thr = [jnp.broadcast_to(thr_ref[h, ii:ii + 1, ls], (SUBLANES, LANES)) for h in range(PEER_HEADS)]
            w1 = [jnp.broadcast_to(w1_ref[h, ii:ii + 1, ls], (SUBLANES, LANES)) for h in range(PEER_HEADS)]
            for j0 in range(0, PEER_NKEYS, SUBLANES):
                js = slice(j0, j0 + SUBLANES)
                gate = jnp.zeros((SUBLANES, LANES), F32)
                for h in range(PEER_HEADS):
                    v2 = v2_ref[h, js, ls]
                    gate = gate + jnp.where(v2 >= thr[h], w1[h] * v2, 0.0)
                g_ref[ii * PEER_NKEYS + j0:ii * PEER_NKEYS + j0 + SUBLANES, ls] = gate

    rows = 2 * SUBLANES
    for r0 in range(0, eb, rows):
        h_ref[r0:r0 + rows, :] = (g_ref[r0:r0 + rows, :] * jax.nn.gelu(a_ref[r0:r0 + rows, :])).astype(BF16)
    acc_ref[...] += _dot(upt_ref[...], h_ref[...])

    @pl.when(e == pl.num_programs(1) - 1)
    def _():
        o_ref[...] = x2_ref[...] + acc_ref[...].T


def _peer_experts(xnt, down, up_t, thr, w1, v2, x2, tb, eb):
    d, t = xnt.shape
    ne = down.shape[0]
    rows_i = eb // PEER_NKEYS
    row_blk = pl.BlockSpec((PEER_HEADS, rows_i, tb), lambda i, e: (0, e, i))
    key_blk = pl.BlockSpec((PEER_HEADS, PEER_NKEYS, tb), lambda i, e: (0, 0, i))
    return pl.pallas_call(
        _peer_expert_kernel,
        grid=(t // tb, ne // eb),
        in_specs=[
            pl.BlockSpec((d, tb), lambda i, e: (0, i)),
            pl.BlockSpec((eb, d), lambda i, e: (e, 0)),
            pl.BlockSpec((d, eb), lambda i, e: (0, e)),
            row_blk, row_blk, key_blk,
            pl.BlockSpec((tb, d), lambda i, e: (i, 0), pipeline_mode=pl.Buffered(1)),
        ],
        out_specs=pl.BlockSpec((tb, d), lambda i, e: (i, 0)),
        out_shape=jax.ShapeDtypeStruct((t, d), F32),
        scratch_shapes=[pltpu.VMEM((d, tb), F32), pltpu.VMEM((eb, tb), F32), pltpu.VMEM((eb, tb), F32),
                        pltpu.VMEM((eb, tb), BF16)],
        compiler_params=_cparams(("parallel", "arbitrary")),
        name="peer_experts",
    )(xnt, down, up_t, thr, w1, v2, x2)


def _reorder_in_proj(w_in):
    qa, ka, va, gdf, gdb, ra, qd, kd, vd = jnp.split(w_in, [int(c) for c in np.cumsum(IN_SPLITS)[:-1]], axis=1)
    pad = jnp.zeros((w_in.shape[0], IN_W_PAD - COL_GD - 2 * GLA_GATE_RANK), w_in.dtype)
    return jnp.concatenate([qd, kd, vd, qa, ka, va, ra, gdf, gdb, pad], axis=1).astype(BF16)


def _rope_tables(seq):
    half = ROPE_DIM // 2
    inv_freq = ROPE_THETA ** (-jnp.arange(half, dtype=F32) / half)
    ang = jnp.arange(seq, dtype=jnp.int32).astype(F32)[:, None] * inv_freq[None, :]
    cos = jnp.cos(ang)
    sin = jnp.sin(ang)
    rest = HEAD_DIM - ROPE_DIM
    c = jnp.concatenate([cos, cos, jnp.ones((seq, rest), F32)], axis=1)
    s = jnp.concatenate([sin, sin, jnp.zeros((seq, rest), F32)], axis=1)
    return c, s


def _layer(x, norm1_g, w_in, up_f, bias_f, up_b, bias_b, out_g, q_norm_g, k_norm_g, w_out, norm2_g,
           w_query, sub_keys, down, up):
    bsz, seq, d = x.shape
    t = bsz * seq
    xf = x.reshape(t, d)

    h = _rms_matmul(xf, norm1_g.reshape(1, d), _reorder_in_proj(w_in), tm=1024, tn=IN_TN)

    zpad = jnp.zeros((LANES, GLA_QK_W), F32)
    upf_pad = zpad.at[:GLA_GATE_RANK].set(up_f)
    upb_pad = zpad.at[GLA_GATE_RANK:2 * GLA_GATE_RANK].set(up_b)
    rc = min(256, seq)
    o_fwd = _gla(h, upf_pad, bias_f.reshape(1, -1), None, None, bsz, seq, rc, reverse=False)
    oa = _gla(h, upb_pad, bias_b.reshape(1, -1), o_fwd, out_g.reshape(1, -1), bsz, seq, rc, reverse=True)

    cos, sin = _rope_tables(seq)
    qs, ks, vs = _qk_prep(h, q_norm_g.reshape(1, -1), k_norm_g.reshape(1, -1), cos, sin, bsz, seq, rc=256)
    outs, lses = [], []
    for p, (window, dilation) in enumerate(DIL_PATTERNS):
        o_p, lse_p = _dilated(qs[p], ks[p], vs[p], window, dilation, tq=min(512, seq // dilation))
        outs.append(o_p)
        lses.append(lse_p)
    mixed = _combine(oa, outs, lses, bsz, seq, rc=512)

    x2, xnt = _outproj(mixed, w_out.astype(BF16), xf, norm2_g.reshape(1, d), tm=512)

    thr, w1, v2 = _peer_route(w_query.T.astype(BF16), xnt, sub_keys, tb=512)
    out = _peer_experts(xnt, down.astype(BF16), up.T.astype(BF16), thr, w1, v2, x2, tb=512, eb=1024)
    return out.reshape(bsz, seq, d)


def kernel(x, norm1_g, w_in, gla_up_f, gla_bias_f, gla_up_b, gla_bias_b, gla_out_g, q_norm_g, k_norm_g, w_out,
           norm2_g, peer_w_query, peer_sub_keys, peer_down, peer_up):
    for l in range(norm1_g.shape[0]):
        x = _layer(x, norm1_g[l], w_in[l], gla_up_f[l], gla_bias_f[l], gla_up_b[l], gla_bias_b[l], gla_out_g[l],
                   q_norm_g[l], k_norm_g[l], w_out[l], norm2_g[l], peer_w_query[l], peer_sub_keys[l],
                   peer_down[l], peer_up[l])
    return x
```

```python
import functools

import jax
import jax.numpy as jnp
import numpy as np
from jax import lax
from jax.experimental import pallas as pl
from jax.experimental.pallas import tpu as pltpu

F32 = jnp.float32
BF16 = jnp.bfloat16

EPS = 1e-6
HEAD_DIM = 128

GLA_HEADS = 4
GLA_DK = 64
GLA_DV = 128
GLA_GATE_RANK = 16
GLA_TAU = 16.0
GLA_CHUNK = 64
GLA_QK_W = GLA_HEADS * GLA_DK
GLA_V_W = GLA_HEADS * GLA_DV

DIL_HEADS = 12
DIL_PATTERNS = ((128, 1), (512, 4), (2048, 16))
DIL_W = DIL_HEADS * HEAD_DIM
ROPE_THETA = 500000.0
ROPE_DIM = HEAD_DIM // 4
MASK_VALUE = -1e30

PEER_HEADS = 8
PEER_NKEYS = 128
PEER_QDIM = 256
PEER_TOPK = 16

MIX_W = GLA_V_W + DIL_W
IN_SPLITS = (GLA_QK_W, GLA_QK_W, GLA_V_W, GLA_GATE_RANK, GLA_GATE_RANK, GLA_V_W, DIL_W, DIL_W, DIL_W)

LANES = 128
COL_QD = 0
COL_KD = COL_QD + DIL_W
COL_VD = COL_KD + DIL_W
COL_QA = COL_VD + DIL_W
COL_KA = COL_QA + GLA_QK_W
COL_VA = COL_KA + GLA_QK_W
COL_RA = COL_VA + GLA_V_W
COL_GD = COL_RA + GLA_V_W
MXU_COLS = 256
IN_TN = 5 * MXU_COLS
IN_W_PAD = -(-(COL_GD + LANES) // IN_TN) * IN_TN

VMEM_LIMIT = 56 * 1024 * 1024


def _cparams(sem):
    return pltpu.CompilerParams(dimension_semantics=sem, vmem_limit_bytes=VMEM_LIMIT)


def _split_bf16(a):
    hi = a.astype(BF16)
    lo = (a - hi.astype(F32)).astype(BF16)
    return hi, lo


def _dot(a, b):
    return jnp.dot(a, b, preferred_element_type=F32)


def _dot_nt(a, b):
    return lax.dot_general(a, b, (((1,), (1,)), ((), ())), preferred_element_type=F32)


def _dot_tn(a, b):
    return lax.dot_general(a, b, (((0,), (0,)), ((), ())), preferred_element_type=F32)


def _dot3(a, b):
    ah, al = _split_bf16(a)
    bh, bl = _split_bf16(b)
    return _dot(ah, bh) + (_dot(ah, bl) + _dot(al, bh))


def _rms_matmul_kernel(x_ref, g_ref, w_ref, o_ref, xn_ref):
    @pl.when(pl.program_id(1) == 0)
    def _():
        x = x_ref[...]
        ms = jnp.mean(x * x, axis=-1, keepdims=True)
        xn_ref[...] = (x * lax.rsqrt(ms + EPS) * g_ref[...]).astype(BF16)

    o_ref[...] = _dot(xn_ref[...], w_ref[...]).astype(o_ref.dtype)


def _rms_matmul(x, g, w, tm, tn):
    m, k = x.shape
    n = w.shape[1]
    return pl.pallas_call(
        _rms_matmul_kernel,
        grid=(m // tm, n // tn),
        in_specs=[
            pl.BlockSpec((tm, k), lambda i, j: (i, 0)),
            pl.BlockSpec((1, k), lambda i, j: (0, 0)),
            pl.BlockSpec((k, tn), lambda i, j: (0, j)),
        ],
        out_specs=pl.BlockSpec((tm, tn), lambda i, j: (i, j)),
        out_shape=jax.ShapeDtypeStruct((m, n), BF16),
        scratch_shapes=[pltpu.VMEM((tm, k), BF16)],
        compiler_params=_cparams(("parallel", "arbitrary")),
        name="rms_inproj",
    )(x, g, w)


def _log_sigmoid(z):
    return -(jnp.maximum(-z, 0.0) + jnp.log1p(jnp.exp(-jnp.abs(z))))


def _gla_kernel(*refs, reverse, finalize):
    if finalize:
        q_ref, k_ref, v_ref, gd_ref, up_ref, bias_ref, ofwd_ref, r_ref, og_ref, o_ref, st_ref = refs
    else:
        q_ref, k_ref, v_ref, gd_ref, up_ref, bias_ref, o_ref, st_ref = refs
    rc = q_ref.shape[0]
    nchunk = rc // GLA_CHUNK

    @pl.when(pl.program_id(1) == 0)
    def _():
        st_ref[...] = jnp.zeros_like(st_ref)

    up_hi, up_lo = _split_bf16(up_ref[...])
    z = _dot(gd_ref[...], up_hi) + _dot(gd_ref[...], up_lo) + bias_ref[...]
    la = _log_sigmoid(z) * (1.0 / GLA_TAU)
    row = lax.broadcasted_iota(jnp.int32, (rc, rc), 0)
    col = lax.broadcasted_iota(jnp.int32, (rc, rc), 1)
    same = (row // GLA_CHUNK) == (col // GLA_CHUNK)
    tri = same & ((col >= row) if reverse else (col <= row))
    la_hi, la_lo = _split_bf16(la)
    tri_b = tri.astype(BF16)
    same_b = same.astype(BF16)
    b = _dot(tri_b, la_hi) + _dot(tri_b, la_lo)
    tot = _dot(same_b, la_hi) + _dot(same_b, la_lo)
    q = q_ref[...].astype(F32)
    k = k_ref[...].astype(F32)
    q_in = (q * jnp.exp(b) * (GLA_DK ** -0.5)).astype(BF16)
    k_in = (k * jnp.exp(-b)).astype(BF16)
    k_st = (k * jnp.exp(tot - b)).astype(BF16)
    decay = jnp.exp(tot)
    v = v_ref[...].astype(BF16)

    r64 = lax.broadcasted_iota(jnp.int32, (GLA_CHUNK, GLA_CHUNK), 0)
    c64 = lax.broadcasted_iota(jnp.int32, (GLA_CHUNK, GLA_CHUNK), 1)
    mask = (c64 > r64) if reverse else (c64 <= r64)

    chunks = range(nchunk - 1, -1, -1) if reverse else range(nchunk)
    for c in chunks:
        rs = slice(c * GLA_CHUNK, (c + 1) * GLA_CHUNK)
        for h in range(GLA_HEADS):
            ks = slice(h * GLA_DK, (h + 1) * GLA_DK)
            vs = slice(h * GLA_DV, (h + 1) * GLA_DV)
            qc = q_in[rs, ks]
            vc = v[rs, vs]
            sc = _dot_nt(qc, k_in[rs, ks])
            p = jnp.where(mask, sc, 0.0).astype(BF16)
            st = st_ref[h]
            o = _dot(p, vc) + _dot_nt(qc, st.astype(BF16))
            inc = _dot_tn(vc, k_st[rs, ks])
            st_ref[h] = decay[c * GLA_CHUNK:c * GLA_CHUNK + 1, ks] * st + inc
            if finalize:
                o = o + ofwd_ref[rs, vs]
                y = o * lax.rsqrt(jnp.mean(o * o, axis=-1, keepdims=True) + EPS) * og_ref[:, vs]
                rr = r_ref[rs, vs].astype(F32)
                o = y * (rr * jax.nn.sigmoid(rr))
            o_ref[rs, vs] = o


def _gla(h, up_pad, bias, ofwd, out_g, bsz, seq, rc, reverse):
    nsteps = seq // rc
    finalize = ofwd is not None

    def rowblk(b, n):
        return b * nsteps + ((nsteps - 1 - n) if reverse else n)

    in_specs = [
        pl.BlockSpec((rc, GLA_QK_W), lambda b, n: (rowblk(b, n), COL_QA // GLA_QK_W)),
        pl.BlockSpec((rc, GLA_QK_W), lambda b, n: (rowblk(b, n), COL_KA // GLA_QK_W)),
        pl.BlockSpec((rc, GLA_V_W), lambda b, n: (rowblk(b, n), COL_VA // GLA_V_W)),
        pl.BlockSpec((rc, LANES), lambda b, n: (rowblk(b, n), COL_GD // LANES)),
        pl.BlockSpec((LANES, GLA_QK_W), lambda b, n: (0, 0)),
        pl.BlockSpec((1, GLA_QK_W), lambda b, n: (0, 0)),
    ]
    args = [h, h, h, h, up_pad, bias]
    if finalize:
        in_specs += [
            pl.BlockSpec((rc, GLA_V_W), lambda b, n: (rowblk(b, n), 0)),
            pl.BlockSpec((rc, GLA_V_W), lambda b, n: (rowblk(b, n), COL_RA // GLA_V_W)),
            pl.BlockSpec((1, GLA_V_W), lambda b, n: (0, 0)),
        ]
        args += [ofwd, h, out_g]
    return pl.pallas_call(
        functools.partial(_gla_kernel, reverse=reverse, finalize=finalize),
        grid=(bsz, nsteps),
        in_specs=in_specs,
        out_specs=pl.BlockSpec((rc, GLA_V_W), lambda b, n: (rowblk(b, n), 0)),
        out_shape=jax.ShapeDtypeStruct((bsz * seq, GLA_V_W), F32),
        scratch_shapes=[pltpu.VMEM((GLA_HEADS, GLA_DV, GLA_DK), F32)],
        compiler_params=_cparams(("parallel", "arbitrary")),
        name="gla_bwd" if reverse else "gla_fwd",
    )(*args)


DILATIONS = tuple(d for _, d in DIL_PATTERNS)


def _qk_prep_kernel(q_ref, k_ref, v_ref, qg_ref, kg_ref, c_ref, s_ref, *rest):
    nd = len(DILATIONS)
    outs = rest[:3 * nd]
    scr = rest[3 * nd:]
    rc = q_ref.shape[0]
    cos = c_ref[...]
    sin = s_ref[...]
    src = lax.broadcasted_iota(jnp.int32, (HEAD_DIM, HEAD_DIM), 0)
    dst = lax.broadcasted_iota(jnp.int32, (HEAD_DIM, HEAD_DIM), 1)
    hr = ROPE_DIM // 2
    swap = (jnp.where((dst < hr) & (src == dst + hr), -1.0, 0.0)
            + jnp.where((dst >= hr) & (dst < ROPE_DIM) & (src == dst - hr), 1.0, 0.0)).astype(BF16)
    ones = jnp.ones((HEAD_DIM, HEAD_DIM), BF16)

    def dot2(a, b):
        hi, lo = _split_bf16(a)
        return _dot(hi, b) + _dot(lo, b)

    def norm_rope(x, g):
        ms = dot2(x * x, ones) * (1.0 / HEAD_DIM)
        y = x * lax.rsqrt(ms + EPS) * g
        return y * cos + dot2(y, swap) * sin

    for h in range(DIL_HEADS):
        hs = slice(h * HEAD_DIM, (h + 1) * HEAD_DIM)
        scr[0][h] = norm_rope(q_ref[:, hs].astype(F32), qg_ref[...]) * (HEAD_DIM ** -0.5)
        scr[1][h] = norm_rope(k_ref[:, hs].astype(F32), kg_ref[...])
        scr[2][h] = v_ref[:, hs].astype(F32)
    for which in range(3):
        for di, d in enumerate(DILATIONS):
            o_ref = outs[which * nd + di]
            for h in range(DIL_HEADS):
                for r in range(d):
                    rows = pl.ds(r, rc // d, stride=d) if d > 1 else slice(None)
                    c0 = r * DIL_W + h * HEAD_DIM
                    o_ref[0, :, c0:c0 + HEAD_DIM] = scr[which].at[h][rows, :].astype(BF16)


def _qk_prep(h, qg, kg, cos, sin, bsz, seq, rc):
    nsteps = seq // rc
    hblk = lambda c: pl.BlockSpec((rc, DIL_W), lambda b, n: (b * nsteps + n, c // DIL_W))
    tab = pl.BlockSpec((rc, HEAD_DIM), lambda b, n: (n, 0))
    gsp = pl.BlockSpec((1, HEAD_DIM), lambda b, n: (0, 0))
    osp = [pl.BlockSpec((1, rc // d, d * DIL_W), lambda b, n: (b, n, 0)) for d in DILATIONS]
    oshape = [jax.ShapeDtypeStruct((bsz, seq // d, d * DIL_W), BF16) for d in DILATIONS]
    outs = pl.pallas_call(
        _qk_prep_kernel,
        grid=(bsz, nsteps),
        in_specs=[hblk(COL_QD), hblk(COL_KD), hblk(COL_VD), gsp, gsp, tab, tab],
        out_specs=osp * 3,
        out_shape=oshape * 3,
        scratch_shapes=[pltpu.VMEM((DIL_HEADS, rc, HEAD_DIM), F32)] * 3,
        compiler_params=_cparams(("parallel", "parallel")),
        name="qk_prep",
    )(h, h, h, qg, kg, cos, sin)
    nd = len(DILATIONS)
    return outs[:nd], outs[nd:2 * nd], outs[2 * nd:]


def _dilated_kernel(q_ref, km_ref, kl_ref, kr_ref, vm_ref, vl_ref, vr_ref, o_ref, lse_ref, *, half):
    n = pl.program_id(2)
    last = pl.num_programs(2) - 1
    tq = q_ref.shape[1]
    qb = 2 * half
    nsb = tq // qb
    r_i = lax.broadcasted_iota(jnp.int32, (qb, 2 * qb), 0)
    c_i = lax.broadcasted_iota(jnp.int32, (qb, 2 * qb), 1)
    band = (c_i >= r_i) & (c_i <= r_i + 2 * half)
    valid = []
    for sb in range(nsb):
        v = band
        if sb == 0:
            v = v & ((c_i >= half) | (n > 0))
        if sb == nsb - 1:
            v = v & ((c_i < 2 * qb - half) | (n < last))
        valid.append(v)
    lane = lax.broadcasted_iota(jnp.int32, (qb, LANES), 1)
    lse_tiles = [jnp.zeros((qb, LANES), F32) for _ in range(nsb)]
    for h in range(DIL_HEADS):
        hs = slice(h * HEAD_DIM, (h + 1) * HEAD_DIM)
        k_h = jnp.concatenate([kl_ref[0, :, hs], km_ref[0, :, hs], kr_ref[0, :, hs]], axis=0)
        v_h = jnp.concatenate([vl_ref[0, :, hs], vm_ref[0, :, hs], vr_ref[0, :, hs]], axis=0)
        for sb in range(nsb):
            q = q_ref[0, sb * qb:(sb + 1) * qb, hs]
            s = jnp.where(valid[sb], _dot_nt(q, k_h[sb * qb:sb * qb + 2 * qb]), MASK_VALUE)
            m = jnp.max(jnp.maximum(s[:, :qb], s[:, qb:]), axis=-1, keepdims=True)
            p = jnp.exp(s - m)
            den = jnp.sum(p[:, :qb] + p[:, qb:], axis=-1, keepdims=True)
            o = _dot(p.astype(BF16), v_h[sb * qb:sb * qb + 2 * qb])
            o_ref[0, sb * qb:(sb + 1) * qb, hs] = (o / den).astype(o_ref.dtype)
            lse_tiles[sb] = jnp.where(lane == h, m + jnp.log(den), lse_tiles[sb])
    for sb in range(nsb):
        lse_ref[0, sb * qb:(sb + 1) * qb, :] = lse_tiles[sb]


def _dilated(qv, kv, vv, window, dilation, tq):
    bsz, ln, _ = qv.shape
    half = window // (2 * dilation)
    nt = ln // tq
    per = tq // half
    nh = ln // half
    main = pl.BlockSpec((1, tq, DIL_W), lambda b, r, n: (b, n, r))
    left = pl.BlockSpec((1, half, DIL_W), lambda b, r, n: (b, jnp.maximum(n * per - 1, 0), r))
    right = pl.BlockSpec((1, half, DIL_W), lambda b, r, n: (b, jnp.minimum((n + 1) * per, nh - 1), r))
    return pl.pallas_call(
        functools.partial(_dilated_kernel, half=half),
        grid=(bsz, dilation, nt),
        in_specs=[main, main, left, right, main, left, right],
        out_specs=[main, pl.BlockSpec((1, tq, LANES), lambda b, r, n: (b, n, r))],
        out_shape=[jax.ShapeDtypeStruct((bsz, ln, dilation * DIL_W), BF16),
                   jax.ShapeDtypeStruct((bsz, ln, dilation * LANES), F32)],
        compiler_params=_cparams(("parallel", "parallel", "parallel")),
        name=f"dilated_d{dilation}",
    )(qv, kv, kv, kv, vv, vv, vv)


def _combine_kernel(oa_ref, o1_ref, o2_ref, o3_ref, l1_ref, l2_ref, l3_ref, out_ref, on_ref, ln_ref):
    out_ref[:, :GLA_V_W] = oa_ref[...].astype(BF16)
    rc = oa_ref.shape[0]
    for p, (d, o_ref, l_ref) in enumerate(zip(DILATIONS, (o1_ref, o2_ref, o3_ref), (l1_ref, l2_ref, l3_ref))):
        for r in range(d):
            rows = pl.ds(r, rc // d, stride=d) if d > 1 else slice(None)
            ln_ref.at[p][rows, :] = l_ref[0, :, r * LANES:(r + 1) * LANES]
            for h in range(DIL_HEADS):
                c0 = r * DIL_W + h * HEAD_DIM
                on_ref.at[p * DIL_HEADS + h][rows, :] = o_ref[0, :, c0:c0 + HEAD_DIM].astype(F32)
    l1 = ln_ref[0]
    l2 = ln_ref[1]
    l3 = ln_ref[2]
    mx = jnp.maximum(l1, jnp.maximum(l2, l3))
    e1 = jnp.exp(l1 - mx)
    e2 = jnp.exp(l2 - mx)
    e3 = jnp.exp(l3 - mx)
    inv = 1.0 / (e1 + e2 + e3)
    w1 = e1 * inv
    w2 = e2 * inv
    w3 = e3 * inv
    rows = oa_ref.shape[0]
    for h in range(DIL_HEADS):
        hs = slice(h * HEAD_DIM, (h + 1) * HEAD_DIM)
        bc = lambda w: jnp.broadcast_to(w[:, h:h + 1], (rows, HEAD_DIM))
        o = bc(w1) * on_ref[h] + bc(w2) * on_ref[DIL_HEADS + h] + bc(w3) * on_ref[2 * DIL_HEADS + h]
        out_ref[:, GLA_V_W + h * HEAD_DIM:GLA_V_W + (h + 1) * HEAD_DIM] = o.astype(BF16)


def _combine(oa, os, ls, bsz, seq, rc):
    t = oa.shape[0]
    nsteps = seq // rc
    rb = lambda w: pl.BlockSpec((rc, w), lambda b, n: (b * nsteps + n, 0))
    pat = lambda w: [pl.BlockSpec((1, rc // d, d * w), lambda b, n: (b, n, 0)) for d in DILATIONS]
    return pl.pallas_call(
        _combine_kernel,
        grid=(bsz, nsteps),
        in_specs=[rb(GLA_V_W)] + pat(DIL_W) + pat(LANES),
        out_specs=rb(MIX_W),
        out_shape=jax.ShapeDtypeStruct((t, MIX_W), BF16),
        scratch_shapes=[pltpu.VMEM((len(DILATIONS) * DIL_HEADS, rc, HEAD_DIM), F32),
                        pltpu.VMEM((len(DILATIONS), rc, LANES), F32)],
        compiler_params=_cparams(("parallel", "parallel")),
        name="combine",
    )(oa, *os, *ls)


def _outproj_kernel(mix_ref, w_ref, x_ref, g_ref, x2_ref, xnt_ref):
    x2 = x_ref[...] + _dot(mix_ref[...], w_ref[...])
    x2_ref[...] = x2
    xn = x2 * lax.rsqrt(jnp.mean(x2 * x2, axis=-1, keepdims=True) + EPS) * g_ref[...]
    xnt_ref[...] = xn.T.astype(BF16)


def _outproj(mixed, w_out, x, g2, tm):
    t, d = x.shape
    return pl.pallas_call(
        _outproj_kernel,
        grid=(t // tm,),
        in_specs=[
            pl.BlockSpec((tm, MIX_W), lambda i: (i, 0)),
            pl.BlockSpec((MIX_W, d), lambda i: (0, 0)),
            pl.BlockSpec((tm, d), lambda i: (i, 0)),
            pl.BlockSpec((1, d), lambda i: (0, 0)),
        ],
        out_specs=[pl.BlockSpec((tm, d), lambda i: (i, 0)), pl.BlockSpec((d, tm), lambda i: (0, i))],
        out_shape=[jax.ShapeDtypeStruct((t, d), F32), jax.ShapeDtypeStruct((d, t), BF16)],
        compiler_params=_cparams(("parallel",)),
        name="outproj",
    )(mixed, w_out, x, g2)


_NTOP = PEER_TOPK + 1
_CAND = [(a, b) for a in range(_NTOP) for b in range(_NTOP) if (a + 1) * (b + 1) <= _NTOP]
_NCAND_PAD = 64


def _merge_sort_network(n):
    pairs = []
    p = 1
    while p < n:
        k = p
        while k >= 1:
            for j in range(k % p, n - k, 2 * k):
                for i in range(min(k, n - j - k)):
                    if (i + j) // (2 * p) == (i + j + k) // (2 * p):
                        pairs.append((i + j, i + j + k))
            k //= 2
        p *= 2
    return pairs


def _top_values(sc, count):
    rows = sc.shape[0]
    ntile = rows // SUBLANES
    tiles = [sc[k * SUBLANES:(k + 1) * SUBLANES, :] for k in range(ntile)]
    for lo, hi in _merge_sort_network(ntile):
        tiles[lo], tiles[hi] = jnp.maximum(tiles[lo], tiles[hi]), jnp.minimum(tiles[lo], tiles[hi])
    vals = []
    for it in range(count):
        m = jnp.max(tiles[0], axis=0, keepdims=True)
        vals.append(m)
        hit = tiles[0] >= m
        for k in range(min(ntile, count - it - 1)):
            below = tiles[k + 1] if k + 1 < ntile else jnp.full_like(tiles[k], -jnp.inf)
            tiles[k] = jnp.where(hit, below, tiles[k])
    return vals


def _peer_route_kernel(wq_ref, xnt_ref, sk_ref, thr_ref, w1_ref, v2_ref, q_ref, cand_ref):
    q_ref[...] = _dot(wq_ref[...], xnt_ref[...])
    half = PEER_QDIM // 2
    cand_ref[...] = jnp.full(cand_ref.shape, -jnp.inf, F32)

    def head(h, carry):
        sc_a = _dot3(sk_ref[h, 0], q_ref[pl.ds(pl.multiple_of(h * PEER_QDIM, PEER_QDIM), half), :])
        sc_b = _dot3(sk_ref[h, 1], q_ref[pl.ds(pl.multiple_of(h * PEER_QDIM + half, half), half), :])
        top_a = _top_values(sc_a, _NTOP)
        top_b = _top_values(sc_b, _NTOP)
        for i, (a, b) in enumerate(_CAND):
            cand_ref[i:i + 1, :] = top_a[a] + top_b[b]
        cand = cand_ref[...]
        top_c = _top_values(cand, _NTOP)
        tau = 0.5 * (top_c[PEER_TOPK - 1] + top_c[PEER_TOPK])
        best = top_a[0] + top_b[0]
        zsum = jnp.sum(jnp.where(cand > tau, jnp.exp(cand - best), 0.0), axis=0, keepdims=True)
        thr_ref[h] = jnp.exp((tau - sc_a) - top_b[0])
        w1_ref[h] = jnp.exp(sc_a - top_a[0]) * (1.0 / zsum)
        v2_ref[h] = jnp.exp(sc_b - top_b[0])
        return carry

    lax.fori_loop(0, PEER_HEADS, head, 0)


def _peer_route(wq_t, xnt, sub_keys, tb):
    d, t = xnt.shape
    hq = wq_t.shape[0]
    key_blk = pl.BlockSpec((PEER_HEADS, PEER_NKEYS, tb), lambda i: (0, 0, i))
    key_shape = jax.ShapeDtypeStruct((PEER_HEADS, PEER_NKEYS, t), F32)
    return pl.pallas_call(
        _peer_route_kernel,
        grid=(t // tb,),
        in_specs=[
            pl.BlockSpec((hq, d), lambda i: (0, 0)),
            pl.BlockSpec((d, tb), lambda i: (0, i)),
            pl.BlockSpec(sub_keys.shape, lambda i: (0, 0, 0, 0)),
        ],
        out_specs=[key_blk, key_blk, key_blk],
        out_shape=[key_shape, key_shape, key_shape],
        scratch_shapes=[pltpu.VMEM((hq, tb), F32), pltpu.VMEM((_NCAND_PAD, tb), F32)],
        compiler_params=_cparams(("parallel",)),
        name="peer_route",
    )(wq_t, xnt, sub_keys)


SUBLANES = 8


def _peer_expert_kernel(xnt_ref, down_ref, upt_ref, thr_ref, w1_ref, v2_ref, x2_ref,
                        o_ref, acc_ref, a_ref, g_ref, h_ref):
    e = pl.program_id(1)
    eb, tb = a_ref.shape
    rows_i = eb // PEER_NKEYS

    @pl.when(e == 0)
    def _():
        acc_ref[...] = jnp.zeros_like(acc_ref)

    chunk_i = 4
    for ii in range(rows_i):
        if ii % chunk_i == 0:
            cs = slice(ii * PEER_NKEYS, (ii + chunk_i) * PEER_NKEYS)
            a_ref[cs, :] = _dot(down_ref[cs, :], xnt_ref[...])
        for lg in range(tb // LANES):
            ls = slice(lg * LANES, (lg + 1) * LANES)
            thr = [jnp.broadcast_to(thr_ref[h, ii:ii + 1, ls], (SUBLANES, LANES)) for h in range(PEER_HEADS)]
            w1 = [jnp.broadcast_to(w1_ref[h, ii:ii + 1, ls], (SUBLANES, LANES)) for h in range(PEER_HEADS)]
            for j0 in range(0, PEER_NKEYS, SUBLANES):
                js = slice(j0, j0 + SUBLANES)
                gate = jnp.zeros((SUBLANES, LANES), F32)
                for h in range(PEER_HEADS):
                    v2 = v2_ref[h, js, ls]
                    gate = gate + jnp.where(v2 >= thr[h], w1[h] * v2, 0.0)
                g_ref[ii * PEER_NKEYS + j0:ii * PEER_NKEYS + j0 + SUBLANES, ls] = gate

    rows = 2 * SUBLANES
    for r0 in range(0, eb, rows):
        h_ref[r0:r0 + rows, :] = (g_ref[r0:r0 + rows, :] * jax.nn.gelu(a_ref[r0:r0 + rows, :])).astype(BF16)
    acc_ref[...] += _dot(upt_ref[...], h_ref[...])

    @pl.when(e == pl.num_programs(1) - 1)
    def _():
        o_ref[...] = x2_ref[...] + acc_ref[...].T


def _peer_experts(xnt, down, up_t, thr, w1, v2, x2, tb, eb):
    d, t = xnt.shape
    ne = down.shape[0]
    rows_i = eb // PEER_NKEYS
    row_blk = pl.BlockSpec((PEER_HEADS, rows_i, tb), lambda i, e: (0, e, i))
    key_blk = pl.BlockSpec((PEER_HEADS, PEER_NKEYS, tb), lambda i, e: (0, 0, i))
    return pl.pallas_call(
        _peer_expert_kernel,
        grid=(t // tb, ne // eb),
        in_specs=[
            pl.BlockSpec((d, tb), lambda i, e: (0, i)),
            pl.BlockSpec((eb, d), lambda i, e: (e, 0)),
            pl.BlockSpec((d, eb), lambda i, e: (0, e)),
            row_blk, row_blk, key_blk,
            pl.BlockSpec((tb, d), lambda i, e: (i, 0), pipeline_mode=pl.Buffered(1)),
        ],
        out_specs=pl.BlockSpec((tb, d), lambda i, e: (i, 0)),
        out_shape=jax.ShapeDtypeStruct((t, d), F32),
        scratch_shapes=[pltpu.VMEM((d, tb), F32), pltpu.VMEM((eb, tb), F32), pltpu.VMEM((eb, tb), F32),
                        pltpu.VMEM((eb, tb), BF16)],
        compiler_params=_cparams(("parallel", "arbitrary")),
        name="peer_experts",
    )(xnt, down, up_t, thr, w1, v2, x2)


def _reorder_in_proj(w_in):
    qa, ka, va, gdf, gdb, ra, qd, kd, vd = jnp.split(w_in, [int(c) for c in np.cumsum(IN_SPLITS)[:-1]], axis=1)
    pad = jnp.zeros((w_in.shape[0], IN_W_PAD - COL_GD - 2 * GLA_GATE_RANK), w_in.dtype)
    return jnp.concatenate([qd, kd, vd, qa, ka, va, ra, gdf, gdb, pad], axis=1).astype(BF16)


def _rope_tables(seq):
    half = ROPE_DIM // 2
    inv_freq = ROPE_THETA ** (-jnp.arange(half, dtype=F32) / half)
    ang = jnp.arange(seq, dtype=jnp.int32).astype(F32)[:, None] * inv_freq[None, :]
    cos = jnp.cos(ang)
    sin = jnp.sin(ang)
    rest = HEAD_DIM - ROPE_DIM
    c = jnp.concatenate([cos, cos, jnp.ones((seq, rest), F32)], axis=1)
    s = jnp.concatenate([sin, sin, jnp.zeros((seq, rest), F32)], axis=1)
    return c, s


def _layer(x, norm1_g, w_in, up_f, bias_f, up_b, bias_b, out_g, q_norm_g, k_norm_g, w_out, norm2_g,
           w_query, sub_keys, down, up):
    bsz, seq, d = x.shape
    t = bsz * seq
    xf = x.reshape(t, d)

    h = _rms_matmul(xf, norm1_g.reshape(1, d), _reorder_in_proj(w_in), tm=1024, tn=IN_TN)

    zpad = jnp.zeros((LANES, GLA_QK_W), F32)
    upf_pad = zpad.at[:GLA_GATE_RANK].set(up_f)
    upb_pad = zpad.at[GLA_GATE_RANK:2 * GLA_GATE_RANK].set(up_b)
    rc = min(256, seq)
    o_fwd = _gla(h, upf_pad, bias_f.reshape(1, -1), None, None, bsz, seq, rc, reverse=False)
    oa = _gla(h, upb_pad, bias_b.reshape(1, -1), o_fwd, out_g.reshape(1, -1), bsz, seq, rc, reverse=True)

    cos, sin = _rope_tables(seq)
    qs, ks, vs = _qk_prep(h, q_norm_g.reshape(1, -1), k_norm_g.reshape(1, -1), cos, sin, bsz, seq, rc=256)
    outs, lses = [], []
    for p, (window, dilation) in enumerate(DIL_PATTERNS):
        o_p, lse_p = _dilated(qs[p], ks[p], vs[p], window, dilation, tq=min(512, seq // dilation))
        outs.append(o_p)
        lses.append(lse_p)
    mixed = _combine(oa, outs, lses, bsz, seq, rc=512)

    x2, xnt = _outproj(mixed, w_out.astype(BF16), xf, norm2_g.reshape(1, d), tm=512)

    thr, w1, v2 = _peer_route(w_query.T.astype(BF16), xnt, sub_keys, tb=512)
    out = _peer_experts(xnt, down.astype(BF16), up.T.astype(BF16), thr, w1, v2, x2, tb=512, eb=1024)
    return out.reshape(bsz, seq, d)


def kernel(x, norm1_g, w_in, gla_up_f, gla_bias_f, gla_up_b, gla_bias_b, gla_out_g, q_norm_g, k_norm_g, w_out,
           norm2_g, peer_w_query, peer_sub_keys, peer_down, peer_up):
    for l in range(norm1_g.shape[0]):
        x = _layer(x, norm1_g[l], w_in[l], gla_up_f[l], gla_bias_f[l], gla_up_b[l], gla_bias_b[l], gla_out_g[l],
                   q_norm_g[l], k_norm_g[l], w_out[l], norm2_g[l], peer_w_query[l], peer_sub_keys[l],
                   peer_down[l], peer_up[l])
    return x
```

```python
import functools

import jax
import jax.numpy as jnp
import numpy as np
from jax import lax
from jax.experimental import pallas as pl
from jax.experimental.pallas import tpu as pltpu

F32 = jnp.float32
BF16 = jnp.bfloat16

EPS = 1e-6
HEAD_DIM = 128

GLA_HEADS = 4
GLA_DK = 64
GLA_DV = 128
GLA_GATE_RANK = 16
GLA_TAU = 16.0
GLA_CHUNK = 64
GLA_QK_W = GLA_HEADS * GLA_DK
GLA_V_W = GLA_HEADS * GLA_DV

DIL_HEADS = 12
DIL_PATTERNS = ((128, 1), (512, 4), (2048, 16))
DIL_W = DIL_HEADS * HEAD_DIM
ROPE_THETA = 500000.0
ROPE_DIM = HEAD_DIM // 4
MASK_VALUE = -1e30

PEER_HEADS = 8
PEER_NKEYS = 128
PEER_QDIM = 256
PEER_TOPK = 16

MIX_W = GLA_V_W + DIL_W
IN_SPLITS = (GLA_QK_W, GLA_QK_W, GLA_V_W, GLA_GATE_RANK, GLA_GATE_RANK, GLA_V_W, DIL_W, DIL_W, DIL_W)

LANES = 128
COL_QD = 0
COL_KD = COL_QD + DIL_W
COL_VD = COL_KD + DIL_W
COL_QA = COL_VD + DIL_W
COL_KA = COL_QA + GLA_QK_W
COL_VA = COL_KA + GLA_QK_W
COL_RA = COL_VA + GLA_V_W
COL_GD = COL_RA + GLA_V_W
MXU_COLS = 256
IN_TN = 5 * MXU_COLS
IN_W_PAD = -(-(COL_GD + LANES) // IN_TN) * IN_TN

VMEM_LIMIT = 56 * 1024 * 1024


def _cparams(sem):
    return pltpu.CompilerParams(dimension_semantics=sem, vmem_limit_bytes=VMEM_LIMIT)


def _split_bf16(a):
    hi = a.astype(BF16)
    lo = (a - hi.astype(F32)).astype(BF16)
    return hi, lo


def _dot(a, b):
    return jnp.dot(a, b, preferred_element_type=F32)


def _dot_nt(a, b):
    return lax.dot_general(a, b, (((1,), (1,)), ((), ())), preferred_element_type=F32)


def _dot_tn(a, b):
    return lax.dot_general(a, b, (((0,), (0,)), ((), ())), preferred_element_type=F32)


def _dot3(a, b):
    ah, al = _split_bf16(a)
    bh, bl = _split_bf16(b)
    return _dot(ah, bh) + (_dot(ah, bl) + _dot(al, bh))


def _rms_matmul_kernel(x_ref, g_ref, w_ref, o_ref, xn_ref):
    @pl.when(pl.program_id(1) == 0)
    def _():
        x = x_ref[...]
        ms = jnp.mean(x * x, axis=-1, keepdims=True)
        xn_ref[...] = (x * lax.rsqrt(ms + EPS) * g_ref[...]).astype(BF16)

    o_ref[...] = _dot(xn_ref[...], w_ref[...]).astype(o_ref.dtype)


def _rms_matmul(x, g, w, tm, tn):
    m, k = x.shape
    n = w.shape[1]
    return pl.pallas_call(
        _rms_matmul_kernel,
        grid=(m // tm, n // tn),
        in_specs=[
            pl.BlockSpec((tm, k), lambda i, j: (i, 0)),
            pl.BlockSpec((1, k), lambda i, j: (0, 0)),
            pl.BlockSpec((k, tn), lambda i, j: (0, j)),
        ],
        out_specs=pl.BlockSpec((tm, tn), lambda i, j: (i, j)),
        out_shape=jax.ShapeDtypeStruct((m, n), BF16),
        scratch_shapes=[pltpu.VMEM((tm, k), BF16)],
        compiler_params=_cparams(("parallel", "arbitrary")),
        name="rms_inproj",
    )(x, g, w)


def _log_sigmoid(z):
    return -(jnp.maximum(-z, 0.0) + jnp.log1p(jnp.exp(-jnp.abs(z))))


def _gla_kernel(qf, kf, vf, gf, upf, bf, qb, kb, vb, gb, upb, bb, of_ref, ob_ref, stf_ref, stb_ref):
    @pl.when(pl.program_id(1) == 0)
    def _():
        stf_ref[...] = jnp.zeros_like(stf_ref)
        stb_ref[...] = jnp.zeros_like(stb_ref)

    both = (0, 1)
    q_refs, k_refs, v_refs, gd_refs = (qf, qb), (kf, kb), (vf, vb), (gf, gb)
    up_refs, bias_refs, o_refs, st_refs = (upf, upb), (bf, bb), (of_ref, ob_ref), (stf_ref, stb_ref)
    rc = qf.shape[0]
    nchunk = rc // GLA_CHUNK
    ups = [_split_bf16(up_refs[d][...]) for d in both]
    z = [_dot(gd_refs[d][...], ups[d][0]) + _dot(gd_refs[d][...], ups[d][1]) + bias_refs[d][...] for d in both]
    la = [_log_sigmoid(z[d]) * (1.0 / GLA_TAU) for d in both]
    row = lax.broadcasted_iota(jnp.int32, (rc, rc), 0)
    col = lax.broadcasted_iota(jnp.int32, (rc, rc), 1)
    same = (row // GLA_CHUNK) == (col // GLA_CHUNK)
    tri = [same & (col <= row), same & (col >= row)]
    las = [_split_bf16(la[d]) for d in both]
    sums = [jnp.concatenate([tri[d].astype(BF16), same.astype(BF16)], axis=0) for d in both]
    bt = [_dot(sums[d], las[d][0]) + _dot(sums[d], las[d][1]) for d in both]
    b = [bt[d][:rc] for d in both]
    tot = [bt[d][rc:] for d in both]
    q = [q_refs[d][...].astype(F32) for d in both]
    k = [k_refs[d][...].astype(F32) for d in both]
    q_in = [(q[d] * jnp.exp(b[d]) * (GLA_DK ** -0.5)).astype(BF16) for d in both]
    k_in = [(k[d] * jnp.exp(-b[d])).astype(BF16) for d in both]
    k_st = [(k[d] * jnp.exp(tot[d] - b[d])).astype(BF16) for d in both]
    decay = [jnp.exp(tot[d]) for d in both]
    v = [v_refs[d][...].astype(BF16) for d in both]

    def head_blocks(shape, row_w, col_w):
        r = lax.broadcasted_iota(jnp.int32, shape, 0) // row_w
        c = lax.broadcasted_iota(jnp.int32, shape, 1) // col_w
        return r == c

    hc = GLA_HEADS * GLA_CHUNK
    kk_blocks = head_blocks((hc, GLA_QK_W), GLA_CHUNK, GLA_DK)
    kv_blocks = head_blocks((hc, GLA_V_W), GLA_CHUNK, GLA_DV)
    st_blocks = head_blocks((GLA_V_W, GLA_QK_W), GLA_DV, GLA_DK)
    r_i = lax.broadcasted_iota(jnp.int32, (GLA_CHUNK, hc), 0)
    j_i = lax.broadcasted_iota(jnp.int32, (GLA_CHUNK, hc), 1) % GLA_CHUNK
    mask = [j_i <= r_i, j_i > r_i]
    zero_b = jnp.zeros((), BF16)

    for step in range(nchunk):
        cs = (step, nchunk - 1 - step)
        rs = [slice(cs[d] * GLA_CHUNK, (cs[d] + 1) * GLA_CHUNK) for d in both]
        qc = [q_in[d][rs[d], :] for d in both]
        vc = [v[d][rs[d], :] for d in both]
        k_bd = [jnp.where(kk_blocks, jnp.concatenate([k_in[d][rs[d], :]] * GLA_HEADS, axis=0), zero_b)
                for d in both]
        v_bd = [jnp.where(kv_blocks, jnp.concatenate([vc[d]] * GLA_HEADS, axis=0), zero_b) for d in both]
        sc = [_dot_nt(qc[d], k_bd[d]) for d in both]
        p = [jnp.where(mask[d], sc[d], 0.0).astype(BF16) for d in both]
        st = [st_refs[d][...] for d in both]
        o_all = [_dot(p[d], v_bd[d]) + _dot_nt(qc[d], st[d].astype(BF16)) for d in both]
        inc = [jnp.where(st_blocks, _dot_tn(vc[d], k_st[d][rs[d], :]), 0.0) for d in both]
        for d in both:
            row0 = cs[d] * GLA_CHUNK
            st_refs[d][...] = decay[d][row0:row0 + 1, :] * st[d] + inc[d]
            o_refs[d][rs[d], :] = o_all[d]


def _gla(h, upf_pad, bias_f, upb_pad, bias_b, bsz, seq, rc):
    nsteps = seq // rc

    def specs(rowblk):
        return [
            pl.BlockSpec((rc, GLA_QK_W), lambda b, n: (rowblk(b, n), COL_QA // GLA_QK_W)),
            pl.BlockSpec((rc, GLA_QK_W), lambda b, n: (rowblk(b, n), COL_KA // GLA_QK_W)),
            pl.BlockSpec((rc, GLA_V_W), lambda b, n: (rowblk(b, n), COL_VA // GLA_V_W)),
            pl.BlockSpec((rc, LANES), lambda b, n: (rowblk(b, n), COL_GD // LANES)),
            pl.BlockSpec((LANES, GLA_QK_W), lambda b, n: (0, 0)),
            pl.BlockSpec((1, GLA_QK_W), lambda b, n: (0, 0)),
        ]

    fwd_blk = lambda b, n: b * nsteps + n
    bwd_blk = lambda b, n: b * nsteps + (nsteps - 1 - n)
    oshape = jax.ShapeDtypeStruct((bsz * seq, GLA_V_W), F32)
    return pl.pallas_call(
        _gla_kernel,
        grid=(bsz, nsteps),
        in_specs=specs(fwd_blk) + specs(bwd_blk),
        out_specs=[pl.BlockSpec((rc, GLA_V_W), lambda b, n: (fwd_blk(b, n), 0)),
                   pl.BlockSpec((rc, GLA_V_W), lambda b, n: (bwd_blk(b, n), 0))],
        out_shape=[oshape, oshape],
        scratch_shapes=[pltpu.VMEM((GLA_V_W, GLA_QK_W), F32)] * 2,
        compiler_params=_cparams(("parallel", "arbitrary")),
        name="gla",
    )(h, h, h, h, upf_pad, bias_f, h, h, h, h, upb_pad, bias_b)


DILATIONS = tuple(d for _, d in DIL_PATTERNS)


def _qk_prep_kernel(q_ref, k_ref, v_ref, qg_ref, kg_ref, c_ref, s_ref, *rest):
    nd = len(DILATIONS)
    outs = rest[:3 * nd]
    scr = rest[3 * nd:]
    rc = q_ref.shape[0]
    cos = c_ref[...]
    sin = s_ref[...]
    src = lax.broadcasted_iota(jnp.int32, (HEAD_DIM, HEAD_DIM), 0)
    dst = lax.broadcasted_iota(jnp.int32, (HEAD_DIM, HEAD_DIM), 1)
    hr = ROPE_DIM // 2
    swap = (jnp.where((dst < hr) & (src == dst + hr), -1.0, 0.0)
            + jnp.where((dst >= hr) & (dst < ROPE_DIM) & (src == dst - hr), 1.0, 0.0)).astype(BF16)
    ones = jnp.ones((HEAD_DIM, HEAD_DIM), BF16)

    def dot2(a, b):
        hi, lo = _split_bf16(a)
        return _dot(hi, b) + _dot(lo, b)

    def norm_rope(x, g):
        ms = dot2(x * x, ones) * (1.0 / HEAD_DIM)
        y = x * lax.rsqrt(ms + EPS) * g
        return y * cos + dot2(y, swap) * sin

    for h in range(DIL_HEADS):
        hs = slice(h * HEAD_DIM, (h + 1) * HEAD_DIM)
        scr[0][h] = norm_rope(q_ref[:, hs].astype(F32), qg_ref[...]) * (HEAD_DIM ** -0.5)
        scr[1][h] = norm_rope(k_ref[:, hs].astype(F32), kg_ref[...])
        scr[2][h] = v_ref[:, hs].astype(F32)
    for which in range(3):
        for di, d in enumerate(DILATIONS):
            o_ref = outs[which * nd + di]
            for h in range(DIL_HEADS):
                for r in range(d):
                    rows = pl.ds(r, rc // d, stride=d) if d > 1 else slice(None)
                    c0 = r * DIL_W + h * HEAD_DIM
                    o_ref[0, :, c0:c0 + HEAD_DIM] = scr[which].at[h][rows, :].astype(BF16)


def _qk_prep(h, qg, kg, cos, sin, bsz, seq, rc):
    nsteps = seq // rc
    hblk = lambda c: pl.BlockSpec((rc, DIL_W), lambda b, n: (b * nsteps + n, c // DIL_W))
    tab = pl.BlockSpec((rc, HEAD_DIM), lambda b, n: (n, 0))
    gsp = pl.BlockSpec((1, HEAD_DIM), lambda b, n: (0, 0))
    osp = [pl.BlockSpec((1, rc // d, d * DIL_W), lambda b, n: (b, n, 0)) for d in DILATIONS]
    oshape = [jax.ShapeDtypeStruct((bsz, seq // d, d * DIL_W), BF16) for d in DILATIONS]
    outs = pl.pallas_call(
        _qk_prep_kernel,
        grid=(bsz, nsteps),
        in_specs=[hblk(COL_QD), hblk(COL_KD), hblk(COL_VD), gsp, gsp, tab, tab],
        out_specs=osp * 3,
        out_shape=oshape * 3,
        scratch_shapes=[pltpu.VMEM((DIL_HEADS, rc, HEAD_DIM), F32)] * 3,
        compiler_params=_cparams(("parallel", "parallel")),
        name="qk_prep",
    )(h, h, h, qg, kg, cos, sin)
    nd = len(DILATIONS)
    return outs[:nd], outs[nd:2 * nd], outs[2 * nd:]


def _dilated_kernel(q_ref, km_ref, kl_ref, kr_ref, vm_ref, vl_ref, vr_ref, o_ref, lse_ref, *, half):
    n = pl.program_id(2)
    last = pl.num_programs(2) - 1
    tq = q_ref.shape[1]
    qb = 2 * half
    nsb = tq // qb
    r_i = lax.broadcasted_iota(jnp.int32, (qb, 2 * qb), 0)
    c_i = lax.broadcasted_iota(jnp.int32, (qb, 2 * qb), 1)
    band = (c_i >= r_i) & (c_i <= r_i + 2 * half)
    valid = []
    for sb in range(nsb):
        v = band
        if sb == 0:
            v = v & ((c_i >= half) | (n > 0))
        if sb == nsb - 1:
            v = v & ((c_i < 2 * qb - half) | (n < last))
        valid.append(v)
    lane = lax.broadcasted_iota(jnp.int32, (qb, LANES), 1)
    lse_tiles = [jnp.zeros((qb, LANES), F32) for _ in range(nsb)]
    for h in range(DIL_HEADS):
        hs = slice(h * HEAD_DIM, (h + 1) * HEAD_DIM)
        k_h = jnp.concatenate([kl_ref[0, :, hs], km_ref[0, :, hs], kr_ref[0, :, hs]], axis=0)
        v_h = jnp.concatenate([vl_ref[0, :, hs], vm_ref[0, :, hs], vr_ref[0, :, hs]], axis=0)
        for sb in range(nsb):
            q = q_ref[0, sb * qb:(sb + 1) * qb, hs]
            s = jnp.where(valid[sb], _dot_nt(q, k_h[sb * qb:sb * qb + 2 * qb]), MASK_VALUE)
            m = jnp.max(jnp.maximum(s[:, :qb], s[:, qb:]), axis=-1, keepdims=True)
            p = jnp.exp(s - m)
            den = jnp.sum(p[:, :qb] + p[:, qb:], axis=-1, keepdims=True)
            o = _dot(p.astype(BF16), v_h[sb * qb:sb * qb + 2 * qb])
            o_ref[0, sb * qb:(sb + 1) * qb, hs] = (o / den).astype(o_ref.dtype)
            lse_tiles[sb] = jnp.where(lane == h, m + jnp.log(den), lse_tiles[sb])
    for sb in range(nsb):
        lse_ref[0, sb * qb:(sb + 1) * qb, :] = lse_tiles[sb]


def _dilated(qv, kv, vv, window, dilation, tq):
    bsz, ln, _ = qv.shape
    half = window // (2 * dilation)
    nt = ln // tq
    per = tq // half
    nh = ln // half
    main = pl.BlockSpec((1, tq, DIL_W), lambda b, r, n: (b, n, r))
    left = pl.BlockSpec((1, half, DIL_W), lambda b, r, n: (b, jnp.maximum(n * per - 1, 0), r))
    right = pl.BlockSpec((1, half, DIL_W), lambda b, r, n: (b, jnp.minimum((n + 1) * per, nh - 1), r))
    return pl.pallas_call(
        functools.partial(_dilated_kernel, half=half),
        grid=(bsz, dilation, nt),
        in_specs=[main, main, left, right, main, left, right],
        out_specs=[main, pl.BlockSpec((1, tq, LANES), lambda b, r, n: (b, n, r))],
        out_shape=[jax.ShapeDtypeStruct((bsz, ln, dilation * DIL_W), BF16),
                   jax.ShapeDtypeStruct((bsz, ln, dilation * LANES), F32)],
        compiler_params=_cparams(("parallel", "parallel", "parallel")),
        name=f"dilated_d{dilation}",
    )(qv, kv, kv, kv, vv, vv, vv)


def _combine_kernel(of_ref, ob_ref, r_ref, og_ref, o1_ref, o2_ref, o3_ref, l1_ref, l2_ref, l3_ref,
                    out_ref, on_ref, ln_ref):
    for h in range(GLA_HEADS):
        vs = slice(h * GLA_DV, (h + 1) * GLA_DV)
        o = of_ref[:, vs] + ob_ref[:, vs]
        y = o * lax.rsqrt(jnp.mean(o * o, axis=-1, keepdims=True) + EPS) * og_ref[:, vs]
        rr = r_ref[:, vs].astype(F32)
        out_ref[:, vs] = (y * (rr * jax.nn.sigmoid(rr))).astype(BF16)
    rc = of_ref.shape[0]
    for p, (d, o_ref, l_ref) in enumerate(zip(DILATIONS, (o1_ref, o2_ref, o3_ref), (l1_ref, l2_ref, l3_ref))):
        for r in range(d):
            rows = pl.ds(r, rc // d, stride=d) if d > 1 else slice(None)
            ln_ref.at[p][rows, :] = l_ref[0, :, r * LANES:(r + 1) * LANES]
            for h in range(DIL_HEADS):
                c0 = r * DIL_W + h * HEAD_DIM
                on_ref.at[p * DIL_HEADS + h][rows, :] = o_ref[0, :, c0:c0 + HEAD_DIM].astype(F32)
    l1 = ln_ref[0]
    l2 = ln_ref[1]
    l3 = ln_ref[2]
    mx = jnp.maximum(l1, jnp.maximum(l2, l3))
    e1 = jnp.exp(l1 - mx)
    e2 = jnp.exp(l2 - mx)
    e3 = jnp.exp(l3 - mx)
    inv = 1.0 / (e1 + e2 + e3)
    w1 = e1 * inv
    w2 = e2 * inv
    w3 = e3 * inv
    rows = rc
    for h in range(DIL_HEADS):
        hs = slice(h * HEAD_DIM, (h + 1) * HEAD_DIM)
        bc = lambda w: jnp.broadcast_to(w[:, h:h + 1], (rows, HEAD_DIM))
        o = bc(w1) * on_ref[h] + bc(w2) * on_ref[DIL_HEADS + h] + bc(w3) * on_ref[2 * DIL_HEADS + h]
        out_ref[:, GLA_V_W + h * HEAD_DIM:GLA_V_W + (h + 1) * HEAD_DIM] = o.astype(BF16)


def _combine(o_fwd, o_bwd, h, out_g, os, ls, bsz, seq, rc):
    t = o_fwd.shape[0]
    nsteps = seq // rc
    rb = lambda w, c=0: pl.BlockSpec((rc, w), lambda b, n: (b * nsteps + n, c))
    pat = lambda w: [pl.BlockSpec((1, rc // d, d * w), lambda b, n: (b, n, 0)) for d in DILATIONS]
    return pl.pallas_call(
        _combine_kernel,
        grid=(bsz, nsteps),
        in_specs=[rb(GLA_V_W), rb(GLA_V_W), rb(GLA_V_W, COL_RA // GLA_V_W),
                  pl.BlockSpec((1, GLA_V_W), lambda b, n: (0, 0))] + pat(DIL_W) + pat(LANES),
        out_specs=rb(MIX_W),
        out_shape=jax.ShapeDtypeStruct((t, MIX_W), BF16),
        scratch_shapes=[pltpu.VMEM((len(DILATIONS) * DIL_HEADS, rc, HEAD_DIM), F32),
                        pltpu.VMEM((len(DILATIONS), rc, LANES), F32)],
        compiler_params=_cparams(("parallel", "parallel")),
        name="combine",
    )(o_fwd, o_bwd, h, out_g, *os, *ls)


def _outproj_kernel(mix_ref, w_ref, x_ref, g_ref, x2_ref, xnt_ref):
    x2 = x_ref[...] + _dot(mix_ref[...], w_ref[...])
    x2_ref[...] = x2
    xn = x2 * lax.rsqrt(jnp.mean(x2 * x2, axis=-1, keepdims=True) + EPS) * g_ref[...]
    xnt_ref[...] = xn.T.astype(BF16)


def _outproj(mixed, w_out, x, g2, tm):
    t, d = x.shape
    return pl.pallas_call(
        _outproj_kernel,
        grid=(t // tm,),
        in_specs=[
            pl.BlockSpec((tm, MIX_W), lambda i: (i, 0)),
            pl.BlockSpec((MIX_W, d), lambda i: (0, 0)),
            pl.BlockSpec((tm, d), lambda i: (i, 0)),
            pl.BlockSpec((1, d), lambda i: (0, 0)),
        ],
        out_specs=[pl.BlockSpec((tm, d), lambda i: (i, 0)), pl.BlockSpec((d, tm), lambda i: (0, i))],
        out_shape=[jax.ShapeDtypeStruct((t, d), F32), jax.ShapeDtypeStruct((d, t), BF16)],
        compiler_params=_cparams(("parallel",)),
        name="outproj",
    )(mixed, w_out, x, g2)


_NTOP = PEER_TOPK + 1
_CAND = [(a, b) for a in range(_NTOP) for b in range(_NTOP) if (a + 1) * (b + 1) <= _NTOP]
_NCAND_PAD = 64


def _merge_sort_network(n):
    pairs = []
    p = 1
    while p < n:
        k = p
        while k >= 1:
            for j in range(k % p, n - k, 2 * k):
                for i in range(min(k, n - j - k)):
                    if (i + j) // (2 * p) == (i + j + k) // (2 * p):
                        pairs.append((i + j, i + j + k))
            k //= 2
        p *= 2
    return pairs


def _top_values(sc, count):
    rows = sc.shape[0]
    ntile = rows // SUBLANES
    tiles = [sc[k * SUBLANES:(k + 1) * SUBLANES, :] for k in range(ntile)]
    for lo, hi in _merge_sort_network(ntile):
        tiles[lo], tiles[hi] = jnp.maximum(tiles[lo], tiles[hi]), jnp.minimum(tiles[lo], tiles[hi])
    vals = []
    for it in range(count):
        m = jnp.max(tiles[0], axis=0, keepdims=True)
        vals.append(m)
        hit = tiles[0] >= m
        for k in range(min(ntile, count - it - 1)):
            below = tiles[k + 1] if k + 1 < ntile else jnp.full_like(tiles[k], -jnp.inf)
            tiles[k] = jnp.where(hit, below, tiles[k])
    return vals


def _peer_route_kernel(wq_ref, xnt_ref, sk_ref, thr_ref, w1_ref, v2_ref, q_ref, cand_ref):
    q_ref[...] = _dot(wq_ref[...], xnt_ref[...])
    half = PEER_QDIM // 2
    cand_ref[...] = jnp.full(cand_ref.shape, -jnp.inf, F32)

    def head(h, carry):
        sc_a = _dot3(sk_ref[h, 0], q_ref[pl.ds(pl.multiple_of(h * PEER_QDIM, PEER_QDIM), half), :])
        sc_b = _dot3(sk_ref[h, 1], q_ref[pl.ds(pl.multiple_of(h * PEER_QDIM + half, half), half), :])
        top_a = _top_values(sc_a, _NTOP)
        top_b = _top_values(sc_b, _NTOP)
        for i, (a, b) in enumerate(_CAND):
            cand_ref[i:i + 1, :] = top_a[a] + top_b[b]
        cand = cand_ref[...]
        top_c = _top_values(cand, _NTOP)
        tau = 0.5 * (top_c[PEER_TOPK - 1] + top_c[PEER_TOPK])
        best = top_a[0] + top_b[0]
        zsum = jnp.sum(jnp.where(cand > tau, jnp.exp(cand - best), 0.0), axis=0, keepdims=True)
        thr_ref[h] = jnp.exp((tau - sc_a) - top_b[0])
        w1_ref[h] = jnp.exp(sc_a - top_a[0]) * (1.0 / zsum)
        v2_ref[h] = jnp.exp(sc_b - top_b[0])
        return carry

    lax.fori_loop(0, PEER_HEADS, head, 0)


def _peer_route(wq_t, xnt, sub_keys, tb):
    d, t = xnt.shape
    hq = wq_t.shape[0]
    key_blk = pl.BlockSpec((PEER_HEADS, PEER_NKEYS, tb), lambda i: (0, 0, i))
    key_shape = jax.ShapeDtypeStruct((PEER_HEADS, PEER_NKEYS, t), F32)
    return pl.pallas_call(
        _peer_route_kernel,
        grid=(t // tb,),
        in_specs=[
            pl.BlockSpec((hq, d), lambda i: (0, 0)),
            pl.BlockSpec((d, tb), lambda i: (0, i)),
            pl.BlockSpec(sub_keys.shape, lambda i: (0, 0, 0, 0)),
        ],
        out_specs=[key_blk, key_blk, key_blk],
        out_shape=[key_shape, key_shape, key_shape],
        scratch_shapes=[pltpu.VMEM((hq, tb), F32), pltpu.VMEM((_NCAND_PAD, tb), F32)],
        compiler_params=_cparams(("parallel",)),
        name="peer_route",
    )(wq_t, xnt, sub_keys)


SUBLANES = 8


def _peer_expert_kernel(xnt_ref, down_ref, upt_ref, thr_ref, w1_ref, v2_ref, x2_ref,
                        o_ref, acc_ref, a_ref, g_ref, h_ref):
    e = pl.program_id(1)
    eb, tb = a_ref.shape
    rows_i = eb // PEER_NKEYS

    @pl.when(e == 0)
    def _():
        acc_ref[...] = jnp.zeros_like(acc_ref)

    chunk_i = 4
    for ii in range(rows_i):
        if ii % chunk_i == 0:
            cs = slice(ii * PEER_NKEYS, (ii + chunk_i) * PEER_NKEYS)
            a_ref[cs, :] = _dot(down_ref[cs, :], xnt_ref[...])
        for lg in range(tb // LANES):
            ls = slice(lg * LANES, (lg + 1) * LANES)
            thr = [jnp.broadcast_to(thr_ref[h, ii:ii + 1, ls], (SUBLANES, LANES)) for h in range(PEER_HEADS)]
            w1 = [jnp.broadcast_to(w1_ref[h, ii:ii + 1, ls], (SUBLANES, LANES)) for h in range(PEER_HEADS)]
            for j0 in range(0, PEER_NKEYS, SUBLANES):
                js = slice(j0, j0 + SUBLANES)
                gate = jnp.zeros((SUBLANES, LANES), F32)
                for h in range(PEER_HEADS):
                    v2 = v2_ref[h, js, ls]
                    gate = gate + jnp.where(v2 >= thr[h], w1[h] * v2, 0.0)
                g_ref[ii * PEER_NKEYS + j0:ii * PEER_NKEYS + j0 + SUBLANES, ls] = gate

    rows = 2 * SUBLANES
    for r0 in range(0, eb, rows):
        h_ref[r0:r0 + rows, :] = (g_ref[r0:r0 + rows, :] * jax.nn.gelu(a_ref[r0:r0 + rows, :])).astype(BF16)
    mrows = 512
    for m0 in range(0, acc_ref.shape[0], mrows):
        acc_ref[m0:m0 + mrows, :] += _dot(upt_ref[m0:m0 + mrows, :], h_ref[...])

    @pl.when(e == pl.num_programs(1) - 1)
    def _():
        o_ref[...] = x2_ref[...] + acc_ref[...].T


def _peer_experts(xnt, down, up_t, thr, w1, v2, x2, tb, eb):
    d, t = xnt.shape
    ne = down.shape[0]
    rows_i = eb // PEER_NKEYS
    row_blk = pl.BlockSpec((PEER_HEADS, rows_i, tb), lambda i, e: (0, e, i))
    key_blk = pl.BlockSpec((PEER_HEADS, PEER_NKEYS, tb), lambda i, e: (0, 0, i))
    return pl.pallas_call(
        _peer_expert_kernel,
        grid=(t // tb, ne // eb),
        in_specs=[
            pl.BlockSpec((d, tb), lambda i, e: (0, i)),
            pl.BlockSpec((eb, d), lambda i, e: (e, 0)),
            pl.BlockSpec((d, eb), lambda i, e: (0, e)),
            row_blk, row_blk, key_blk,
            pl.BlockSpec((tb, d), lambda i, e: (i, 0), pipeline_mode=pl.Buffered(1)),
        ],
        out_specs=pl.BlockSpec((tb, d), lambda i, e: (i, 0)),
        out_shape=jax.ShapeDtypeStruct((t, d), F32),
        scratch_shapes=[pltpu.VMEM((d, tb), F32), pltpu.VMEM((eb, tb), F32), pltpu.VMEM((eb, tb), F32),
                        pltpu.VMEM((eb, tb), BF16)],
        compiler_params=_cparams(("parallel", "arbitrary")),
        name="peer_experts",
    )(xnt, down, up_t, thr, w1, v2, x2)


def _reorder_in_proj(w_in):
    qa, ka, va, gdf, gdb, ra, qd, kd, vd = jnp.split(w_in, [int(c) for c in np.cumsum(IN_SPLITS)[:-1]], axis=1)
    pad = jnp.zeros((w_in.shape[0], IN_W_PAD - COL_GD - 2 * GLA_GATE_RANK), w_in.dtype)
    return jnp.concatenate([qd, kd, vd, qa, ka, va, ra, gdf, gdb, pad], axis=1).astype(BF16)


def _rope_tables(seq):
    half = ROPE_DIM // 2
    inv_freq = ROPE_THETA ** (-jnp.arange(half, dtype=F32) / half)
    ang = jnp.arange(seq, dtype=jnp.int32).astype(F32)[:, None] * inv_freq[None, :]
    cos = jnp.cos(ang)
    sin = jnp.sin(ang)
    rest = HEAD_DIM - ROPE_DIM
    c = jnp.concatenate([cos, cos, jnp.ones((seq, rest), F32)], axis=1)
    s = jnp.concatenate([sin, sin, jnp.zeros((seq, rest), F32)], axis=1)
    return c, s


def _layer(x, norm1_g, w_in, up_f, bias_f, up_b, bias_b, out_g, q_norm_g, k_norm_g, w_out, norm2_g,
           w_query, sub_keys, down, up):
    bsz, seq, d = x.shape
    t = bsz * seq
    xf = x.reshape(t, d)

    h = _rms_matmul(xf, norm1_g.reshape(1, d), _reorder_in_proj(w_in), tm=1024, tn=IN_TN)

    zpad = jnp.zeros((LANES, GLA_QK_W), F32)
    upf_pad = zpad.at[:GLA_GATE_RANK].set(up_f)
    upb_pad = zpad.at[GLA_GATE_RANK:2 * GLA_GATE_RANK].set(up_b)
    rc = min(256, seq)
    o_fwd, o_bwd = _gla(h, upf_pad, bias_f.reshape(1, -1), upb_pad, bias_b.reshape(1, -1), bsz, seq, rc)

    cos, sin = _rope_tables(seq)
    qs, ks, vs = _qk_prep(h, q_norm_g.reshape(1, -1), k_norm_g.reshape(1, -1), cos, sin, bsz, seq, rc=256)
    outs, lses = [], []
    for p, (window, dilation) in enumerate(DIL_PATTERNS):
        o_p, lse_p = _dilated(qs[p], ks[p], vs[p], window, dilation, tq=min(512, seq // dilation))
        outs.append(o_p)
        lses.append(lse_p)
    mixed = _combine(o_fwd, o_bwd, h, out_g.reshape(1, -1), outs, lses, bsz, seq, rc=512)

    x2, xnt = _outproj(mixed, w_out.astype(BF16), xf, norm2_g.reshape(1, d), tm=512)

    thr, w1, v2 = _peer_route(w_query.T.astype(BF16), xnt, sub_keys, tb=512)
    out = _peer_experts(xnt, down.astype(BF16), up.T.astype(BF16), thr, w1, v2, x2, tb=512, eb=1024)
    return out.reshape(bsz, seq, d)


def kernel(x, norm1_g, w_in, gla_up_f, gla_bias_f, gla_up_b, gla_bias_b, gla_out_g, q_norm_g, k_norm_g, w_out,
           norm2_g, peer_w_query, peer_sub_keys, peer_down, peer_up):
    for l in range(norm1_g.shape[0]):
        x = _layer(x, norm1_g[l], w_in[l], gla_up_f[l], gla_bias_f[l], gla_up_b[l], gla_bias_b[l], gla_out_g[l],
                   q_norm_g[l], k_norm_g[l], w_out[l], norm2_g[l], peer_w_query[l], peer_sub_keys[l],
                   peer_down[l], peer_up[l])
    return x
```

```python
import functools

import jax
import jax.numpy as jnp
import numpy as np
from jax import lax
from jax.experimental import pallas as pl
from jax.experimental.pallas import tpu as pltpu

F32 = jnp.float32
BF16 = jnp.bfloat16

EPS = 1e-6
HEAD_DIM = 128

GLA_HEADS = 4
GLA_DK = 64
GLA_DV = 128
GLA_GATE_RANK = 16
GLA_TAU = 16.0
GLA_CHUNK = 64
GLA_QK_W = GLA_HEADS * GLA_DK
GLA_V_W = GLA_HEADS * GLA_DV

DIL_HEADS = 12
DIL_PATTERNS = ((128, 1), (512, 4), (2048, 16))
DIL_W = DIL_HEADS * HEAD_DIM
ROPE_THETA = 500000.0
ROPE_DIM = HEAD_DIM // 4
MASK_VALUE = -1e30

PEER_HEADS = 8
PEER_NKEYS = 128
PEER_QDIM = 256
PEER_TOPK = 16

MIX_W = GLA_V_W + DIL_W
IN_SPLITS = (GLA_QK_W, GLA_QK_W, GLA_V_W, GLA_GATE_RANK, GLA_GATE_RANK, GLA_V_W, DIL_W, DIL_W, DIL_W)

LANES = 128
SUBLANES = 8
COL_QD = 0
COL_KD = COL_QD + DIL_W
COL_VD = COL_KD + DIL_W
COL_QA = COL_VD + DIL_W
COL_KA = COL_QA + GLA_QK_W
COL_VA = COL_KA + GLA_QK_W
COL_RA = COL_VA + GLA_V_W
COL_GD = COL_RA + GLA_V_W
MXU_COLS = 256
IN_TN = 5 * MXU_COLS
IN_W_PAD = -(-(COL_GD + LANES) // IN_TN) * IN_TN

VMEM_LIMIT = 56 * 1024 * 1024


def _cparams(sem):
    return pltpu.CompilerParams(dimension_semantics=sem, vmem_limit_bytes=VMEM_LIMIT)


def _split_bf16(a):
    hi = a.astype(BF16)
    lo = (a - hi.astype(F32)).astype(BF16)
    return hi, lo


def _dot(a, b):
    return jnp.dot(a, b, preferred_element_type=F32)


def _dot_nt(a, b):
    return lax.dot_general(a, b, (((1,), (1,)), ((), ())), preferred_element_type=F32)


def _dot_tn(a, b):
    return lax.dot_general(a, b, (((0,), (0,)), ((), ())), preferred_element_type=F32)


def _dot3(a, b):
    ah, al = _split_bf16(a)
    bh, bl = _split_bf16(b)
    return _dot(ah, bh) + (_dot(ah, bl) + _dot(al, bh))


def _rms_matmul_kernel(x_ref, g_ref, w_ref, o_ref, xn_ref):
    @pl.when(pl.program_id(1) == 0)
    def _():
        x = x_ref[...]
        ms = jnp.mean(x * x, axis=-1, keepdims=True)
        xn_ref[...] = (x * lax.rsqrt(ms + EPS) * g_ref[...]).astype(BF16)

    o_ref[...] = _dot(xn_ref[...], w_ref[...]).astype(o_ref.dtype)


def _rms_matmul(x, g, w, tm, tn):
    m, k = x.shape
    n = w.shape[1]
    return pl.pallas_call(
        _rms_matmul_kernel,
        grid=(m // tm, n // tn),
        in_specs=[
            pl.BlockSpec((tm, k), lambda i, j: (i, 0)),
            pl.BlockSpec((1, k), lambda i, j: (0, 0)),
            pl.BlockSpec((k, tn), lambda i, j: (0, j)),
        ],
        out_specs=pl.BlockSpec((tm, tn), lambda i, j: (i, j)),
        out_shape=jax.ShapeDtypeStruct((m, n), BF16),
        scratch_shapes=[pltpu.VMEM((tm, k), BF16)],
        compiler_params=_cparams(("parallel", "arbitrary")),
        name="rms_inproj",
    )(x, g, w)


def _log_sigmoid(z):
    return -(jnp.maximum(-z, 0.0) + jnp.log1p(jnp.exp(-jnp.abs(z))))


def _gla_kernel(qf, kf, vf, gf, upf, bf, qb, kb, vb, gb, upb, bb, of_ref, ob_ref, stf_ref, stb_ref):
    @pl.when(pl.program_id(1) == 0)
    def _():
        stf_ref[...] = jnp.zeros_like(stf_ref)
        stb_ref[...] = jnp.zeros_like(stb_ref)

    both = (0, 1)
    q_refs, k_refs, v_refs, gd_refs = (qf, qb), (kf, kb), (vf, vb), (gf, gb)
    up_refs, bias_refs, o_refs, st_refs = (upf, upb), (bf, bb), (of_ref, ob_ref), (stf_ref, stb_ref)
    rc = qf.shape[0]
    nchunk = rc // GLA_CHUNK
    ups = [_split_bf16(up_refs[d][...]) for d in both]
    z = [_dot(gd_refs[d][...], ups[d][0]) + _dot(gd_refs[d][...], ups[d][1]) + bias_refs[d][...] for d in both]
    la = [_log_sigmoid(z[d]) * (1.0 / GLA_TAU) for d in both]
    row = lax.broadcasted_iota(jnp.int32, (rc, rc), 0)
    col = lax.broadcasted_iota(jnp.int32, (rc, rc), 1)
    same = (row // GLA_CHUNK) == (col // GLA_CHUNK)
    tri = [same & (col <= row), same & (col >= row)]
    las = [_split_bf16(la[d]) for d in both]
    sums = [jnp.concatenate([tri[d].astype(BF16), same.astype(BF16)], axis=0) for d in both]
    bt = [_dot(sums[d], las[d][0]) + _dot(sums[d], las[d][1]) for d in both]
    b = [bt[d][:rc] for d in both]
    tot = [bt[d][rc:] for d in both]
    q = [q_refs[d][...].astype(F32) for d in both]
    k = [k_refs[d][...].astype(F32) for d in both]
    q_in = [(q[d] * jnp.exp(b[d]) * (GLA_DK ** -0.5)).astype(BF16) for d in both]
    k_in = [(k[d] * jnp.exp(-b[d])).astype(BF16) for d in both]
    k_st = [(k[d] * jnp.exp(tot[d] - b[d])).astype(BF16) for d in both]
    decay = [jnp.exp(tot[d]) for d in both]
    v = [v_refs[d][...].astype(BF16) for d in both]

    def head_blocks(shape, row_w, col_w):
        r = lax.broadcasted_iota(jnp.int32, shape, 0) // row_w
        c = lax.broadcasted_iota(jnp.int32, shape, 1) // col_w
        return r == c

    hc = GLA_HEADS * GLA_CHUNK
    kk_blocks = head_blocks((hc, GLA_QK_W), GLA_CHUNK, GLA_DK)
    kv_blocks = head_blocks((hc, GLA_V_W), GLA_CHUNK, GLA_DV)
    st_blocks = head_blocks((GLA_V_W, GLA_QK_W), GLA_DV, GLA_DK)
    r_i = lax.broadcasted_iota(jnp.int32, (GLA_CHUNK, hc), 0)
    j_i = lax.broadcasted_iota(jnp.int32, (GLA_CHUNK, hc), 1) % GLA_CHUNK
    mask = [j_i <= r_i, j_i > r_i]
    zero_b = jnp.zeros((), BF16)

    for step in range(nchunk):
        cs = (step, nchunk - 1 - step)
        rs = [slice(cs[d] * GLA_CHUNK, (cs[d] + 1) * GLA_CHUNK) for d in both]
        qc = [q_in[d][rs[d], :] for d in both]
        vc = [v[d][rs[d], :] for d in both]
        k_bd = [jnp.where(kk_blocks, jnp.concatenate([k_in[d][rs[d], :]] * GLA_HEADS, axis=0), zero_b)
                for d in both]
        v_bd = [jnp.where(kv_blocks, jnp.concatenate([vc[d]] * GLA_HEADS, axis=0), zero_b) for d in both]
        sc = [_dot_nt(qc[d], k_bd[d]) for d in both]
        p = [jnp.where(mask[d], sc[d], 0.0).astype(BF16) for d in both]
        st = [st_refs[d][...] for d in both]
        o_all = [_dot(p[d], v_bd[d]) + _dot_nt(qc[d], st[d].astype(BF16)) for d in both]
        inc = [jnp.where(st_blocks, _dot_tn(vc[d], k_st[d][rs[d], :]), 0.0) for d in both]
        for d in both:
            row0 = cs[d] * GLA_CHUNK
            st_refs[d][...] = decay[d][row0:row0 + 1, :] * st[d] + inc[d]
            o_refs[d][rs[d], :] = o_all[d]


def _gla(h, upf_pad, bias_f, upb_pad, bias_b, bsz, seq, rc):
    nsteps = seq // rc

    def specs(rowblk):
        return [
            pl.BlockSpec((rc, GLA_QK_W), lambda b, n: (rowblk(b, n), COL_QA // GLA_QK_W)),
            pl.BlockSpec((rc, GLA_QK_W), lambda b, n: (rowblk(b, n), COL_KA // GLA_QK_W)),
            pl.BlockSpec((rc, GLA_V_W), lambda b, n: (rowblk(b, n), COL_VA // GLA_V_W)),
            pl.BlockSpec((rc, LANES), lambda b, n: (rowblk(b, n), COL_GD // LANES)),
            pl.BlockSpec((LANES, GLA_QK_W), lambda b, n: (0, 0)),
            pl.BlockSpec((1, GLA_QK_W), lambda b, n: (0, 0)),
        ]

    fwd_blk = lambda b, n: b * nsteps + n
    bwd_blk = lambda b, n: b * nsteps + (nsteps - 1 - n)
    oshape = jax.ShapeDtypeStruct((bsz * seq, GLA_V_W), F32)
    return pl.pallas_call(
        _gla_kernel,
        grid=(bsz, nsteps),
        in_specs=specs(fwd_blk) + specs(bwd_blk),
        out_specs=[pl.BlockSpec((rc, GLA_V_W), lambda b, n: (fwd_blk(b, n), 0)),
                   pl.BlockSpec((rc, GLA_V_W), lambda b, n: (bwd_blk(b, n), 0))],
        out_shape=[oshape, oshape],
        scratch_shapes=[pltpu.VMEM((GLA_V_W, GLA_QK_W), F32)] * 2,
        compiler_params=_cparams(("parallel", "arbitrary")),
        name="gla",
    )(h, h, h, h, upf_pad, bias_f, h, h, h, h, upb_pad, bias_b)


DILATIONS = tuple(d for _, d in DIL_PATTERNS)


def _qk_prep_kernel(q_ref, k_ref, v_ref, qg_ref, kg_ref, c_ref, s_ref, *rest):
    nd = len(DILATIONS)
    outs = rest[:3 * nd]
    scr = rest[3 * nd:]
    rc = q_ref.shape[0]
    cos = c_ref[...]
    sin = s_ref[...]
    src = lax.broadcasted_iota(jnp.int32, (HEAD_DIM, HEAD_DIM), 0)
    dst = lax.broadcasted_iota(jnp.int32, (HEAD_DIM, HEAD_DIM), 1)
    hr = ROPE_DIM // 2
    swap = (jnp.where((dst < hr) & (src == dst + hr), -1.0, 0.0)
            + jnp.where((dst >= hr) & (dst < ROPE_DIM) & (src == dst - hr), 1.0, 0.0)).astype(BF16)
    ones = jnp.ones((HEAD_DIM, HEAD_DIM), BF16)

    def dot2(a, b):
        hi, lo = _split_bf16(a)
        return _dot(hi, b) + _dot(lo, b)

    def norm_rope(x, g):
        ms = dot2(x * x, ones) * (1.0 / HEAD_DIM)
        y = x * lax.rsqrt(ms + EPS) * g
        return y * cos + dot2(y, swap) * sin

    for h in range(DIL_HEADS):
        hs = slice(h * HEAD_DIM, (h + 1) * HEAD_DIM)
        scr[0][h] = norm_rope(q_ref[:, hs].astype(F32), qg_ref[...]) * (HEAD_DIM ** -0.5)
        scr[1][h] = norm_rope(k_ref[:, hs].astype(F32), kg_ref[...])
        scr[2][h] = v_ref[:, hs].astype(F32)
    for which in range(3):
        for di, d in enumerate(DILATIONS):
            o_ref = outs[which * nd + di]
            for h in range(DIL_HEADS):
                for r in range(d):
                    rows = pl.ds(r, rc // d, stride=d) if d > 1 else slice(None)
                    c0 = r * DIL_W + h * HEAD_DIM
                    o_ref[0, :, c0:c0 + HEAD_DIM] = scr[which].at[h][rows, :].astype(BF16)


def _qk_prep(h, qg, kg, cos, sin, bsz, seq, rc):
    nsteps = seq // rc
    hblk = lambda c: pl.BlockSpec((rc, DIL_W), lambda b, n: (b * nsteps + n, c // DIL_W))
    tab = pl.BlockSpec((rc, HEAD_DIM), lambda b, n: (n, 0))
    gsp = pl.BlockSpec((1, HEAD_DIM), lambda b, n: (0, 0))
    osp = [pl.BlockSpec((1, rc // d, d * DIL_W), lambda b, n: (b, n, 0)) for d in DILATIONS]
    oshape = [jax.ShapeDtypeStruct((bsz, seq // d, d * DIL_W), BF16) for d in DILATIONS]
    outs = pl.pallas_call(
        _qk_prep_kernel,
        grid=(bsz, nsteps),
        in_specs=[hblk(COL_QD), hblk(COL_KD), hblk(COL_VD), gsp, gsp, tab, tab],
        out_specs=osp * 3,
        out_shape=oshape * 3,
        scratch_shapes=[pltpu.VMEM((DIL_HEADS, rc, HEAD_DIM), F32)] * 3,
        compiler_params=_cparams(("parallel", "parallel")),
        name="qk_prep",
    )(h, h, h, qg, kg, cos, sin)
    nd = len(DILATIONS)
    return outs[:nd], outs[nd:2 * nd], outs[2 * nd:]


def _dilated_kernel(q_ref, km_ref, kl_ref, kr_ref, vm_ref, vl_ref, vr_ref, o_ref, lse_ref, *, half):
    n = pl.program_id(2)
    last = pl.num_programs(2) - 1
    tq = q_ref.shape[1]
    qb = 2 * half
    nsb = tq // qb
    r_i = lax.broadcasted_iota(jnp.int32, (qb, 2 * qb), 0)
    c_i = lax.broadcasted_iota(jnp.int32, (qb, 2 * qb), 1)
    band = (c_i >= r_i) & (c_i <= r_i + 2 * half)
    valid = []
    for sb in range(nsb):
        v = band
        if sb == 0:
            v = v & ((c_i >= half) | (n > 0))
        if sb == nsb - 1:
            v = v & ((c_i < 2 * qb - half) | (n < last))
        valid.append(v)
    lane = lax.broadcasted_iota(jnp.int32, (qb, LANES), 1)
    lse_tiles = [jnp.zeros((qb, LANES), F32) for _ in range(nsb)]
    for h in range(DIL_HEADS):
        hs = slice(h * HEAD_DIM, (h + 1) * HEAD_DIM)
        k_h = jnp.concatenate([kl_ref[0, :, hs], km_ref[0, :, hs], kr_ref[0, :, hs]], axis=0)
        v_h = jnp.concatenate([vl_ref[0, :, hs], vm_ref[0, :, hs], vr_ref[0, :, hs]], axis=0)
        for sb in range(nsb):
            q = q_ref[0, sb * qb:(sb + 1) * qb, hs]
            s = jnp.where(valid[sb], _dot_nt(q, k_h[sb * qb:sb * qb + 2 * qb]), MASK_VALUE)
            m = jnp.max(jnp.maximum(s[:, :qb], s[:, qb:]), axis=-1, keepdims=True)
            p = jnp.exp(s - m)
            den = jnp.sum(p[:, :qb] + p[:, qb:], axis=-1, keepdims=True)
            o = _dot(p.astype(BF16), v_h[sb * qb:sb * qb + 2 * qb])
            o_ref[0, sb * qb:(sb + 1) * qb, hs] = (o / den).astype(o_ref.dtype)
            lse_tiles[sb] = jnp.where(lane == h, m + jnp.log(den), lse_tiles[sb])
    for sb in range(nsb):
        lse_ref[0, sb * qb:(sb + 1) * qb, :] = lse_tiles[sb]


def _dilated(qv, kv, vv, window, dilation, tq):
    bsz, ln, _ = qv.shape
    half = window // (2 * dilation)
    nt = ln // tq
    per = tq // half
    nh = ln // half
    main = pl.BlockSpec((1, tq, DIL_W), lambda b, r, n: (b, n, r))
    left = pl.BlockSpec((1, half, DIL_W), lambda b, r, n: (b, jnp.maximum(n * per - 1, 0), r))
    right = pl.BlockSpec((1, half, DIL_W), lambda b, r, n: (b, jnp.minimum((n + 1) * per, nh - 1), r))
    return pl.pallas_call(
        functools.partial(_dilated_kernel, half=half),
        grid=(bsz, dilation, nt),
        in_specs=[main, main, left, right, main, left, right],
        out_specs=[main, pl.BlockSpec((1, tq, LANES), lambda b, r, n: (b, n, r))],
        out_shape=[jax.ShapeDtypeStruct((bsz, ln, dilation * DIL_W), BF16),
                   jax.ShapeDtypeStruct((bsz, ln, dilation * LANES), F32)],
        compiler_params=_cparams(("parallel", "parallel", "parallel")),
        name=f"dilated_d{dilation}",
    )(qv, kv, kv, kv, vv, vv, vv)


def _combine_kernel(of_ref, ob_ref, r_ref, og_ref, o1_ref, o2_ref, o3_ref, l1_ref, l2_ref, l3_ref,
                    out_ref, on_ref, ln_ref):
    for h in range(GLA_HEADS):
        vs = slice(h * GLA_DV, (h + 1) * GLA_DV)
        o = of_ref[:, vs] + ob_ref[:, vs]
        y = o * lax.rsqrt(jnp.mean(o * o, axis=-1, keepdims=True) + EPS) * og_ref[:, vs]
        rr = r_ref[:, vs].astype(F32)
        out_ref[:, vs] = (y * (rr * jax.nn.sigmoid(rr))).astype(BF16)
    rc = of_ref.shape[0]
    for p, (d, o_ref, l_ref) in enumerate(zip(DILATIONS, (o1_ref, o2_ref, o3_ref), (l1_ref, l2_ref, l3_ref))):
        for r in range(d):
            rows = pl.ds(r, rc // d, stride=d) if d > 1 else slice(None)
            ln_ref.at[p][rows, :] = l_ref[0, :, r * LANES:(r + 1) * LANES]
            for h in range(DIL_HEADS):
                c0 = r * DIL_W + h * HEAD_DIM
                on_ref.at[p * DIL_HEADS + h][rows, :] = o_ref[0, :, c0:c0 + HEAD_DIM].astype(F32)
    l1 = ln_ref[0]
    l2 = ln_ref[1]
    l3 = ln_ref[2]
    mx = jnp.maximum(l1, jnp.maximum(l2, l3))
    e1 = jnp.exp(l1 - mx)
    e2 = jnp.exp(l2 - mx)
    e3 = jnp.exp(l3 - mx)
    inv = 1.0 / (e1 + e2 + e3)
    w1 = e1 * inv
    w2 = e2 * inv
    w3 = e3 * inv
    rows = rc
    for h in range(DIL_HEADS):
        hs = slice(h * HEAD_DIM, (h + 1) * HEAD_DIM)
        bc = lambda w: jnp.broadcast_to(w[:, h:h + 1], (rows, HEAD_DIM))
        o = bc(w1) * on_ref[h] + bc(w2) * on_ref[DIL_HEADS + h] + bc(w3) * on_ref[2 * DIL_HEADS + h]
        out_ref[:, GLA_V_W + h * HEAD_DIM:GLA_V_W + (h + 1) * HEAD_DIM] = o.astype(BF16)


def _combine(o_fwd, o_bwd, h, out_g, os, ls, bsz, seq, rc):
    t = o_fwd.shape[0]
    nsteps = seq // rc
    rb = lambda w, c=0: pl.BlockSpec((rc, w), lambda b, n: (b * nsteps + n, c))
    pat = lambda w: [pl.BlockSpec((1, rc // d, d * w), lambda b, n: (b, n, 0)) for d in DILATIONS]
    return pl.pallas_call(
        _combine_kernel,
        grid=(bsz, nsteps),
        in_specs=[rb(GLA_V_W), rb(GLA_V_W), rb(GLA_V_W, COL_RA // GLA_V_W),
                  pl.BlockSpec((1, GLA_V_W), lambda b, n: (0, 0))] + pat(DIL_W) + pat(LANES),
        out_specs=rb(MIX_W),
        out_shape=jax.ShapeDtypeStruct((t, MIX_W), BF16),
        scratch_shapes=[pltpu.VMEM((len(DILATIONS) * DIL_HEADS, rc, HEAD_DIM), F32),
                        pltpu.VMEM((len(DILATIONS), rc, LANES), F32)],
        compiler_params=_cparams(("parallel", "parallel")),
        name="combine",
    )(o_fwd, o_bwd, h, out_g, *os, *ls)


def _outproj_kernel(mix_ref, w_ref, x_ref, g_ref, x2_ref, xnt_ref):
    x2 = x_ref[...] + _dot(mix_ref[...], w_ref[...])
    x2_ref[...] = x2
    xn = x2 * lax.rsqrt(jnp.mean(x2 * x2, axis=-1, keepdims=True) + EPS) * g_ref[...]
    xnt_ref[...] = xn.T.astype(BF16)


def _outproj(mixed, w_out, x, g2, tm):
    t, d = x.shape
    return pl.pallas_call(
        _outproj_kernel,
        grid=(t // tm,),
        in_specs=[
            pl.BlockSpec((tm, MIX_W), lambda i: (i, 0)),
            pl.BlockSpec((MIX_W, d), lambda i: (0, 0)),
            pl.BlockSpec((tm, d), lambda i: (i, 0)),
            pl.BlockSpec((1, d), lambda i: (0, 0)),
        ],
        out_specs=[pl.BlockSpec((tm, d), lambda i: (i, 0)), pl.BlockSpec((d, tm), lambda i: (0, i))],
        out_shape=[jax.ShapeDtypeStruct((t, d), F32), jax.ShapeDtypeStruct((d, t), BF16)],
        compiler_params=_cparams(("parallel",)),
        name="outproj",
    )(mixed, w_out, x, g2)


_NTOP = PEER_TOPK + 1
_CAND = [(a, b) for a in range(_NTOP) for b in range(_NTOP) if (a + 1) * (b + 1) <= _NTOP]
_NCAND_PAD = 64


def _merge_sort_network(n):
    pairs = []
    p = 1
    while p < n:
        k = p
        while k >= 1:
            for j in range(k % p, n - k, 2 * k):
                for i in range(min(k, n - j - k)):
                    if (i + j) // (2 * p) == (i + j + k) // (2 * p):
                        pairs.append((i + j, i + j + k))
            k //= 2
        p *= 2
    return pairs


def _top_values(sc, count):
    rows = sc.shape[0]
    ntile = rows // SUBLANES
    tiles = [sc[k * SUBLANES:(k + 1) * SUBLANES, :] for k in range(ntile)]
    for lo, hi in _merge_sort_network(ntile):
        tiles[lo], tiles[hi] = jnp.maximum(tiles[lo], tiles[hi]), jnp.minimum(tiles[lo], tiles[hi])
    vals = []
    for it in range(count):
        m = jnp.max(tiles[0], axis=0, keepdims=True)
        vals.append(m)
        hit = tiles[0] >= m
        for k in range(min(ntile, count - it - 1)):
            below = tiles[k + 1] if k + 1 < ntile else jnp.full_like(tiles[k], -jnp.inf)
            tiles[k] = jnp.where(hit, below, tiles[k])
    return vals


def _peer_route_kernel(wq_ref, xnt_ref, sk_ref, thr_ref, w1_ref, v2_ref, q_ref, cand_ref):
    q_ref[...] = _dot(wq_ref[...], xnt_ref[...])
    half = PEER_QDIM // 2
    cand_ref[...] = jnp.full(cand_ref.shape, -jnp.inf, F32)

    def head(h, carry):
        sc_a = _dot3(sk_ref[h, 0], q_ref[pl.ds(pl.multiple_of(h * PEER_QDIM, PEER_QDIM), half), :])
        sc_b = _dot3(sk_ref[h, 1], q_ref[pl.ds(pl.multiple_of(h * PEER_QDIM + half, half), half), :])
        top_a = _top_values(sc_a, _NTOP)
        top_b = _top_values(sc_b, _NTOP)
        for i, (a, b) in enumerate(_CAND):
            cand_ref[i:i + 1, :] = top_a[a] + top_b[b]
        cand = cand_ref[...]
        top_c = _top_values(cand, _NTOP)
        tau = 0.5 * (top_c[PEER_TOPK - 1] + top_c[PEER_TOPK])
        best = top_a[0] + top_b[0]
        zsum = jnp.sum(jnp.where(cand > tau, jnp.exp(cand - best), 0.0), axis=0, keepdims=True)
        thr_ref[h] = jnp.exp((tau - sc_a) - top_b[0])
        w1_ref[h] = jnp.exp(sc_a - top_a[0]) * (1.0 / zsum)
        v2_ref[h] = jnp.exp(sc_b - top_b[0])
        return carry

    lax.fori_loop(0, PEER_HEADS, head, 0)


def _peer_route(wq_t, xnt, sub_keys, tb):
    d, t = xnt.shape
    hq = wq_t.shape[0]
    key_blk = pl.BlockSpec((PEER_HEADS, PEER_NKEYS, tb), lambda i: (0, 0, i))
    key_shape = jax.ShapeDtypeStruct((PEER_HEADS, PEER_NKEYS, t), F32)
    return pl.pallas_call(
        _peer_route_kernel,
        grid=(t // tb,),
        in_specs=[
            pl.BlockSpec((hq, d), lambda i: (0, 0)),
            pl.BlockSpec((d, tb), lambda i: (0, i)),
            pl.BlockSpec(sub_keys.shape, lambda i: (0, 0, 0, 0)),
        ],
        out_specs=[key_blk, key_blk, key_blk],
        out_shape=[key_shape, key_shape, key_shape],
        scratch_shapes=[pltpu.VMEM((hq, tb), F32), pltpu.VMEM((_NCAND_PAD, tb), F32)],
        compiler_params=_cparams(("parallel",)),
        name="peer_route",
    )(wq_t, xnt, sub_keys)


def _peer_expert_kernel(xnt_ref, down_ref, upt_ref, thr_ref, w1_ref, v2_ref, x2_ref,
                        o_ref, acc_ref, a_ref, h_ref):
    e = pl.program_id(1)
    eb, tb = a_ref.shape
    rows_i = eb // PEER_NKEYS

    @pl.when(e == 0)
    def _():
        acc_ref[...] = jnp.zeros_like(acc_ref)

    chunk_i = 2
    for ii in range(rows_i):
        if ii % chunk_i == 0:
            cs = slice(ii * PEER_NKEYS, (ii + chunk_i) * PEER_NKEYS)
            a_ref[cs, :] = _dot(down_ref[cs, :], xnt_ref[...])
        for lg in range(tb // LANES):
            ls = slice(lg * LANES, (lg + 1) * LANES)
            thr = [jnp.broadcast_to(thr_ref[h, ii:ii + 1, ls], (SUBLANES, LANES)) for h in range(PEER_HEADS)]
            w1 = [jnp.broadcast_to(w1_ref[h, ii:ii + 1, ls], (SUBLANES, LANES)) for h in range(PEER_HEADS)]
            for j0 in range(0, PEER_NKEYS, 2 * SUBLANES):
                halves = []
                for js0 in (j0, j0 + SUBLANES):
                    js = slice(js0, js0 + SUBLANES)
                    gate = jnp.zeros((SUBLANES, LANES), F32)
                    for h in range(PEER_HEADS):
                        v2 = v2_ref[h, js, ls]
                        gate = gate + jnp.where(v2 >= thr[h], w1[h] * v2, 0.0)
                    rs = slice(ii * PEER_NKEYS + js0, ii * PEER_NKEYS + js0 + SUBLANES)
                    halves.append(gate * jax.nn.gelu(a_ref[rs, ls]))
                rs = slice(ii * PEER_NKEYS + j0, ii * PEER_NKEYS + j0 + 2 * SUBLANES)
                h_ref[rs, ls] = jnp.concatenate(halves, axis=0).astype(BF16)
    mrows = 512
    for m0 in range(0, acc_ref.shape[0], mrows):
        acc_ref[m0:m0 + mrows, :] += _dot(upt_ref[m0:m0 + mrows, :], h_ref[...])

    @pl.when(e == pl.num_programs(1) - 1)
    def _():
        o_ref[...] = x2_ref[...] + acc_ref[...].T


def _peer_experts(xnt, down, up_t, thr, w1, v2, x2, tb, eb):
    d, t = xnt.shape
    ne = down.shape[0]
    rows_i = eb // PEER_NKEYS
    row_blk = pl.BlockSpec((PEER_HEADS, rows_i, tb), lambda i, e: (0, e, i))
    key_blk = pl.BlockSpec((PEER_HEADS, PEER_NKEYS, tb), lambda i, e: (0, 0, i))
    return pl.pallas_call(
        _peer_expert_kernel,
        grid=(t // tb, ne // eb),
        in_specs=[
            pl.BlockSpec((d, tb), lambda i, e: (0, i)),
            pl.BlockSpec((eb, d), lambda i, e: (e, 0)),
            pl.BlockSpec((d, eb), lambda i, e: (0, e)),
            row_blk, row_blk, key_blk,
            pl.BlockSpec((tb, d), lambda i, e: (i, 0)),
        ],
        out_specs=pl.BlockSpec((tb, d), lambda i, e: (i, 0)),
        out_shape=jax.ShapeDtypeStruct((t, d), F32),
        scratch_shapes=[pltpu.VMEM((d, tb), F32), pltpu.VMEM((eb, tb), F32), pltpu.VMEM((eb, tb), BF16)],
        compiler_params=_cparams(("parallel", "arbitrary")),
        name="peer_experts",
    )(xnt, down, up_t, thr, w1, v2, x2)


def _reorder_in_proj(w_in):
    qa, ka, va, gdf, gdb, ra, qd, kd, vd = jnp.split(w_in, [int(c) for c in np.cumsum(IN_SPLITS)[:-1]], axis=1)
    pad = jnp.zeros((w_in.shape[0], IN_W_PAD - COL_GD - 2 * GLA_GATE_RANK), w_in.dtype)
    return jnp.concatenate([qd, kd, vd, qa, ka, va, ra, gdf, gdb, pad], axis=1).astype(BF16)


def _rope_tables(seq):
    half = ROPE_DIM // 2
    inv_freq = ROPE_THETA ** (-jnp.arange(half, dtype=F32) / half)
    ang = jnp.arange(seq, dtype=jnp.int32).astype(F32)[:, None] * inv_freq[None, :]
    cos = jnp.cos(ang)
    sin = jnp.sin(ang)
    rest = HEAD_DIM - ROPE_DIM
    c = jnp.concatenate([cos, cos, jnp.ones((seq, rest), F32)], axis=1)
    s = jnp.concatenate([sin, sin, jnp.zeros((seq, rest), F32)], axis=1)
    return c, s


def _tiles(bsz, seq):
    t = bsz * seq
    max_dil = max(DILATIONS)
    return dict(
        inproj_rows=min(1024, t),
        gla_rows=min(256, seq),
        prep_rows=min(16 * max_dil, seq),
        attn_rows=min(512, seq // max_dil),
        combine_rows=min(512, seq),
        outproj_rows=min(512, t),
        peer_tokens=min(2 * MXU_COLS, t),
        peer_experts=8 * PEER_NKEYS,
    )


def _layer(x, norm1_g, w_in, up_f, bias_f, up_b, bias_b, out_g, q_norm_g, k_norm_g, w_out, norm2_g,
           w_query, sub_keys, down, up):
    bsz, seq, d = x.shape
    t = bsz * seq
    xf = x.reshape(t, d)
    tl = _tiles(bsz, seq)

    h = _rms_matmul(xf, norm1_g.reshape(1, d), _reorder_in_proj(w_in), tm=tl["inproj_rows"], tn=IN_TN)

    zpad = jnp.zeros((LANES, GLA_QK_W), F32)
    upf_pad = zpad.at[:GLA_GATE_RANK].set(up_f)
    upb_pad = zpad.at[GLA_GATE_RANK:2 * GLA_GATE_RANK].set(up_b)
    o_fwd, o_bwd = _gla(h, upf_pad, bias_f.reshape(1, -1), upb_pad, bias_b.reshape(1, -1), bsz, seq,
                        tl["gla_rows"])

    cos, sin = _rope_tables(seq)
    qs, ks, vs = _qk_prep(h, q_norm_g.reshape(1, -1), k_norm_g.reshape(1, -1), cos, sin, bsz, seq,
                          rc=tl["prep_rows"])
    outs, lses = [], []
    for p, (window, dilation) in enumerate(DIL_PATTERNS):
        o_p, lse_p = _dilated(qs[p], ks[p], vs[p], window, dilation, tq=min(tl["attn_rows"], seq // dilation))
        outs.append(o_p)
        lses.append(lse_p)
    mixed = _combine(o_fwd, o_bwd, h, out_g.reshape(1, -1), outs, lses, bsz, seq, rc=tl["combine_rows"])

    x2, xnt = _outproj(mixed, w_out.astype(BF16), xf, norm2_g.reshape(1, d), tm=tl["outproj_rows"])

    thr, w1, v2 = _peer_route(w_query.T.astype(BF16), xnt, sub_keys, tb=tl["peer_tokens"])
    out = _peer_experts(xnt, down.astype(BF16), up.T.astype(BF16), thr, w1, v2, x2,
                        tb=tl["peer_tokens"], eb=tl["peer_experts"])
    return out.reshape(bsz, seq, d)


def kernel(x, norm1_g, w_in, gla_up_f, gla_bias_f, gla_up_b, gla_bias_b, gla_out_g, q_norm_g, k_norm_g, w_out,
           norm2_g, peer_w_query, peer_sub_keys, peer_down, peer_up):
    for l in range(norm1_g.shape[0]):
        x = _layer(x, norm1_g[l], w_in[l], gla_up_f[l], gla_bias_f[l], gla_up_b[l], gla_bias_b[l], gla_out_g[l],
                   q_norm_g[l], k_norm_g[l], w_out[l], norm2_g[l], peer_w_query[l], peer_sub_keys[l],
                   peer_down[l], peer_up[l])
    return x
```

```python
import functools

import jax
import jax.numpy as jnp
import numpy as np
from jax import lax
from jax.experimental import pallas as pl
from jax.experimental.pallas import tpu as pltpu

F32 = jnp.float32
BF16 = jnp.bfloat16

EPS = 1e-6
HEAD_DIM = 128

GLA_HEADS = 4
GLA_DK = 64
GLA_DV = 128
GLA_GATE_RANK = 16
GLA_TAU = 16.0
GLA_CHUNK = 64
GLA_QK_W = GLA_HEADS * GLA_DK
GLA_V_W = GLA_HEADS * GLA_DV

DIL_HEADS = 12
DIL_PATTERNS = ((128, 1), (512, 4), (2048, 16))
DIL_W = DIL_HEADS * HEAD_DIM
ROPE_THETA = 500000.0
ROPE_DIM = HEAD_DIM // 4
MASK_VALUE = -1e30

PEER_HEADS = 8
PEER_NKEYS = 128
PEER_QDIM = 256
PEER_TOPK = 16

MIX_W = GLA_V_W + DIL_W
IN_SPLITS = (GLA_QK_W, GLA_QK_W, GLA_V_W, GLA_GATE_RANK, GLA_GATE_RANK, GLA_V_W, DIL_W, DIL_W, DIL_W)

LANES = 128
SUBLANES = 8
COL_QD = 0
COL_KD = COL_QD + DIL_W
COL_VD = COL_KD + DIL_W
COL_QA = COL_VD + DIL_W
COL_KA = COL_QA + GLA_QK_W
COL_VA = COL_KA + GLA_QK_W
COL_RA = COL_VA + GLA_V_W
COL_GD = COL_RA + GLA_V_W
MXU_COLS = 256
IN_TN = 5 * MXU_COLS
IN_W_PAD = -(-(COL_GD + LANES) // IN_TN) * IN_TN

VMEM_LIMIT = 56 * 1024 * 1024


def _cparams(sem):
    return pltpu.CompilerParams(dimension_semantics=sem, vmem_limit_bytes=VMEM_LIMIT)


def _split_bf16(a):
    hi = a.astype(BF16)
    lo = (a - hi.astype(F32)).astype(BF16)
    return hi, lo


def _dot(a, b):
    return jnp.dot(a, b, preferred_element_type=F32)


def _dot_nt(a, b):
    return lax.dot_general(a, b, (((1,), (1,)), ((), ())), preferred_element_type=F32)


def _dot_tn(a, b):
    return lax.dot_general(a, b, (((0,), (0,)), ((), ())), preferred_element_type=F32)


def _dot3(a, b):
    ah, al = _split_bf16(a)
    bh, bl = _split_bf16(b)
    return _dot(ah, bh) + (_dot(ah, bl) + _dot(al, bh))


def _rms_matmul_kernel(x_ref, g_ref, w_ref, o_ref, xn_ref):
    @pl.when(pl.program_id(1) == 0)
    def _():
        x = x_ref[...]
        ms = jnp.mean(x * x, axis=-1, keepdims=True)
        xn_ref[...] = (x * lax.rsqrt(ms + EPS) * g_ref[...]).astype(BF16)

    o_ref[...] = _dot(xn_ref[...], w_ref[...]).astype(o_ref.dtype)


def _rms_matmul(x, g, w, tm, tn):
    m, k = x.shape
    n = w.shape[1]
    return pl.pallas_call(
        _rms_matmul_kernel,
        grid=(m // tm, n // tn),
        in_specs=[
            pl.BlockSpec((tm, k), lambda i, j: (i, 0)),
            pl.BlockSpec((1, k), lambda i, j: (0, 0)),
            pl.BlockSpec((k, tn), lambda i, j: (0, j)),
        ],
        out_specs=pl.BlockSpec((tm, tn), lambda i, j: (i, j)),
        out_shape=jax.ShapeDtypeStruct((m, n), BF16),
        scratch_shapes=[pltpu.VMEM((tm, k), BF16)],
        compiler_params=_cparams(("parallel", "arbitrary")),
        name="rms_inproj",
    )(x, g, w)


def _log_sigmoid(z):
    return -(jnp.maximum(-z, 0.0) + jnp.log1p(jnp.exp(-jnp.abs(z))))


def _gla_kernel(qf, kf, vf, gf, upf, bf, qb, kb, vb, gb, upb, bb, of_ref, ob_ref, stf_ref, stb_ref):
    @pl.when(pl.program_id(1) == 0)
    def _():
        stf_ref[...] = jnp.zeros_like(stf_ref)
        stb_ref[...] = jnp.zeros_like(stb_ref)

    both = (0, 1)
    q_refs, k_refs, v_refs, gd_refs = (qf, qb), (kf, kb), (vf, vb), (gf, gb)
    up_refs, bias_refs, o_refs, st_refs = (upf, upb), (bf, bb), (of_ref, ob_ref), (stf_ref, stb_ref)
    rc = qf.shape[0]
    nchunk = rc // GLA_CHUNK
    ups = [_split_bf16(up_refs[d][...]) for d in both]
    z = [_dot(gd_refs[d][...], ups[d][0]) + _dot(gd_refs[d][...], ups[d][1]) + bias_refs[d][...] for d in both]
    la = [_log_sigmoid(z[d]) * (1.0 / GLA_TAU) for d in both]
    row = lax.broadcasted_iota(jnp.int32, (rc, rc), 0)
    col = lax.broadcasted_iota(jnp.int32, (rc, rc), 1)
    same = (row // GLA_CHUNK) == (col // GLA_CHUNK)
    tri = [same & (col <= row), same & (col >= row)]
    las = [_split_bf16(la[d]) for d in both]
    sums = [jnp.concatenate([tri[d].astype(BF16), same.astype(BF16)], axis=0) for d in both]
    bt = [_dot(sums[d], las[d][0]) + _dot(sums[d], las[d][1]) for d in both]
    b = [bt[d][:rc] for d in both]
    tot = [bt[d][rc:] for d in both]
    q = [q_refs[d][...].astype(F32) for d in both]
    k = [k_refs[d][...].astype(F32) for d in both]
    q_in = [(q[d] * jnp.exp(b[d]) * (GLA_DK ** -0.5)).astype(BF16) for d in both]
    k_in = [(k[d] * jnp.exp(-b[d])).astype(BF16) for d in both]
    k_st = [(k[d] * jnp.exp(tot[d] - b[d])).astype(BF16) for d in both]
    decay = [jnp.exp(tot[d]) for d in both]
    v = [v_refs[d][...].astype(BF16) for d in both]

    def head_blocks(shape, row_w, col_w):
        r = lax.broadcasted_iota(jnp.int32, shape, 0) // row_w
        c = lax.broadcasted_iota(jnp.int32, shape, 1) // col_w
        return r == c

    hc = GLA_HEADS * GLA_CHUNK
    kk_blocks = head_blocks((hc, GLA_QK_W), GLA_CHUNK, GLA_DK)
    kv_blocks = head_blocks((hc, GLA_V_W), GLA_CHUNK, GLA_DV)
    st_blocks = head_blocks((GLA_V_W, GLA_QK_W), GLA_DV, GLA_DK)
    r_i = lax.broadcasted_iota(jnp.int32, (GLA_CHUNK, hc), 0)
    j_i = lax.broadcasted_iota(jnp.int32, (GLA_CHUNK, hc), 1) % GLA_CHUNK
    mask = [j_i <= r_i, j_i > r_i]
    zero_b = jnp.zeros((), BF16)

    for step in range(nchunk):
        cs = (step, nchunk - 1 - step)
        rs = [slice(cs[d] * GLA_CHUNK, (cs[d] + 1) * GLA_CHUNK) for d in both]
        qc = [q_in[d][rs[d], :] for d in both]
        vc = [v[d][rs[d], :] for d in both]
        k_bd = [jnp.where(kk_blocks, jnp.concatenate([k_in[d][rs[d], :]] * GLA_HEADS, axis=0), zero_b)
                for d in both]
        v_bd = [jnp.where(kv_blocks, jnp.concatenate([vc[d]] * GLA_HEADS, axis=0), zero_b) for d in both]
        sc = [_dot_nt(qc[d], k_bd[d]) for d in both]
        p = [jnp.where(mask[d], sc[d], 0.0).astype(BF16) for d in both]
        st = [st_refs[d][...] for d in both]
        o_all = [_dot(p[d], v_bd[d]) + _dot_nt(qc[d], st[d].astype(BF16)) for d in both]
        inc = [jnp.where(st_blocks, _dot_tn(vc[d], k_st[d][rs[d], :]), 0.0) for d in both]
        for d in both:
            row0 = cs[d] * GLA_CHUNK
            st_refs[d][...] = decay[d][row0:row0 + 1, :] * st[d] + inc[d]
            o_refs[d][rs[d], :] = o_all[d]


def _gla(h, upf_pad, bias_f, upb_pad, bias_b, bsz, seq, rc):
    nsteps = seq // rc

    def specs(rowblk):
        return [
            pl.BlockSpec((rc, GLA_QK_W), lambda b, n: (rowblk(b, n), COL_QA // GLA_QK_W)),
            pl.BlockSpec((rc, GLA_QK_W), lambda b, n: (rowblk(b, n), COL_KA // GLA_QK_W)),
            pl.BlockSpec((rc, GLA_V_W), lambda b, n: (rowblk(b, n), COL_VA // GLA_V_W)),
            pl.BlockSpec((rc, LANES), lambda b, n: (rowblk(b, n), COL_GD // LANES)),
            pl.BlockSpec((LANES, GLA_QK_W), lambda b, n: (0, 0)),
            pl.BlockSpec((1, GLA_QK_W), lambda b, n: (0, 0)),
        ]

    fwd_blk = lambda b, n: b * nsteps + n
    bwd_blk = lambda b, n: b * nsteps + (nsteps - 1 - n)
    oshape = jax.ShapeDtypeStruct((bsz * seq, GLA_V_W), F32)
    return pl.pallas_call(
        _gla_kernel,
        grid=(bsz, nsteps),
        in_specs=specs(fwd_blk) + specs(bwd_blk),
        out_specs=[pl.BlockSpec((rc, GLA_V_W), lambda b, n: (fwd_blk(b, n), 0)),
                   pl.BlockSpec((rc, GLA_V_W), lambda b, n: (bwd_blk(b, n), 0))],
        out_shape=[oshape, oshape],
        scratch_shapes=[pltpu.VMEM((GLA_V_W, GLA_QK_W), F32)] * 2,
        compiler_params=_cparams(("parallel", "arbitrary")),
        name="gla",
    )(h, h, h, h, upf_pad, bias_f, h, h, h, h, upb_pad, bias_b)


DILATIONS = tuple(d for _, d in DIL_PATTERNS)


def _qk_prep_kernel(q_ref, k_ref, v_ref, qg_ref, kg_ref, c_ref, s_ref, *rest):
    nd = len(DILATIONS)
    outs = rest[:3 * nd]
    scr = rest[3 * nd:]
    rc = q_ref.shape[0]
    cos = c_ref[...]
    sin = s_ref[...]
    src = lax.broadcasted_iota(jnp.int32, (HEAD_DIM, HEAD_DIM), 0)
    dst = lax.broadcasted_iota(jnp.int32, (HEAD_DIM, HEAD_DIM), 1)
    hr = ROPE_DIM // 2
    swap = (jnp.where((dst < hr) & (src == dst + hr), -1.0, 0.0)
            + jnp.where((dst >= hr) & (dst < ROPE_DIM) & (src == dst - hr), 1.0, 0.0)).astype(BF16)
    ones = jnp.ones((HEAD_DIM, HEAD_DIM), BF16)

    def dot2(a, b):
        hi, lo = _split_bf16(a)
        return _dot(hi, b) + _dot(lo, b)

    def norm_rope(x, g):
        ms = dot2(x * x, ones) * (1.0 / HEAD_DIM)
        y = x * lax.rsqrt(ms + EPS) * g
        return y * cos + dot2(y, swap) * sin

    for h in range(DIL_HEADS):
        hs = slice(h * HEAD_DIM, (h + 1) * HEAD_DIM)
        scr[0][h] = norm_rope(q_ref[:, hs].astype(F32), qg_ref[...]) * (HEAD_DIM ** -0.5)
        scr[1][h] = norm_rope(k_ref[:, hs].astype(F32), kg_ref[...])
        scr[2][h] = v_ref[:, hs].astype(F32)
    for which in range(3):
        for di, d in enumerate(DILATIONS):
            o_ref = outs[which * nd + di]
            for h in range(DIL_HEADS):
                for r in range(d):
                    rows = pl.ds(r, rc // d, stride=d) if d > 1 else slice(None)
                    c0 = r * DIL_W + h * HEAD_DIM
                    o_ref[0, :, c0:c0 + HEAD_DIM] = scr[which].at[h][rows, :].astype(BF16)


def _qk_prep(h, qg, kg, cos, sin, bsz, seq, rc):
    nsteps = seq // rc
    hblk = lambda c: pl.BlockSpec((rc, DIL_W), lambda b, n: (b * nsteps + n, c // DIL_W))
    tab = pl.BlockSpec((rc, HEAD_DIM), lambda b, n: (n, 0))
    gsp = pl.BlockSpec((1, HEAD_DIM), lambda b, n: (0, 0))
    osp = [pl.BlockSpec((1, rc // d, d * DIL_W), lambda b, n: (b, n, 0)) for d in DILATIONS]
    oshape = [jax.ShapeDtypeStruct((bsz, seq // d, d * DIL_W), BF16) for d in DILATIONS]
    outs = pl.pallas_call(
        _qk_prep_kernel,
        grid=(bsz, nsteps),
        in_specs=[hblk(COL_QD), hblk(COL_KD), hblk(COL_VD), gsp, gsp, tab, tab],
        out_specs=osp * 3,
        out_shape=oshape * 3,
        scratch_shapes=[pltpu.VMEM((DIL_HEADS, rc, HEAD_DIM), F32)] * 3,
        compiler_params=_cparams(("parallel", "parallel")),
        name="qk_prep",
    )(h, h, h, qg, kg, cos, sin)
    nd = len(DILATIONS)
    return outs[:nd], outs[nd:2 * nd], outs[2 * nd:]


def _dilated_kernel(q_ref, km_ref, kl_ref, kr_ref, vm_ref, vl_ref, vr_ref, o_ref, lse_ref, *, half):
    n = pl.program_id(2)
    last = pl.num_programs(2) - 1
    tq = q_ref.shape[1]
    qb = 2 * half
    nsb = tq // qb
    r_i = lax.broadcasted_iota(jnp.int32, (qb, 2 * qb), 0)
    c_i = lax.broadcasted_iota(jnp.int32, (qb, 2 * qb), 1)
    band = (c_i >= r_i) & (c_i <= r_i + 2 * half)
    valid = []
    for sb in range(nsb):
        v = band
        if sb == 0:
            v = v & ((c_i >= half) | (n > 0))
        if sb == nsb - 1:
            v = v & ((c_i < 2 * qb - half) | (n < last))
        valid.append(v)
    lane = lax.broadcasted_iota(jnp.int32, (qb, LANES), 1)
    lse_tiles = [jnp.zeros((qb, LANES), F32) for _ in range(nsb)]
    for h in range(DIL_HEADS):
        hs = slice(h * HEAD_DIM, (h + 1) * HEAD_DIM)
        k_h = jnp.concatenate([kl_ref[0, :, hs], km_ref[0, :, hs], kr_ref[0, :, hs]], axis=0)
        v_h = jnp.concatenate([vl_ref[0, :, hs], vm_ref[0, :, hs], vr_ref[0, :, hs]], axis=0)
        for sb in range(nsb):
            q = q_ref[0, sb * qb:(sb + 1) * qb, hs]
            s = jnp.where(valid[sb], _dot_nt(q, k_h[sb * qb:sb * qb + 2 * qb]), MASK_VALUE)
            m = jnp.max(jnp.maximum(s[:, :qb], s[:, qb:]), axis=-1, keepdims=True)
            p = jnp.exp(s - m)
            den = jnp.sum(p[:, :qb] + p[:, qb:], axis=-1, keepdims=True)
            o = _dot(p.astype(BF16), v_h[sb * qb:sb * qb + 2 * qb])
            o_ref[0, sb * qb:(sb + 1) * qb, hs] = (o / den).astype(o_ref.dtype)
            lse_tiles[sb] = jnp.where(lane == h, m + jnp.log(den), lse_tiles[sb])
    for sb in range(nsb):
        lse_ref[0, sb * qb:(sb + 1) * qb, :] = lse_tiles[sb]


def _dilated(qv, kv, vv, window, dilation, tq):
    bsz, ln, _ = qv.shape
    half = window // (2 * dilation)
    nt = ln // tq
    per = tq // half
    nh = ln // half
    main = pl.BlockSpec((1, tq, DIL_W), lambda b, r, n: (b, n, r))
    left = pl.BlockSpec((1, half, DIL_W), lambda b, r, n: (b, jnp.maximum(n * per - 1, 0), r))
    right = pl.BlockSpec((1, half, DIL_W), lambda b, r, n: (b, jnp.minimum((n + 1) * per, nh - 1), r))
    return pl.pallas_call(
        functools.partial(_dilated_kernel, half=half),
        grid=(bsz, dilation, nt),
        in_specs=[main, main, left, right, main, left, right],
        out_specs=[main, pl.BlockSpec((1, tq, LANES), lambda b, r, n: (b, n, r))],
        out_shape=[jax.ShapeDtypeStruct((bsz, ln, dilation * DIL_W), BF16),
                   jax.ShapeDtypeStruct((bsz, ln, dilation * LANES), F32)],
        compiler_params=_cparams(("parallel", "parallel", "parallel")),
        name=f"dilated_d{dilation}",
    )(qv, kv, kv, kv, vv, vv, vv)


def _combine_kernel(of_ref, ob_ref, r_ref, og_ref, o1_ref, o2_ref, o3_ref, l1_ref, l2_ref, l3_ref,
                    out_ref, on_ref, ln_ref):
    for h in range(GLA_HEADS):
        vs = slice(h * GLA_DV, (h + 1) * GLA_DV)
        o = of_ref[:, vs] + ob_ref[:, vs]
        y = o * lax.rsqrt(jnp.mean(o * o, axis=-1, keepdims=True) + EPS) * og_ref[:, vs]
        rr = r_ref[:, vs].astype(F32)
        out_ref[:, vs] = (y * (rr * jax.nn.sigmoid(rr))).astype(BF16)
    rc = of_ref.shape[0]
    for p, (d, o_ref, l_ref) in enumerate(zip(DILATIONS, (o1_ref, o2_ref, o3_ref), (l1_ref, l2_ref, l3_ref))):
        for r in range(d):
            rows = pl.ds(r, rc // d, stride=d) if d > 1 else slice(None)
            ln_ref.at[p][rows, :] = l_ref[0, :, r * LANES:(r + 1) * LANES]
            for h in range(DIL_HEADS):
                c0 = r * DIL_W + h * HEAD_DIM
                on_ref.at[p * DIL_HEADS + h][rows, :] = o_ref[0, :, c0:c0 + HEAD_DIM].astype(F32)
    l1 = ln_ref[0]
    l2 = ln_ref[1]
    l3 = ln_ref[2]
    mx = jnp.maximum(l1, jnp.maximum(l2, l3))
    e1 = jnp.exp(l1 - mx)
    e2 = jnp.exp(l2 - mx)
    e3 = jnp.exp(l3 - mx)
    inv = 1.0 / (e1 + e2 + e3)
    w1 = e1 * inv
    w2 = e2 * inv
    w3 = e3 * inv
    rows = rc
    for h in range(DIL_HEADS):
        hs = slice(h * HEAD_DIM, (h + 1) * HEAD_DIM)
        bc = lambda w: jnp.broadcast_to(w[:, h:h + 1], (rows, HEAD_DIM))
        o = bc(w1) * on_ref[h] + bc(w2) * on_ref[DIL_HEADS + h] + bc(w3) * on_ref[2 * DIL_HEADS + h]
        out_ref[:, GLA_V_W + h * HEAD_DIM:GLA_V_W + (h + 1) * HEAD_DIM] = o.astype(BF16)


def _combine(o_fwd, o_bwd, h, out_g, os, ls, bsz, seq, rc):
    t = o_fwd.shape[0]
    nsteps = seq // rc
    rb = lambda w, c=0: pl.BlockSpec((rc, w), lambda b, n: (b * nsteps + n, c))
    pat = lambda w: [pl.BlockSpec((1, rc // d, d * w), lambda b, n: (b, n, 0)) for d in DILATIONS]
    return pl.pallas_call(
        _combine_kernel,
        grid=(bsz, nsteps),
        in_specs=[rb(GLA_V_W), rb(GLA_V_W), rb(GLA_V_W, COL_RA // GLA_V_W),
                  pl.BlockSpec((1, GLA_V_W), lambda b, n: (0, 0))] + pat(DIL_W) + pat(LANES),
        out_specs=rb(MIX_W),
        out_shape=jax.ShapeDtypeStruct((t, MIX_W), BF16),
        scratch_shapes=[pltpu.VMEM((len(DILATIONS) * DIL_HEADS, rc, HEAD_DIM), F32),
                        pltpu.VMEM((len(DILATIONS), rc, LANES), F32)],
        compiler_params=_cparams(("parallel", "parallel")),
        name="combine",
    )(o_fwd, o_bwd, h, out_g, *os, *ls)


def _outproj_kernel(mix_ref, w_ref, x_ref, g_ref, x2_ref, xnt_ref):
    x2 = x_ref[...] + _dot(mix_ref[...], w_ref[...])
    x2_ref[...] = x2
    xn = x2 * lax.rsqrt(jnp.mean(x2 * x2, axis=-1, keepdims=True) + EPS) * g_ref[...]
    xnt_ref[...] = xn.T.astype(BF16)


def _outproj(mixed, w_out, x, g2, tm):
    t, d = x.shape
    return pl.pallas_call(
        _outproj_kernel,
        grid=(t // tm,),
        in_specs=[
            pl.BlockSpec((tm, MIX_W), lambda i: (i, 0)),
            pl.BlockSpec((MIX_W, d), lambda i: (0, 0)),
            pl.BlockSpec((tm, d), lambda i: (i, 0)),
            pl.BlockSpec((1, d), lambda i: (0, 0)),
        ],
        out_specs=[pl.BlockSpec((tm, d), lambda i: (i, 0)), pl.BlockSpec((d, tm), lambda i: (0, i))],
        out_shape=[jax.ShapeDtypeStruct((t, d), F32), jax.ShapeDtypeStruct((d, t), BF16)],
        compiler_params=_cparams(("parallel",)),
        name="outproj",
    )(mixed, w_out, x, g2)


_NTOP = PEER_TOPK + 1
_CAND = [(a, b) for a in range(_NTOP) for b in range(_NTOP) if (a + 1) * (b + 1) <= _NTOP]
_NCAND_PAD = 64


def _merge_sort_network(n):
    pairs = []
    p = 1
    while p < n:
        k = p
        while k >= 1:
            for j in range(k % p, n - k, 2 * k):
                for i in range(min(k, n - j - k)):
                    if (i + j) // (2 * p) == (i + j + k) // (2 * p):
                        pairs.append((i + j, i + j + k))
            k //= 2
        p *= 2
    return pairs


def _top_values(sc, count):
    rows = sc.shape[0]
    ntile = rows // SUBLANES
    tiles = [sc[k * SUBLANES:(k + 1) * SUBLANES, :] for k in range(ntile)]
    for lo, hi in _merge_sort_network(ntile):
        tiles[lo], tiles[hi] = jnp.maximum(tiles[lo], tiles[hi]), jnp.minimum(tiles[lo], tiles[hi])
    vals = []
    for it in range(count):
        m = jnp.max(tiles[0], axis=0, keepdims=True)
        vals.append(m)
        hit = tiles[0] >= m
        for k in range(min(ntile, count - it - 1)):
            below = tiles[k + 1] if k + 1 < ntile else jnp.full_like(tiles[k], -jnp.inf)
            tiles[k] = jnp.where(hit, below, tiles[k])
    return vals


def _peer_route_kernel(wq_ref, xnt_ref, sk_ref, thr_ref, w1_ref, v2_ref, q_ref, cand_ref):
    q_ref[...] = _dot(wq_ref[...], xnt_ref[...])
    half = PEER_QDIM // 2
    cand_ref[...] = jnp.full(cand_ref.shape, -jnp.inf, F32)

    def head(h, carry):
        sc_a = _dot3(sk_ref[h, 0], q_ref[pl.ds(pl.multiple_of(h * PEER_QDIM, PEER_QDIM), half), :])
        sc_b = _dot3(sk_ref[h, 1], q_ref[pl.ds(pl.multiple_of(h * PEER_QDIM + half, half), half), :])
        top_a = _top_values(sc_a, _NTOP)
        top_b = _top_values(sc_b, _NTOP)
        for i, (a, b) in enumerate(_CAND):
            cand_ref[i:i + 1, :] = top_a[a] + top_b[b]
        cand = cand_ref[...]
        top_c = _top_values(cand, _NTOP)
        tau = 0.5 * (top_c[PEER_TOPK - 1] + top_c[PEER_TOPK])
        best = top_a[0] + top_b[0]
        zsum = jnp.sum(jnp.where(cand > tau, jnp.exp(cand - best), 0.0), axis=0, keepdims=True)
        thr_ref[h] = jnp.exp((tau - sc_a) - top_b[0])
        w1_ref[h] = jnp.exp(sc_a - top_a[0]) * (1.0 / zsum)
        v2_ref[h] = jnp.exp(sc_b - top_b[0])
        return carry

    lax.fori_loop(0, PEER_HEADS, head, 0)


def _peer_route(wq_t, xnt, sub_keys, tb):
    d, t = xnt.shape
    hq = wq_t.shape[0]
    key_blk = pl.BlockSpec((PEER_HEADS, PEER_NKEYS, tb), lambda i: (0, 0, i))
    key_shape = jax.ShapeDtypeStruct((PEER_HEADS, PEER_NKEYS, t), F32)
    return pl.pallas_call(
        _peer_route_kernel,
        grid=(t // tb,),
        in_specs=[
            pl.BlockSpec((hq, d), lambda i: (0, 0)),
            pl.BlockSpec((d, tb), lambda i: (0, i)),
            pl.BlockSpec(sub_keys.shape, lambda i: (0, 0, 0, 0)),
        ],
        out_specs=[key_blk, key_blk, key_blk],
        out_shape=[key_shape, key_shape, key_shape],
        scratch_shapes=[pltpu.VMEM((hq, tb), F32), pltpu.VMEM((_NCAND_PAD, tb), F32)],
        compiler_params=_cparams(("parallel",)),
        name="peer_route",
    )(wq_t, xnt, sub_keys)


def _peer_expert_kernel(xnt_ref, down_ref, upt_ref, thr_ref, w1_ref, v2_ref, x2_ref,
                        o_ref, acc_ref, a_ref, g_ref, h_ref, v2s_ref):
    e = pl.program_id(1)
    eb = a_ref.shape[0]
    tb = xnt_ref.shape[1]
    rows_i = eb // PEER_NKEYS

    @pl.when(e == 0)
    def _():
        acc_ref[...] = jnp.zeros_like(acc_ref)
        v2s_ref[:, :, :tb] = v2_ref[...]

    chunk_i = 4
    for ii in range(rows_i):
        if ii % chunk_i == 0:
            cs = slice(ii * PEER_NKEYS, (ii + chunk_i) * PEER_NKEYS)
            a_ref[cs, :tb] = _dot(down_ref[cs, :], xnt_ref[...])
        for lg in range(tb // LANES):
            ls = slice(lg * LANES, (lg + 1) * LANES)
            thr = [jnp.broadcast_to(thr_ref[h, ii:ii + 1, ls], (SUBLANES, LANES)) for h in range(PEER_HEADS)]
            w1 = [jnp.broadcast_to(w1_ref[h, ii:ii + 1, ls], (SUBLANES, LANES)) for h in range(PEER_HEADS)]
            for j0 in range(0, PEER_NKEYS, SUBLANES):
                js = slice(j0, j0 + SUBLANES)
                gate = jnp.zeros((SUBLANES, LANES), F32)
                for h in range(PEER_HEADS):
                    v2 = v2s_ref[h, js, ls]
                    gate = gate + jnp.where(v2 >= thr[h], w1[h] * v2, 0.0)
                g_ref[ii * PEER_NKEYS + j0:ii * PEER_NKEYS + j0 + SUBLANES, ls] = gate

    rows = 2 * SUBLANES
    for r0 in range(0, eb, rows):
        h_ref[r0:r0 + rows, :tb] = (g_ref[r0:r0 + rows, :tb]
                                    * jax.nn.gelu(a_ref[r0:r0 + rows, :tb])).astype(BF16)
    mrows = 512
    for m0 in range(0, acc_ref.shape[0], mrows):
        acc_ref[m0:m0 + mrows, :] += _dot(upt_ref[0, m0:m0 + mrows, :], h_ref[:, :tb])

    @pl.when(e == pl.num_programs(1) - 1)
    def _():
        o_ref[...] = x2_ref[...] + acc_ref[...].T


def _peer_experts(xnt, down, up_t, thr, w1, v2, x2, tb, eb):
    d, t = xnt.shape
    ne = down.shape[0]
    rows_i = eb // PEER_NKEYS
    row_blk = pl.BlockSpec((PEER_HEADS, rows_i, tb), lambda i, e: (0, e, i))
    key_blk = pl.BlockSpec((PEER_HEADS, PEER_NKEYS, tb), lambda i, e: (0, 0, i))
    return pl.pallas_call(
        _peer_expert_kernel,
        grid=(t // tb, ne // eb),
        in_specs=[
            pl.BlockSpec((d, tb), lambda i, e: (0, i)),
            pl.BlockSpec((eb, d), lambda i, e: (e, 0)),
            pl.BlockSpec((1, d, eb), lambda i, e: (e, 0, 0)),
            row_blk, row_blk, key_blk,
            pl.BlockSpec((tb, d), lambda i, e: (i, 0), pipeline_mode=pl.Buffered(1)),
        ],
        out_specs=pl.BlockSpec((tb, d), lambda i, e: (i, 0)),
        out_shape=jax.ShapeDtypeStruct((t, d), F32),
        scratch_shapes=[pltpu.VMEM((d, tb), F32), pltpu.VMEM((eb, tb + LANES), F32),
                        pltpu.VMEM((eb, tb + LANES), F32), pltpu.VMEM((eb, tb + LANES), BF16),
                        pltpu.VMEM((PEER_HEADS, PEER_NKEYS, tb + LANES), F32)],
        compiler_params=_cparams(("parallel", "arbitrary")),
        name="peer_experts",
    )(xnt, down, up_t, thr, w1, v2, x2)


def _reorder_in_proj(w_in):
    qa, ka, va, gdf, gdb, ra, qd, kd, vd = jnp.split(w_in, [int(c) for c in np.cumsum(IN_SPLITS)[:-1]], axis=1)
    pad = jnp.zeros((w_in.shape[0], IN_W_PAD - COL_GD - 2 * GLA_GATE_RANK), w_in.dtype)
    return jnp.concatenate([qd, kd, vd, qa, ka, va, ra, gdf, gdb, pad], axis=1).astype(BF16)


def _rope_tables(seq):
    half = ROPE_DIM // 2
    inv_freq = ROPE_THETA ** (-jnp.arange(half, dtype=F32) / half)
    ang = jnp.arange(seq, dtype=jnp.int32).astype(F32)[:, None] * inv_freq[None, :]
    cos = jnp.cos(ang)
    sin = jnp.sin(ang)
    rest = HEAD_DIM - ROPE_DIM
    c = jnp.concatenate([cos, cos, jnp.ones((seq, rest), F32)], axis=1)
    s = jnp.concatenate([sin, sin, jnp.zeros((seq, rest), F32)], axis=1)
    return c, s


def _tiles(bsz, seq):
    t = bsz * seq
    max_dil = max(DILATIONS)
    return dict(
        inproj_rows=min(1024, t),
        gla_rows=min(256, seq),
        prep_rows=min(16 * max_dil, seq),
        attn_rows=min(512, seq // max_dil),
        combine_rows=min(512, seq),
        outproj_rows=min(512, t),
        peer_tokens=min(2 * MXU_COLS, t),
        peer_experts=8 * PEER_NKEYS,
    )


def _layer(x, norm1_g, w_in, up_f, bias_f, up_b, bias_b, out_g, q_norm_g, k_norm_g, w_out, norm2_g,
           w_query, sub_keys, down, up):
    bsz, seq, d = x.shape
    t = bsz * seq
    xf = x.reshape(t, d)
    tl = _tiles(bsz, seq)

    h = _rms_matmul(xf, norm1_g.reshape(1, d), _reorder_in_proj(w_in), tm=tl["inproj_rows"], tn=IN_TN)

    zpad = jnp.zeros((LANES, GLA_QK_W), F32)
    upf_pad = zpad.at[:GLA_GATE_RANK].set(up_f)
    upb_pad = zpad.at[GLA_GATE_RANK:2 * GLA_GATE_RANK].set(up_b)
    o_fwd, o_bwd = _gla(h, upf_pad, bias_f.reshape(1, -1), upb_pad, bias_b.reshape(1, -1), bsz, seq,
                        tl["gla_rows"])

    cos, sin = _rope_tables(seq)
    qs, ks, vs = _qk_prep(h, q_norm_g.reshape(1, -1), k_norm_g.reshape(1, -1), cos, sin, bsz, seq,
                          rc=tl["prep_rows"])
    outs, lses = [], []
    for p, (window, dilation) in enumerate(DIL_PATTERNS):
        o_p, lse_p = _dilated(qs[p], ks[p], vs[p], window, dilation, tq=min(tl["attn_rows"], seq // dilation))
        outs.append(o_p)
        lses.append(lse_p)
    mixed = _combine(o_fwd, o_bwd, h, out_g.reshape(1, -1), outs, lses, bsz, seq, rc=tl["combine_rows"])

    x2, xnt = _outproj(mixed, w_out.astype(BF16), xf, norm2_g.reshape(1, d), tm=tl["outproj_rows"])

    thr, w1, v2 = _peer_route(w_query.T.astype(BF16), xnt, sub_keys, tb=tl["peer_tokens"])
    eb = tl["peer_experts"]
    up_blocks = up.reshape(-1, eb, d).transpose(0, 2, 1).astype(BF16)
    out = _peer_experts(xnt, down.astype(BF16), up_blocks, thr, w1, v2, x2, tb=tl["peer_tokens"], eb=eb)
    return out.reshape(bsz, seq, d)


def kernel(x, norm1_g, w_in, gla_up_f, gla_bias_f, gla_up_b, gla_bias_b, gla_out_g, q_norm_g, k_norm_g, w_out,
           norm2_g, peer_w_query, peer_sub_keys, peer_down, peer_up):
    for l in range(norm1_g.shape[0]):
        x = _layer(x, norm1_g[l], w_in[l], gla_up_f[l], gla_bias_f[l], gla_up_b[l], gla_bias_b[l], gla_out_g[l],
                   q_norm_g[l], k_norm_g[l], w_out[l], norm2_g[l], peer_w_query[l], peer_sub_keys[l],
                   peer_down[l], peer_up[l])
    return x
```

```python
import functools

import jax
import jax.numpy as jnp
import numpy as np
from jax import lax
from jax.experimental import pallas as pl
from jax.experimental.pallas import tpu as pltpu

F32 = jnp.float32
BF16 = jnp.bfloat16

EPS = 1e-6
HEAD_DIM = 128

GLA_HEADS = 4
GLA_DK = 64
GLA_DV = 128
GLA_GATE_RANK = 16
GLA_TAU = 16.0
GLA_CHUNK = 64
GLA_QK_W = GLA_HEADS * GLA_DK
GLA_V_W = GLA_HEADS * GLA_DV

DIL_HEADS = 12
DIL_PATTERNS = ((128, 1), (512, 4), (2048, 16))
DIL_W = DIL_HEADS * HEAD_DIM
ROPE_THETA = 500000.0
ROPE_DIM = HEAD_DIM // 4
MASK_VALUE = -1e30

PEER_HEADS = 8
PEER_NKEYS = 128
PEER_QDIM = 256
PEER_TOPK = 16

MIX_W = GLA_V_W + DIL_W
IN_SPLITS = (GLA_QK_W, GLA_QK_W, GLA_V_W, GLA_GATE_RANK, GLA_GATE_RANK, GLA_V_W, DIL_W, DIL_W, DIL_W)

LANES = 128
SUBLANES = 8
COL_QD = 0
COL_KD = COL_QD + DIL_W
COL_VD = COL_KD + DIL_W
COL_QA = COL_VD + DIL_W
COL_KA = COL_QA + GLA_QK_W
COL_VA = COL_KA + GLA_QK_W
COL_RA = COL_VA + GLA_V_W
COL_GD = COL_RA + GLA_V_W
MXU_COLS = 256
IN_TN = 5 * MXU_COLS
IN_W_PAD = -(-(COL_GD + LANES) // IN_TN) * IN_TN

VMEM_LIMIT = 56 * 1024 * 1024


def _cparams(sem):
    return pltpu.CompilerParams(dimension_semantics=sem, vmem_limit_bytes=VMEM_LIMIT)


def _split_bf16(a):
    hi = a.astype(BF16)
    lo = (a - hi.astype(F32)).astype(BF16)
    return hi, lo


def _dot(a, b):
    return jnp.dot(a, b, preferred_element_type=F32)


def _dot_nt(a, b):
    return lax.dot_general(a, b, (((1,), (1,)), ((), ())), preferred_element_type=F32)


def _dot_tn(a, b):
    return lax.dot_general(a, b, (((0,), (0,)), ((), ())), preferred_element_type=F32)


def _dot3(a, b):
    ah, al = _split_bf16(a)
    bh, bl = _split_bf16(b)
    return _dot(ah, bh) + (_dot(ah, bl) + _dot(al, bh))


def _rms_matmul_kernel(x_ref, g_ref, w_ref, o_ref, xn_ref):
    @pl.when(pl.program_id(1) == 0)
    def _():
        x = x_ref[...]
        ms = jnp.mean(x * x, axis=-1, keepdims=True)
        xn_ref[...] = (x * lax.rsqrt(ms + EPS) * g_ref[...]).astype(BF16)

    o_ref[...] = _dot(xn_ref[...], w_ref[...]).astype(o_ref.dtype)


def _rms_matmul(x, g, w, tm, tn):
    m, k = x.shape
    n = w.shape[1]
    return pl.pallas_call(
        _rms_matmul_kernel,
        grid=(m // tm, n // tn),
        in_specs=[
            pl.BlockSpec((tm, k), lambda i, j: (i, 0)),
            pl.BlockSpec((1, k), lambda i, j: (0, 0)),
            pl.BlockSpec((k, tn), lambda i, j: (0, j)),
        ],
        out_specs=pl.BlockSpec((tm, tn), lambda i, j: (i, j)),
        out_shape=jax.ShapeDtypeStruct((m, n), BF16),
        scratch_shapes=[pltpu.VMEM((tm, k), BF16)],
        compiler_params=_cparams(("parallel", "arbitrary")),
        name="rms_inproj",
    )(x, g, w)


def _log_sigmoid(z):
    return -(jnp.maximum(-z, 0.0) + jnp.log1p(jnp.exp(-jnp.abs(z))))


def _gla_kernel(qf, kf, vf, gf, upf, bf, qb, kb, vb, gb, upb, bb, of_ref, ob_ref, stf_ref, stb_ref):
    @pl.when(pl.program_id(1) == 0)
    def _():
        stf_ref[...] = jnp.zeros_like(stf_ref)
        stb_ref[...] = jnp.zeros_like(stb_ref)

    both = (0, 1)
    q_refs, k_refs, v_refs, gd_refs = (qf, qb), (kf, kb), (vf, vb), (gf, gb)
    up_refs, bias_refs, o_refs, st_refs = (upf, upb), (bf, bb), (of_ref, ob_ref), (stf_ref, stb_ref)
    rc = qf.shape[0]
    nchunk = rc // GLA_CHUNK
    ups = [_split_bf16(up_refs[d][...]) for d in both]
    z = [_dot(gd_refs[d][...], ups[d][0]) + _dot(gd_refs[d][...], ups[d][1]) + bias_refs[d][...] for d in both]
    la = [_log_sigmoid(z[d]) * (1.0 / GLA_TAU) for d in both]
    row = lax.broadcasted_iota(jnp.int32, (rc, rc), 0)
    col = lax.broadcasted_iota(jnp.int32, (rc, rc), 1)
    same = (row // GLA_CHUNK) == (col // GLA_CHUNK)
    tri = [same & (col <= row), same & (col >= row)]
    las = [_split_bf16(la[d]) for d in both]
    sums = [jnp.concatenate([tri[d].astype(BF16), same.astype(BF16)], axis=0) for d in both]
    bt = [_dot(sums[d], las[d][0]) + _dot(sums[d], las[d][1]) for d in both]
    b = [bt[d][:rc] for d in both]
    tot = [bt[d][rc:] for d in both]
    q = [q_refs[d][...].astype(F32) for d in both]
    k = [k_refs[d][...].astype(F32) for d in both]
    q_in = [(q[d] * jnp.exp(b[d]) * (GLA_DK ** -0.5)).astype(BF16) for d in both]
    k_in = [(k[d] * jnp.exp(-b[d])).astype(BF16) for d in both]
    k_st = [(k[d] * jnp.exp(tot[d] - b[d])).astype(BF16) for d in both]
    decay = [jnp.exp(tot[d]) for d in both]
    v = [v_refs[d][...].astype(BF16) for d in both]

    def head_blocks(shape, row_w, col_w):
        r = lax.broadcasted_iota(jnp.int32, shape, 0) // row_w
        c = lax.broadcasted_iota(jnp.int32, shape, 1) // col_w
        return r == c

    hc = GLA_HEADS * GLA_CHUNK
    kk_blocks = head_blocks((hc, GLA_QK_W), GLA_CHUNK, GLA_DK)
    kv_blocks = head_blocks((hc, GLA_V_W), GLA_CHUNK, GLA_DV)
    st_blocks = head_blocks((GLA_V_W, GLA_QK_W), GLA_DV, GLA_DK)
    r_i = lax.broadcasted_iota(jnp.int32, (GLA_CHUNK, hc), 0)
    j_i = lax.broadcasted_iota(jnp.int32, (GLA_CHUNK, hc), 1) % GLA_CHUNK
    mask = [j_i <= r_i, j_i > r_i]
    zero_b = jnp.zeros((), BF16)

    for step in range(nchunk):
        cs = (step, nchunk - 1 - step)
        rs = [slice(cs[d] * GLA_CHUNK, (cs[d] + 1) * GLA_CHUNK) for d in both]
        qc = [q_in[d][rs[d], :] for d in both]
        vc = [v[d][rs[d], :] for d in both]
        k_bd = [jnp.where(kk_blocks, jnp.concatenate([k_in[d][rs[d], :]] * GLA_HEADS, axis=0), zero_b)
                for d in both]
        v_bd = [jnp.where(kv_blocks, jnp.concatenate([vc[d]] * GLA_HEADS, axis=0), zero_b) for d in both]
        sc = [_dot_nt(qc[d], k_bd[d]) for d in both]
        p = [jnp.where(mask[d], sc[d], 0.0).astype(BF16) for d in both]
        st = [st_refs[d][...] for d in both]
        o_all = [_dot(p[d], v_bd[d]) + _dot_nt(qc[d], st[d].astype(BF16)) for d in both]
        inc = [jnp.where(st_blocks, _dot_tn(vc[d], k_st[d][rs[d], :]), 0.0) for d in both]
        for d in both:
            row0 = cs[d] * GLA_CHUNK
            st_refs[d][...] = decay[d][row0:row0 + 1, :] * st[d] + inc[d]
            o_refs[d][rs[d], :] = o_all[d]


def _gla(h, upf_pad, bias_f, upb_pad, bias_b, bsz, seq, rc):
    nsteps = seq // rc

    def specs(rowblk):
        return [
            pl.BlockSpec((rc, GLA_QK_W), lambda b, n: (rowblk(b, n), COL_QA // GLA_QK_W)),
            pl.BlockSpec((rc, GLA_QK_W), lambda b, n: (rowblk(b, n), COL_KA // GLA_QK_W)),
            pl.BlockSpec((rc, GLA_V_W), lambda b, n: (rowblk(b, n), COL_VA // GLA_V_W)),
            pl.BlockSpec((rc, LANES), lambda b, n: (rowblk(b, n), COL_GD // LANES)),
            pl.BlockSpec((LANES, GLA_QK_W), lambda b, n: (0, 0)),
            pl.BlockSpec((1, GLA_QK_W), lambda b, n: (0, 0)),
        ]

    fwd_blk = lambda b, n: b * nsteps + n
    bwd_blk = lambda b, n: b * nsteps + (nsteps - 1 - n)
    oshape = jax.ShapeDtypeStruct((bsz * seq, GLA_V_W), F32)
    return pl.pallas_call(
        _gla_kernel,
        grid=(bsz, nsteps),
        in_specs=specs(fwd_blk) + specs(bwd_blk),
        out_specs=[pl.BlockSpec((rc, GLA_V_W), lambda b, n: (fwd_blk(b, n), 0)),
                   pl.BlockSpec((rc, GLA_V_W), lambda b, n: (bwd_blk(b, n), 0))],
        out_shape=[oshape, oshape],
        scratch_shapes=[pltpu.VMEM((GLA_V_W, GLA_QK_W), F32)] * 2,
        compiler_params=_cparams(("parallel", "arbitrary")),
        name="gla",
    )(h, h, h, h, upf_pad, bias_f, h, h, h, h, upb_pad, bias_b)


DILATIONS = tuple(d for _, d in DIL_PATTERNS)


def _qk_prep_kernel(q_ref, k_ref, v_ref, qg_ref, kg_ref, c_ref, s_ref, *rest):
    nd = len(DILATIONS)
    outs = rest[:3 * nd]
    scr = rest[3 * nd:]
    rc = q_ref.shape[0]
    cos = c_ref[...]
    sin = s_ref[...]
    src = lax.broadcasted_iota(jnp.int32, (HEAD_DIM, HEAD_DIM), 0)
    dst = lax.broadcasted_iota(jnp.int32, (HEAD_DIM, HEAD_DIM), 1)
    hr = ROPE_DIM // 2
    swap = (jnp.where((dst < hr) & (src == dst + hr), -1.0, 0.0)
            + jnp.where((dst >= hr) & (dst < ROPE_DIM) & (src == dst - hr), 1.0, 0.0)).astype(BF16)
    ones = jnp.ones((HEAD_DIM, HEAD_DIM), BF16)

    def dot2(a, b):
        hi, lo = _split_bf16(a)
        return _dot(hi, b) + _dot(lo, b)

    def norm_rope(x, g):
        ms = dot2(x * x, ones) * (1.0 / HEAD_DIM)
        y = x * lax.rsqrt(ms + EPS) * g
        return y * cos + dot2(y, swap) * sin

    for h in range(DIL_HEADS):
        hs = slice(h * HEAD_DIM, (h + 1) * HEAD_DIM)
        scr[0][h] = norm_rope(q_ref[:, hs].astype(F32), qg_ref[...]) * (HEAD_DIM ** -0.5)
        scr[1][h] = norm_rope(k_ref[:, hs].astype(F32), kg_ref[...])
        scr[2][h] = v_ref[:, hs].astype(F32)
    for which in range(3):
        for di, d in enumerate(DILATIONS):
            o_ref = outs[which * nd + di]
            for h in range(DIL_HEADS):
                for r in range(d):
                    rows = pl.ds(r, rc // d, stride=d) if d > 1 else slice(None)
                    c0 = r * DIL_W + h * HEAD_DIM
                    o_ref[0, :, c0:c0 + HEAD_DIM] = scr[which].at[h][rows, :].astype(BF16)


def _qk_prep(h, qg, kg, cos, sin, bsz, seq, rc):
    nsteps = seq // rc
    hblk = lambda c: pl.BlockSpec((rc, DIL_W), lambda b, n: (b * nsteps + n, c // DIL_W))
    tab = pl.BlockSpec((rc, HEAD_DIM), lambda b, n: (n, 0))
    gsp = pl.BlockSpec((1, HEAD_DIM), lambda b, n: (0, 0))
    osp = [pl.BlockSpec((1, rc // d, d * DIL_W), lambda b, n: (b, n, 0)) for d in DILATIONS]
    oshape = [jax.ShapeDtypeStruct((bsz, seq // d, d * DIL_W), BF16) for d in DILATIONS]
    outs = pl.pallas_call(
        _qk_prep_kernel,
        grid=(bsz, nsteps),
        in_specs=[hblk(COL_QD), hblk(COL_KD), hblk(COL_VD), gsp, gsp, tab, tab],
        out_specs=osp * 3,
        out_shape=oshape * 3,
        scratch_shapes=[pltpu.VMEM((DIL_HEADS, rc, HEAD_DIM), F32)] * 3,
        compiler_params=_cparams(("parallel", "parallel")),
        name="qk_prep",
    )(h, h, h, qg, kg, cos, sin)
    nd = len(DILATIONS)
    return outs[:nd], outs[nd:2 * nd], outs[2 * nd:]


def _dilated_kernel(q_ref, km_ref, kl_ref, kr_ref, vm_ref, vl_ref, vr_ref, o_ref, lse_ref, *, half):
    n = pl.program_id(2)
    last = pl.num_programs(2) - 1
    tq = q_ref.shape[1]
    qb = 2 * half
    nsb = tq // qb
    r_i = lax.broadcasted_iota(jnp.int32, (qb, 2 * qb), 0)
    c_i = lax.broadcasted_iota(jnp.int32, (qb, 2 * qb), 1)
    band = (c_i >= r_i) & (c_i <= r_i + 2 * half)
    valid = []
    for sb in range(nsb):
        v = band
        if sb == 0:
            v = v & ((c_i >= half) | (n > 0))
        if sb == nsb - 1:
            v = v & ((c_i < 2 * qb - half) | (n < last))
        valid.append(v)
    lane = lax.broadcasted_iota(jnp.int32, (qb, LANES), 1)
    lse_tiles = [jnp.zeros((qb, LANES), F32) for _ in range(nsb)]
    for h in range(DIL_HEADS):
        hs = slice(h * HEAD_DIM, (h + 1) * HEAD_DIM)
        k_h = jnp.concatenate([kl_ref[0, :, hs], km_ref[0, :, hs], kr_ref[0, :, hs]], axis=0)
        v_h = jnp.concatenate([vl_ref[0, :, hs], vm_ref[0, :, hs], vr_ref[0, :, hs]], axis=0)
        for sb in range(nsb):
            q = q_ref[0, sb * qb:(sb + 1) * qb, hs]
            s = jnp.where(valid[sb], _dot_nt(q, k_h[sb * qb:sb * qb + 2 * qb]), MASK_VALUE)
            m = jnp.max(jnp.maximum(s[:, :qb], s[:, qb:]), axis=-1, keepdims=True)
            p = jnp.exp(s - m)
            den = jnp.sum(p[:, :qb] + p[:, qb:], axis=-1, keepdims=True)
            o = _dot(p.astype(BF16), v_h[sb * qb:sb * qb + 2 * qb])
            o_ref[0, sb * qb:(sb + 1) * qb, hs] = (o / den).astype(o_ref.dtype)
            lse_tiles[sb] = jnp.where(lane == h, m + jnp.log(den), lse_tiles[sb])
    for sb in range(nsb):
        lse_ref[0, sb * qb:(sb + 1) * qb, :] = lse_tiles[sb]


def _dilated(qv, kv, vv, window, dilation, tq):
    bsz, ln, _ = qv.shape
    half = window // (2 * dilation)
    nt = ln // tq
    per = tq // half
    nh = ln // half
    main = pl.BlockSpec((1, tq, DIL_W), lambda b, r, n: (b, n, r))
    left = pl.BlockSpec((1, half, DIL_W), lambda b, r, n: (b, jnp.maximum(n * per - 1, 0), r))
    right = pl.BlockSpec((1, half, DIL_W), lambda b, r, n: (b, jnp.minimum((n + 1) * per, nh - 1), r))
    return pl.pallas_call(
        functools.partial(_dilated_kernel, half=half),
        grid=(bsz, dilation, nt),
        in_specs=[main, main, left, right, main, left, right],
        out_specs=[main, pl.BlockSpec((1, tq, LANES), lambda b, r, n: (b, n, r))],
        out_shape=[jax.ShapeDtypeStruct((bsz, ln, dilation * DIL_W), BF16),
                   jax.ShapeDtypeStruct((bsz, ln, dilation * LANES), F32)],
        compiler_params=_cparams(("parallel", "parallel", "parallel")),
        name=f"dilated_d{dilation}",
    )(qv, kv, kv, kv, vv, vv, vv)


def _combine_kernel(of_ref, ob_ref, r_ref, og_ref, o1_ref, o2_ref, o3_ref, l1_ref, l2_ref, l3_ref,
                    out_ref, on_ref, ln_ref):
    for h in range(GLA_HEADS):
        vs = slice(h * GLA_DV, (h + 1) * GLA_DV)
        o = of_ref[:, vs] + ob_ref[:, vs]
        y = o * lax.rsqrt(jnp.mean(o * o, axis=-1, keepdims=True) + EPS) * og_ref[:, vs]
        rr = r_ref[:, vs].astype(F32)
        out_ref[:, vs] = (y * (rr * jax.nn.sigmoid(rr))).astype(BF16)
    rc = of_ref.shape[0]
    for p, (d, o_ref, l_ref) in enumerate(zip(DILATIONS, (o1_ref, o2_ref, o3_ref), (l1_ref, l2_ref, l3_ref))):
        for r in range(d):
            rows = pl.ds(r, rc // d, stride=d) if d > 1 else slice(None)
            ln_ref.at[p][rows, :] = l_ref[0, :, r * LANES:(r + 1) * LANES]
            for h in range(DIL_HEADS):
                c0 = r * DIL_W + h * HEAD_DIM
                on_ref.at[p * DIL_HEADS + h][rows, :] = o_ref[0, :, c0:c0 + HEAD_DIM].astype(F32)
    l1 = ln_ref[0]
    l2 = ln_ref[1]
    l3 = ln_ref[2]
    mx = jnp.maximum(l1, jnp.maximum(l2, l3))
    e1 = jnp.exp(l1 - mx)
    e2 = jnp.exp(l2 - mx)
    e3 = jnp.exp(l3 - mx)
    inv = 1.0 / (e1 + e2 + e3)
    w1 = e1 * inv
    w2 = e2 * inv
    w3 = e3 * inv
    rows = rc
    for h in range(DIL_HEADS):
        hs = slice(h * HEAD_DIM, (h + 1) * HEAD_DIM)
        bc = lambda w: jnp.broadcast_to(w[:, h:h + 1], (rows, HEAD_DIM))
        o = bc(w1) * on_ref[h] + bc(w2) * on_ref[DIL_HEADS + h] + bc(w3) * on_ref[2 * DIL_HEADS + h]
        out_ref[:, GLA_V_W + h * HEAD_DIM:GLA_V_W + (h + 1) * HEAD_DIM] = o.astype(BF16)


def _combine(o_fwd, o_bwd, h, out_g, os, ls, bsz, seq, rc):
    t = o_fwd.shape[0]
    nsteps = seq // rc
    rb = lambda w, c=0: pl.BlockSpec((rc, w), lambda b, n: (b * nsteps + n, c))
    pat = lambda w: [pl.BlockSpec((1, rc // d, d * w), lambda b, n: (b, n, 0)) for d in DILATIONS]
    return pl.pallas_call(
        _combine_kernel,
        grid=(bsz, nsteps),
        in_specs=[rb(GLA_V_W), rb(GLA_V_W), rb(GLA_V_W, COL_RA // GLA_V_W),
                  pl.BlockSpec((1, GLA_V_W), lambda b, n: (0, 0))] + pat(DIL_W) + pat(LANES),
        out_specs=rb(MIX_W),
        out_shape=jax.ShapeDtypeStruct((t, MIX_W), BF16),
        scratch_shapes=[pltpu.VMEM((len(DILATIONS) * DIL_HEADS, rc, HEAD_DIM), F32),
                        pltpu.VMEM((len(DILATIONS), rc, LANES), F32)],
        compiler_params=_cparams(("parallel", "parallel")),
        name="combine",
    )(o_fwd, o_bwd, h, out_g, *os, *ls)


def _outproj_kernel(mix_ref, w_ref, x_ref, g_ref, x2_ref, xnt_ref):
    x2 = x_ref[...] + _dot(mix_ref[...], w_ref[...])
    x2_ref[...] = x2
    xn = x2 * lax.rsqrt(jnp.mean(x2 * x2, axis=-1, keepdims=True) + EPS) * g_ref[...]
    xnt_ref[...] = xn.T.astype(BF16)


def _outproj(mixed, w_out, x, g2, tm):
    t, d = x.shape
    return pl.pallas_call(
        _outproj_kernel,
        grid=(t // tm,),
        in_specs=[
            pl.BlockSpec((tm, MIX_W), lambda i: (i, 0)),
            pl.BlockSpec((MIX_W, d), lambda i: (0, 0)),
            pl.BlockSpec((tm, d), lambda i: (i, 0)),
            pl.BlockSpec((1, d), lambda i: (0, 0)),
        ],
        out_specs=[pl.BlockSpec((tm, d), lambda i: (i, 0)), pl.BlockSpec((d, tm), lambda i: (0, i))],
        out_shape=[jax.ShapeDtypeStruct((t, d), F32), jax.ShapeDtypeStruct((d, t), BF16)],
        compiler_params=_cparams(("parallel",)),
        name="outproj",
    )(mixed, w_out, x, g2)


_NTOP = PEER_TOPK + 1
_CAND = [(a, b) for a in range(_NTOP) for b in range(_NTOP) if (a + 1) * (b + 1) <= _NTOP]
_NCAND_PAD = 64


def _merge_sort_network(n):
    pairs = []
    p = 1
    while p < n:
        k = p
        while k >= 1:
            for j in range(k % p, n - k, 2 * k):
                for i in range(min(k, n - j - k)):
                    if (i + j) // (2 * p) == (i + j + k) // (2 * p):
                        pairs.append((i + j, i + j + k))
            k //= 2
        p *= 2
    return pairs


def _top_values(sc, count):
    rows = sc.shape[0]
    ntile = rows // SUBLANES
    tiles = [sc[k * SUBLANES:(k + 1) * SUBLANES, :] for k in range(ntile)]
    for lo, hi in _merge_sort_network(ntile):
        tiles[lo], tiles[hi] = jnp.maximum(tiles[lo], tiles[hi]), jnp.minimum(tiles[lo], tiles[hi])
    vals = []
    for it in range(count):
        m = jnp.max(tiles[0], axis=0, keepdims=True)
        vals.append(m)
        hit = tiles[0] >= m
        for k in range(min(ntile, count - it - 1)):
            below = tiles[k + 1] if k + 1 < ntile else jnp.full_like(tiles[k], -jnp.inf)
            tiles[k] = jnp.where(hit, below, tiles[k])
    return vals


def _peer_route_kernel(wq_ref, xnt_ref, sk_ref, thr_ref, w1_ref, v2_ref, q_ref, cand_ref):
    q_ref[...] = _dot(wq_ref[...], xnt_ref[...])
    half = PEER_QDIM // 2
    cand_ref[...] = jnp.full(cand_ref.shape, -jnp.inf, F32)

    def head(h, carry):
        sc_a = _dot3(sk_ref[h, 0], q_ref[pl.ds(pl.multiple_of(h * PEER_QDIM, PEER_QDIM), half), :])
        sc_b = _dot3(sk_ref[h, 1], q_ref[pl.ds(pl.multiple_of(h * PEER_QDIM + half, half), half), :])
        top_a = _top_values(sc_a, _NTOP)
        top_b = _top_values(sc_b, _NTOP)
        for i, (a, b) in enumerate(_CAND):
            cand_ref[i:i + 1, :] = top_a[a] + top_b[b]
        cand = cand_ref[...]
        top_c = _top_values(cand, _NTOP)
        tau = 0.5 * (top_c[PEER_TOPK - 1] + top_c[PEER_TOPK])
        best = top_a[0] + top_b[0]
        zsum = jnp.sum(jnp.where(cand > tau, jnp.exp(cand - best), 0.0), axis=0, keepdims=True)
        thr_ref[h] = jnp.exp((tau - sc_a) - top_b[0])
        w1_ref[h] = jnp.exp(sc_a - top_a[0]) * (1.0 / zsum)
        v2_ref[h] = jnp.exp(sc_b - top_b[0])
        return carry

    lax.fori_loop(0, PEER_HEADS, head, 0)


def _peer_route(wq_t, xnt, sub_keys, tb):
    d, t = xnt.shape
    hq = wq_t.shape[0]
    key_blk = pl.BlockSpec((PEER_HEADS, PEER_NKEYS, tb), lambda i: (0, 0, i))
    key_shape = jax.ShapeDtypeStruct((PEER_HEADS, PEER_NKEYS, t), F32)
    return pl.pallas_call(
        _peer_route_kernel,
        grid=(t // tb,),
        in_specs=[
            pl.BlockSpec((hq, d), lambda i: (0, 0)),
            pl.BlockSpec((d, tb), lambda i: (0, i)),
            pl.BlockSpec(sub_keys.shape, lambda i: (0, 0, 0, 0)),
        ],
        out_specs=[key_blk, key_blk, key_blk],
        out_shape=[key_shape, key_shape, key_shape],
        scratch_shapes=[pltpu.VMEM((hq, tb), F32), pltpu.VMEM((_NCAND_PAD, tb), F32)],
        compiler_params=_cparams(("parallel",)),
        name="peer_route",
    )(wq_t, xnt, sub_keys)


def _peer_expert_kernel(xnt_ref, down_ref, upt_ref, thr_ref, w1_ref, v2_ref, x2_ref,
                        o_ref, acc_ref, a_ref, g_ref, h_ref):
    e = pl.program_id(1)
    eb, tb = a_ref.shape
    rows_i = eb // PEER_NKEYS

    @pl.when(e == 0)
    def _():
        acc_ref[...] = jnp.zeros_like(acc_ref)

    chunk_i = 4
    for ii in range(rows_i):
        if ii % chunk_i == 0:
            cs = slice(ii * PEER_NKEYS, (ii + chunk_i) * PEER_NKEYS)
            a_ref[cs, :] = _dot(down_ref[cs, :], xnt_ref[...])
        for lg in range(tb // LANES):
            ls = slice(lg * LANES, (lg + 1) * LANES)
            thr = [jnp.broadcast_to(thr_ref[h, ii:ii + 1, ls], (SUBLANES, LANES)) for h in range(PEER_HEADS)]
            w1 = [jnp.broadcast_to(w1_ref[h, ii:ii + 1, ls], (SUBLANES, LANES)) for h in range(PEER_HEADS)]
            for j0 in range(0, PEER_NKEYS, SUBLANES):
                js = slice(j0, j0 + SUBLANES)
                gate = jnp.zeros((SUBLANES, LANES), F32)
                for h in range(PEER_HEADS):
                    v2 = v2_ref[h, js, ls]
                    gate = gate + jnp.where(v2 >= thr[h], w1[h] * v2, 0.0)
                g_ref[ii * PEER_NKEYS + j0:ii * PEER_NKEYS + j0 + SUBLANES, ls] = gate

    rows = 2 * SUBLANES
    for r0 in range(0, eb, rows):
        h_ref[r0:r0 + rows, :] = (g_ref[r0:r0 + rows, :] * jax.nn.gelu(a_ref[r0:r0 + rows, :])).astype(BF16)
    mrows = 1024
    for m0 in range(0, acc_ref.shape[0], mrows):
        acc_ref[m0:m0 + mrows, :] += _dot(upt_ref[0, m0:m0 + mrows, :], h_ref[...])

    @pl.when(e == pl.num_programs(1) - 1)
    def _():
        o_ref[...] = x2_ref[...] + acc_ref[...].T


def _peer_experts(xnt, down, up_t, thr, w1, v2, x2, tb, eb):
    d, t = xnt.shape
    ne = down.shape[0]
    rows_i = eb // PEER_NKEYS
    row_blk = pl.BlockSpec((PEER_HEADS, rows_i, tb), lambda i, e: (0, e, i))
    key_blk = pl.BlockSpec((PEER_HEADS, PEER_NKEYS, tb), lambda i, e: (0, 0, i))
    return pl.pallas_call(
        _peer_expert_kernel,
        grid=(t // tb, ne // eb),
        in_specs=[
            pl.BlockSpec((d, tb), lambda i, e: (0, i)),
            pl.BlockSpec((eb, d), lambda i, e: (e, 0)),
            pl.BlockSpec((1, d, eb), lambda i, e: (e, 0, 0)),
            row_blk, row_blk, key_blk,
            pl.BlockSpec((tb, d), lambda i, e: (i, 0), pipeline_mode=pl.Buffered(1)),
        ],
        out_specs=pl.BlockSpec((tb, d), lambda i, e: (i, 0)),
        out_shape=jax.ShapeDtypeStruct((t, d), F32),
        scratch_shapes=[pltpu.VMEM((d, tb), F32), pltpu.VMEM((eb, tb), F32), pltpu.VMEM((eb, tb), F32),
                        pltpu.VMEM((eb, tb), BF16)],
        compiler_params=_cparams(("parallel", "arbitrary")),
        name="peer_experts",
    )(xnt, down, up_t, thr, w1, v2, x2)


def _reorder_in_proj(w_in):
    qa, ka, va, gdf, gdb, ra, qd, kd, vd = jnp.split(w_in, [int(c) for c in np.cumsum(IN_SPLITS)[:-1]], axis=1)
    pad = jnp.zeros((w_in.shape[0], IN_W_PAD - COL_GD - 2 * GLA_GATE_RANK), w_in.dtype)
    return jnp.concatenate([qd, kd, vd, qa, ka, va, ra, gdf, gdb, pad], axis=1).astype(BF16)


def _rope_tables(seq):
    half = ROPE_DIM // 2
    inv_freq = ROPE_THETA ** (-jnp.arange(half, dtype=F32) / half)
    ang = jnp.arange(seq, dtype=jnp.int32).astype(F32)[:, None] * inv_freq[None, :]
    cos = jnp.cos(ang)
    sin = jnp.sin(ang)
    rest = HEAD_DIM - ROPE_DIM
    c = jnp.concatenate([cos, cos, jnp.ones((seq, rest), F32)], axis=1)
    s = jnp.concatenate([sin, sin, jnp.zeros((seq, rest), F32)], axis=1)
    return c, s


def _tiles(bsz, seq):
    t = bsz * seq
    max_dil = max(DILATIONS)
    return dict(
        inproj_rows=min(1024, t),
        gla_rows=min(256, seq),
        prep_rows=min(16 * max_dil, seq),
        attn_rows=min(512, seq // max_dil),
        combine_rows=min(512, seq),
        outproj_rows=min(512, t),
        peer_tokens=min(2 * MXU_COLS, t),
        peer_experts=8 * PEER_NKEYS,
    )


def _layer(x, norm1_g, w_in, up_f, bias_f, up_b, bias_b, out_g, q_norm_g, k_norm_g, w_out, norm2_g,
           w_query, sub_keys, down, up):
    bsz, seq, d = x.shape
    t = bsz * seq
    xf = x.reshape(t, d)
    tl = _tiles(bsz, seq)

    h = _rms_matmul(xf, norm1_g.reshape(1, d), _reorder_in_proj(w_in), tm=tl["inproj_rows"], tn=IN_TN)

    zpad = jnp.zeros((LANES, GLA_QK_W), F32)
    upf_pad = zpad.at[:GLA_GATE_RANK].set(up_f)
    upb_pad = zpad.at[GLA_GATE_RANK:2 * GLA_GATE_RANK].set(up_b)
    o_fwd, o_bwd = _gla(h, upf_pad, bias_f.reshape(1, -1), upb_pad, bias_b.reshape(1, -1), bsz, seq,
                        tl["gla_rows"])

    cos, sin = _rope_tables(seq)
    qs, ks, vs = _qk_prep(h, q_norm_g.reshape(1, -1), k_norm_g.reshape(1, -1), cos, sin, bsz, seq,
                          rc=tl["prep_rows"])
    outs, lses = [], []
    for p, (window, dilation) in enumerate(DIL_PATTERNS):
        o_p, lse_p = _dilated(qs[p], ks[p], vs[p], window, dilation, tq=min(tl["attn_rows"], seq // dilation))
        outs.append(o_p)
        lses.append(lse_p)
    mixed = _combine(o_fwd, o_bwd, h, out_g.reshape(1, -1), outs, lses, bsz, seq, rc=tl["combine_rows"])

    x2, xnt = _outproj(mixed, w_out.astype(BF16), xf, norm2_g.reshape(1, d), tm=tl["outproj_rows"])

    thr, w1, v2 = _peer_route(w_query.T.astype(BF16), xnt, sub_keys, tb=tl["peer_tokens"])
    eb = tl["peer_experts"]
    up_blocks = up.reshape(-1, eb, d).transpose(0, 2, 1).astype(BF16)
    out = _peer_experts(xnt, down.astype(BF16), up_blocks, thr, w1, v2, x2, tb=tl["peer_tokens"], eb=eb)
    return out.reshape(bsz, seq, d)


def kernel(x, norm1_g, w_in, gla_up_f, gla_bias_f, gla_up_b, gla_bias_b, gla_out_g, q_norm_g, k_norm_g, w_out,
           norm2_g, peer_w_query, peer_sub_keys, peer_down, peer_up):
    for l in range(norm1_g.shape[0]):
        x = _layer(x, norm1_g[l], w_in[l], gla_up_f[l], gla_bias_f[l], gla_up_b[l], gla_bias_b[l], gla_out_g[l],
                   q_norm_g[l], k_norm_g[l], w_out[l], norm2_g[l], peer_w_query[l], peer_sub_keys[l],
                   peer_down[l], peer_up[l])
    return x
```

```python
import functools

import jax
import jax.numpy as jnp
import numpy as np
from jax import lax
from jax.experimental import pallas as pl
from jax.experimental.pallas import tpu as pltpu

F32 = jnp.float32
BF16 = jnp.bfloat16

EPS = 1e-6
HEAD_DIM = 128

GLA_HEADS = 4
GLA_DK = 64
GLA_DV = 128
GLA_GATE_RANK = 16
GLA_TAU = 16.0
GLA_CHUNK = 64
GLA_QK_W = GLA_HEADS * GLA_DK
GLA_V_W = GLA_HEADS * GLA_DV

DIL_HEADS = 12
DIL_PATTERNS = ((128, 1), (512, 4), (2048, 16))
DIL_W = DIL_HEADS * HEAD_DIM
ROPE_THETA = 500000.0
ROPE_DIM = HEAD_DIM // 4
MASK_VALUE = -1e30

PEER_HEADS = 8
PEER_NKEYS = 128
PEER_QDIM = 256
PEER_TOPK = 16

MIX_W = GLA_V_W + DIL_W
IN_SPLITS = (GLA_QK_W, GLA_QK_W, GLA_V_W, GLA_GATE_RANK, GLA_GATE_RANK, GLA_V_W, DIL_W, DIL_W, DIL_W)

LANES = 128
SUBLANES = 8
COL_QD = 0
COL_KD = COL_QD + DIL_W
COL_VD = COL_KD + DIL_W
COL_QA = COL_VD + DIL_W
COL_KA = COL_QA + GLA_QK_W
COL_VA = COL_KA + GLA_QK_W
COL_RA = COL_VA + GLA_V_W
COL_GD = COL_RA + GLA_V_W
MXU_COLS = 256
IN_TN = 5 * MXU_COLS
IN_W_PAD = -(-(COL_GD + LANES) // IN_TN) * IN_TN

VMEM_LIMIT = 56 * 1024 * 1024


def _cparams(sem):
    return pltpu.CompilerParams(dimension_semantics=sem, vmem_limit_bytes=VMEM_LIMIT)


def _split_bf16(a):
    hi = a.astype(BF16)
    lo = (a - hi.astype(F32)).astype(BF16)
    return hi, lo


def _dot(a, b):
    return jnp.dot(a, b, preferred_element_type=F32)


def _dot_nt(a, b):
    return lax.dot_general(a, b, (((1,), (1,)), ((), ())), preferred_element_type=F32)


def _dot_tn(a, b):
    return lax.dot_general(a, b, (((0,), (0,)), ((), ())), preferred_element_type=F32)


def _dot3(a, b):
    ah, al = _split_bf16(a)
    bh, bl = _split_bf16(b)
    return _dot(ah, bh) + (_dot(ah, bl) + _dot(al, bh))


def _rms_matmul_kernel(x_ref, g_ref, w_ref, o_ref, xn_ref):
    @pl.when(pl.program_id(1) == 0)
    def _():
        x = x_ref[...]
        ms = jnp.mean(x * x, axis=-1, keepdims=True)
        xn_ref[...] = (x * lax.rsqrt(ms + EPS) * g_ref[...]).astype(BF16)

    o_ref[...] = _dot(xn_ref[...], w_ref[...]).astype(o_ref.dtype)


def _rms_matmul(x, g, w, tm, tn):
    m, k = x.shape
    n = w.shape[1]
    return pl.pallas_call(
        _rms_matmul_kernel,
        grid=(m // tm, n // tn),
        in_specs=[
            pl.BlockSpec((tm, k), lambda i, j: (i, 0)),
            pl.BlockSpec((1, k), lambda i, j: (0, 0)),
            pl.BlockSpec((k, tn), lambda i, j: (0, j)),
        ],
        out_specs=pl.BlockSpec((tm, tn), lambda i, j: (i, j)),
        out_shape=jax.ShapeDtypeStruct((m, n), BF16),
        scratch_shapes=[pltpu.VMEM((tm, k), BF16)],
        compiler_params=_cparams(("parallel", "arbitrary")),
        name="rms_inproj",
    )(x, g, w)


def _log_sigmoid(z):
    return -(jnp.maximum(-z, 0.0) + jnp.log1p(jnp.exp(-jnp.abs(z))))


def _gla_kernel(qf, kf, vf, gf, upf, bf, qb, kb, vb, gb, upb, bb, of_ref, ob_ref, stf_ref, stb_ref):
    @pl.when(pl.program_id(1) == 0)
    def _():
        stf_ref[...] = jnp.zeros_like(stf_ref)
        stb_ref[...] = jnp.zeros_like(stb_ref)

    both = (0, 1)
    q_refs, k_refs, v_refs, gd_refs = (qf, qb), (kf, kb), (vf, vb), (gf, gb)
    up_refs, bias_refs, o_refs, st_refs = (upf, upb), (bf, bb), (of_ref, ob_ref), (stf_ref, stb_ref)
    rc = qf.shape[0]
    nchunk = rc // GLA_CHUNK
    ups = [_split_bf16(up_refs[d][...]) for d in both]
    z = [_dot(gd_refs[d][...], ups[d][0]) + _dot(gd_refs[d][...], ups[d][1]) + bias_refs[d][...] for d in both]
    la = [_log_sigmoid(z[d]) * (1.0 / GLA_TAU) for d in both]
    row = lax.broadcasted_iota(jnp.int32, (rc, rc), 0)
    col = lax.broadcasted_iota(jnp.int32, (rc, rc), 1)
    same = (row // GLA_CHUNK) == (col // GLA_CHUNK)
    tri = [same & (col <= row), same & (col >= row)]
    las = [_split_bf16(la[d]) for d in both]
    sums = [jnp.concatenate([tri[d].astype(BF16), same.astype(BF16)], axis=0) for d in both]
    bt = [_dot(sums[d], las[d][0]) + _dot(sums[d], las[d][1]) for d in both]
    b = [bt[d][:rc] for d in both]
    tot = [bt[d][rc:] for d in both]
    q = [q_refs[d][...].astype(F32) for d in both]
    k = [k_refs[d][...].astype(F32) for d in both]
    q_in = [(q[d] * jnp.exp(b[d]) * (GLA_DK ** -0.5)).astype(BF16) for d in both]
    k_in = [(k[d] * jnp.exp(-b[d])).astype(BF16) for d in both]
    k_st = [(k[d] * jnp.exp(tot[d] - b[d])).astype(BF16) for d in both]
    decay = [jnp.exp(tot[d]) for d in both]
    v = [v_refs[d][...].astype(BF16) for d in both]

    def head_blocks(shape, row_w, col_w):
        r = lax.broadcasted_iota(jnp.int32, shape, 0) // row_w
        c = lax.broadcasted_iota(jnp.int32, shape, 1) // col_w
        return r == c

    hc = GLA_HEADS * GLA_CHUNK
    kk_blocks = head_blocks((hc, GLA_QK_W), GLA_CHUNK, GLA_DK)
    kv_blocks = head_blocks((hc, GLA_V_W), GLA_CHUNK, GLA_DV)
    st_blocks = head_blocks((GLA_V_W, GLA_QK_W), GLA_DV, GLA_DK)
    r_i = lax.broadcasted_iota(jnp.int32, (GLA_CHUNK, hc), 0)
    j_i = lax.broadcasted_iota(jnp.int32, (GLA_CHUNK, hc), 1) % GLA_CHUNK
    mask = [j_i <= r_i, j_i > r_i]
    zero_b = jnp.zeros((), BF16)

    for step in range(nchunk):
        cs = (step, nchunk - 1 - step)
        rs = [slice(cs[d] * GLA_CHUNK, (cs[d] + 1) * GLA_CHUNK) for d in both]
        qc = [q_in[d][rs[d], :] for d in both]
        vc = [v[d][rs[d], :] for d in both]
        k_bd = [jnp.where(kk_blocks, jnp.concatenate([k_in[d][rs[d], :]] * GLA_HEADS, axis=0), zero_b)
                for d in both]
        v_bd = [jnp.where(kv_blocks, jnp.concatenate([vc[d]] * GLA_HEADS, axis=0), zero_b) for d in both]
        sc = [_dot_nt(qc[d], k_bd[d]) for d in both]
        p = [jnp.where(mask[d], sc[d], 0.0).astype(BF16) for d in both]
        st = [st_refs[d][...] for d in both]
        o_all = [_dot(p[d], v_bd[d]) + _dot_nt(qc[d], st[d].astype(BF16)) for d in both]
        inc = [jnp.where(st_blocks, _dot_tn(vc[d], k_st[d][rs[d], :]), 0.0) for d in both]
        for d in both:
            row0 = cs[d] * GLA_CHUNK
            st_refs[d][...] = decay[d][row0:row0 + 1, :] * st[d] + inc[d]
            o_refs[d][rs[d], :] = o_all[d]


def _gla(h, upf_pad, bias_f, upb_pad, bias_b, bsz, seq, rc):
    nsteps = seq // rc

    def specs(rowblk):
        return [
            pl.BlockSpec((rc, GLA_QK_W), lambda b, n: (rowblk(b, n), COL_QA // GLA_QK_W)),
            pl.BlockSpec((rc, GLA_QK_W), lambda b, n: (rowblk(b, n), COL_KA // GLA_QK_W)),
            pl.BlockSpec((rc, GLA_V_W), lambda b, n: (rowblk(b, n), COL_VA // GLA_V_W)),
            pl.BlockSpec((rc, LANES), lambda b, n: (rowblk(b, n), COL_GD // LANES)),
            pl.BlockSpec((LANES, GLA_QK_W), lambda b, n: (0, 0)),
            pl.BlockSpec((1, GLA_QK_W), lambda b, n: (0, 0)),
        ]

    fwd_blk = lambda b, n: b * nsteps + n
    bwd_blk = lambda b, n: b * nsteps + (nsteps - 1 - n)
    oshape = jax.ShapeDtypeStruct((bsz * seq, GLA_V_W), F32)
    return pl.pallas_call(
        _gla_kernel,
        grid=(bsz, nsteps),
        in_specs=specs(fwd_blk) + specs(bwd_blk),
        out_specs=[pl.BlockSpec((rc, GLA_V_W), lambda b, n: (fwd_blk(b, n), 0)),
                   pl.BlockSpec((rc, GLA_V_W), lambda b, n: (bwd_blk(b, n), 0))],
        out_shape=[oshape, oshape],
        scratch_shapes=[pltpu.VMEM((GLA_V_W, GLA_QK_W), F32)] * 2,
        compiler_params=_cparams(("parallel", "arbitrary")),
        name="gla",
    )(h, h, h, h, upf_pad, bias_f, h, h, h, h, upb_pad, bias_b)


DILATIONS = tuple(d for _, d in DIL_PATTERNS)


def _qk_prep_kernel(q_ref, k_ref, v_ref, qg_ref, kg_ref, c_ref, s_ref, *rest):
    nd = len(DILATIONS)
    outs = rest[:3 * nd]
    scr = rest[3 * nd:]
    rc = q_ref.shape[0]
    cos = c_ref[...]
    sin = s_ref[...]
    src = lax.broadcasted_iota(jnp.int32, (HEAD_DIM, HEAD_DIM), 0)
    dst = lax.broadcasted_iota(jnp.int32, (HEAD_DIM, HEAD_DIM), 1)
    hr = ROPE_DIM // 2
    swap = (jnp.where((dst < hr) & (src == dst + hr), -1.0, 0.0)
            + jnp.where((dst >= hr) & (dst < ROPE_DIM) & (src == dst - hr), 1.0, 0.0)).astype(BF16)
    ones = jnp.ones((HEAD_DIM, HEAD_DIM), BF16)

    def dot2(a, b):
        hi, lo = _split_bf16(a)
        return _dot(hi, b) + _dot(lo, b)

    def norm_rope(x, g):
        ms = dot2(x * x, ones) * (1.0 / HEAD_DIM)
        y = x * lax.rsqrt(ms + EPS) * g
        return y * cos + dot2(y, swap) * sin

    for h in range(DIL_HEADS):
        hs = slice(h * HEAD_DIM, (h + 1) * HEAD_DIM)
        scr[0][h] = norm_rope(q_ref[:, hs].astype(F32), qg_ref[...]) * (HEAD_DIM ** -0.5)
        scr[1][h] = norm_rope(k_ref[:, hs].astype(F32), kg_ref[...])
        scr[2][h] = v_ref[:, hs].astype(F32)
    for which in range(3):
        for di, d in enumerate(DILATIONS):
            o_ref = outs[which * nd + di]
            for h in range(DIL_HEADS):
                for r in range(d):
                    rows = pl.ds(r, rc // d, stride=d) if d > 1 else slice(None)
                    c0 = r * DIL_W + h * HEAD_DIM
                    o_ref[0, :, c0:c0 + HEAD_DIM] = scr[which].at[h][rows, :].astype(BF16)


def _qk_prep(h, qg, kg, cos, sin, bsz, seq, rc):
    nsteps = seq // rc
    hblk = lambda c: pl.BlockSpec((rc, DIL_W), lambda b, n: (b * nsteps + n, c // DIL_W))
    tab = pl.BlockSpec((rc, HEAD_DIM), lambda b, n: (n, 0))
    gsp = pl.BlockSpec((1, HEAD_DIM), lambda b, n: (0, 0))
    osp = [pl.BlockSpec((1, rc // d, d * DIL_W), lambda b, n: (b, n, 0)) for d in DILATIONS]
    oshape = [jax.ShapeDtypeStruct((bsz, seq // d, d * DIL_W), BF16) for d in DILATIONS]
    outs = pl.pallas_call(
        _qk_prep_kernel,
        grid=(bsz, nsteps),
        in_specs=[hblk(COL_QD), hblk(COL_KD), hblk(COL_VD), gsp, gsp, tab, tab],
        out_specs=osp * 3,
        out_shape=oshape * 3,
        scratch_shapes=[pltpu.VMEM((DIL_HEADS, rc, HEAD_DIM), F32)] * 3,
        compiler_params=_cparams(("parallel", "parallel")),
        name="qk_prep",
    )(h, h, h, qg, kg, cos, sin)
    nd = len(DILATIONS)
    return outs[:nd], outs[nd:2 * nd], outs[2 * nd:]


def _dilated_kernel(q_ref, km_ref, kl_ref, kr_ref, vm_ref, vl_ref, vr_ref, o_ref, lse_ref, *, half):
    n = pl.program_id(2)
    last = pl.num_programs(2) - 1
    tq = q_ref.shape[1]
    qb = 2 * half
    nsb = tq // qb
    r_i = lax.broadcasted_iota(jnp.int32, (qb, 2 * qb), 0)
    c_i = lax.broadcasted_iota(jnp.int32, (qb, 2 * qb), 1)
    band = (c_i >= r_i) & (c_i <= r_i + 2 * half)
    valid = []
    for sb in range(nsb):
        v = band
        if sb == 0:
            v = v & ((c_i >= half) | (n > 0))
        if sb == nsb - 1:
            v = v & ((c_i < 2 * qb - half) | (n < last))
        valid.append(v)
    lane = lax.broadcasted_iota(jnp.int32, (qb, LANES), 1)
    lse_tiles = [jnp.zeros((qb, LANES), F32) for _ in range(nsb)]
    for h in range(DIL_HEADS):
        hs = slice(h * HEAD_DIM, (h + 1) * HEAD_DIM)
        k_h = jnp.concatenate([kl_ref[0, :, hs], km_ref[0, :, hs], kr_ref[0, :, hs]], axis=0)
        v_h = jnp.concatenate([vl_ref[0, :, hs], vm_ref[0, :, hs], vr_ref[0, :, hs]], axis=0)
        for sb in range(nsb):
            q = q_ref[0, sb * qb:(sb + 1) * qb, hs]
            s = jnp.where(valid[sb], _dot_nt(q, k_h[sb * qb:sb * qb + 2 * qb]), MASK_VALUE)
            m = jnp.max(jnp.maximum(s[:, :qb], s[:, qb:]), axis=-1, keepdims=True)
            p = jnp.exp(s - m)
            den = jnp.sum(p[:, :qb] + p[:, qb:], axis=-1, keepdims=True)
            o = _dot(p.astype(BF16), v_h[sb * qb:sb * qb + 2 * qb])
            o_ref[0, sb * qb:(sb + 1) * qb, hs] = (o / den).astype(o_ref.dtype)
            lse_tiles[sb] = jnp.where(lane == h, m + jnp.log(den), lse_tiles[sb])
    for sb in range(nsb):
        lse_ref[0, sb * qb:(sb + 1) * qb, :] = lse_tiles[sb]


def _dilated(qv, kv, vv, window, dilation, tq):
    bsz, ln, _ = qv.shape
    half = window // (2 * dilation)
    nt = ln // tq
    per = tq // half
    nh = ln // half
    main = pl.BlockSpec((1, tq, DIL_W), lambda b, r, n: (b, n, r))
    left = pl.BlockSpec((1, half, DIL_W), lambda b, r, n: (b, jnp.maximum(n * per - 1, 0), r))
    right = pl.BlockSpec((1, half, DIL_W), lambda b, r, n: (b, jnp.minimum((n + 1) * per, nh - 1), r))
    return pl.pallas_call(
        functools.partial(_dilated_kernel, half=half),
        grid=(bsz, dilation, nt),
        in_specs=[main, main, left, right, main, left, right],
        out_specs=[main, pl.BlockSpec((1, tq, LANES), lambda b, r, n: (b, n, r))],
        out_shape=[jax.ShapeDtypeStruct((bsz, ln, dilation * DIL_W), BF16),
                   jax.ShapeDtypeStruct((bsz, ln, dilation * LANES), F32)],
        compiler_params=_cparams(("parallel", "parallel", "parallel")),
        name=f"dilated_d{dilation}",
    )(qv, kv, kv, kv, vv, vv, vv)


def _combine_kernel(of_ref, ob_ref, r_ref, og_ref, o1_ref, o2_ref, o3_ref, l1_ref, l2_ref, l3_ref,
                    out_ref, on_ref, ln_ref):
    for h in range(GLA_HEADS):
        vs = slice(h * GLA_DV, (h + 1) * GLA_DV)
        o = of_ref[:, vs] + ob_ref[:, vs]
        y = o * lax.rsqrt(jnp.mean(o * o, axis=-1, keepdims=True) + EPS) * og_ref[:, vs]
        rr = r_ref[:, vs].astype(F32)
        out_ref[:, vs] = (y * (rr * jax.nn.sigmoid(rr))).astype(BF16)
    rc = of_ref.shape[0]
    for p, (d, o_ref, l_ref) in enumerate(zip(DILATIONS, (o1_ref, o2_ref, o3_ref), (l1_ref, l2_ref, l3_ref))):
        for r in range(d):
            rows = pl.ds(r, rc // d, stride=d) if d > 1 else slice(None)
            ln_ref.at[p][rows, :] = l_ref[0, :, r * LANES:(r + 1) * LANES]
            for h in range(DIL_HEADS):
                c0 = r * DIL_W + h * HEAD_DIM
                on_ref.at[p * DIL_HEADS + h][rows, :] = o_ref[0, :, c0:c0 + HEAD_DIM].astype(F32)
    l1 = ln_ref[0]
    l2 = ln_ref[1]
    l3 = ln_ref[2]
    mx = jnp.maximum(l1, jnp.maximum(l2, l3))
    e1 = jnp.exp(l1 - mx)
    e2 = jnp.exp(l2 - mx)
    e3 = jnp.exp(l3 - mx)
    inv = 1.0 / (e1 + e2 + e3)
    w1 = e1 * inv
    w2 = e2 * inv
    w3 = e3 * inv
    rows = rc
    for h in range(DIL_HEADS):
        hs = slice(h * HEAD_DIM, (h + 1) * HEAD_DIM)
        bc = lambda w: jnp.broadcast_to(w[:, h:h + 1], (rows, HEAD_DIM))
        o = bc(w1) * on_ref[h] + bc(w2) * on_ref[DIL_HEADS + h] + bc(w3) * on_ref[2 * DIL_HEADS + h]
        out_ref[:, GLA_V_W + h * HEAD_DIM:GLA_V_W + (h + 1) * HEAD_DIM] = o.astype(BF16)


def _combine(o_fwd, o_bwd, h, out_g, os, ls, bsz, seq, rc):
    t = o_fwd.shape[0]
    nsteps = seq // rc
    rb = lambda w, c=0: pl.BlockSpec((rc, w), lambda b, n: (b * nsteps + n, c))
    pat = lambda w: [pl.BlockSpec((1, rc // d, d * w), lambda b, n: (b, n, 0)) for d in DILATIONS]
    return pl.pallas_call(
        _combine_kernel,
        grid=(bsz, nsteps),
        in_specs=[rb(GLA_V_W), rb(GLA_V_W), rb(GLA_V_W, COL_RA // GLA_V_W),
                  pl.BlockSpec((1, GLA_V_W), lambda b, n: (0, 0))] + pat(DIL_W) + pat(LANES),
        out_specs=rb(MIX_W),
        out_shape=jax.ShapeDtypeStruct((t, MIX_W), BF16),
        scratch_shapes=[pltpu.VMEM((len(DILATIONS) * DIL_HEADS, rc, HEAD_DIM), F32),
                        pltpu.VMEM((len(DILATIONS), rc, LANES), F32)],
        compiler_params=_cparams(("parallel", "parallel")),
        name="combine",
    )(o_fwd, o_bwd, h, out_g, *os, *ls)


def _outproj_kernel(mix_ref, w_ref, x_ref, g_ref, x2_ref, xnt_ref):
    x2 = x_ref[...] + _dot(mix_ref[...], w_ref[...])
    x2_ref[...] = x2
    xn = x2 * lax.rsqrt(jnp.mean(x2 * x2, axis=-1, keepdims=True) + EPS) * g_ref[...]
    xnt_ref[...] = xn.T.astype(BF16)


def _outproj(mixed, w_out, x, g2, tm):
    t, d = x.shape
    return pl.pallas_call(
        _outproj_kernel,
        grid=(t // tm,),
        in_specs=[
            pl.BlockSpec((tm, MIX_W), lambda i: (i, 0)),
            pl.BlockSpec((MIX_W, d), lambda i: (0, 0)),
            pl.BlockSpec((tm, d), lambda i: (i, 0)),
            pl.BlockSpec((1, d), lambda i: (0, 0)),
        ],
        out_specs=[pl.BlockSpec((tm, d), lambda i: (i, 0)), pl.BlockSpec((d, tm), lambda i: (0, i))],
        out_shape=[jax.ShapeDtypeStruct((t, d), F32), jax.ShapeDtypeStruct((d, t), BF16)],
        compiler_params=_cparams(("parallel",)),
        name="outproj",
    )(mixed, w_out, x, g2)


_NTOP = PEER_TOPK + 1
_CAND = [(a, b) for a in range(_NTOP) for b in range(_NTOP) if (a + 1) * (b + 1) <= _NTOP]
_NCAND_PAD = 64


def _merge_sort_network(n):
    pairs = []
    p = 1
    while p < n:
        k = p
        while k >= 1:
            for j in range(k % p, n - k, 2 * k):
                for i in range(min(k, n - j - k)):
                    if (i + j) // (2 * p) == (i + j + k) // (2 * p):
                        pairs.append((i + j, i + j + k))
            k //= 2
        p *= 2
    return pairs


def _top_values(sc, count):
    rows = sc.shape[0]
    ntile = rows // SUBLANES
    tiles = [sc[k * SUBLANES:(k + 1) * SUBLANES, :] for k in range(ntile)]
    for lo, hi in _merge_sort_network(ntile):
        tiles[lo], tiles[hi] = jnp.maximum(tiles[lo], tiles[hi]), jnp.minimum(tiles[lo], tiles[hi])
    vals = []
    for it in range(count):
        m = jnp.max(tiles[0], axis=0, keepdims=True)
        vals.append(m)
        hit = tiles[0] >= m
        for k in range(min(ntile, count - it - 1)):
            below = tiles[k + 1] if k + 1 < ntile else jnp.full_like(tiles[k], -jnp.inf)
            tiles[k] = jnp.where(hit, below, tiles[k])
    return vals


def _peer_route_kernel(wq_ref, xnt_ref, sk_ref, thr_ref, w1_ref, v2_ref, q_ref, cand_ref):
    q_ref[...] = _dot(wq_ref[...], xnt_ref[...])
    half = PEER_QDIM // 2
    cand_ref[...] = jnp.full(cand_ref.shape, -jnp.inf, F32)

    def head(h, carry):
        sc_a = _dot3(sk_ref[h, 0], q_ref[pl.ds(pl.multiple_of(h * PEER_QDIM, PEER_QDIM), half), :])
        sc_b = _dot3(sk_ref[h, 1], q_ref[pl.ds(pl.multiple_of(h * PEER_QDIM + half, half), half), :])
        top_a = _top_values(sc_a, _NTOP)
        top_b = _top_values(sc_b, _NTOP)
        for i, (a, b) in enumerate(_CAND):
            cand_ref[i:i + 1, :] = top_a[a] + top_b[b]
        cand = cand_ref[...]
        top_c = _top_values(cand, _NTOP)
        tau = 0.5 * (top_c[PEER_TOPK - 1] + top_c[PEER_TOPK])
        best = top_a[0] + top_b[0]
        zsum = jnp.sum(jnp.where(cand > tau, jnp.exp(cand - best), 0.0), axis=0, keepdims=True)
        thr_ref[h] = jnp.exp((tau - sc_a) - top_b[0])
        w1_ref[h] = jnp.exp(sc_a - top_a[0]) * (1.0 / zsum)
        v2_ref[h] = jnp.exp(sc_b - top_b[0])
        return carry

    lax.fori_loop(0, PEER_HEADS, head, 0)


def _peer_route(wq_t, xnt, sub_keys, tb):
    d, t = xnt.shape
    hq = wq_t.shape[0]
    key_blk = pl.BlockSpec((PEER_HEADS, PEER_NKEYS, tb), lambda i: (0, 0, i))
    key_shape = jax.ShapeDtypeStruct((PEER_HEADS, PEER_NKEYS, t), F32)
    return pl.pallas_call(
        _peer_route_kernel,
        grid=(t // tb,),
        in_specs=[
            pl.BlockSpec((hq, d), lambda i: (0, 0)),
            pl.BlockSpec((d, tb), lambda i: (0, i)),
            pl.BlockSpec(sub_keys.shape, lambda i: (0, 0, 0, 0)),
        ],
        out_specs=[key_blk, key_blk, key_blk],
        out_shape=[key_shape, key_shape, key_shape],
        scratch_shapes=[pltpu.VMEM((hq, tb), F32), pltpu.VMEM((_NCAND_PAD, tb), F32)],
        compiler_params=_cparams(("parallel",)),
        name="peer_route",
    )(wq_t, xnt, sub_keys)


def _peer_expert_kernel(xnt_ref, down_ref, upt_ref, thr_ref, w1_ref, v2_ref, x2_ref,
                        o_ref, acc_ref, a_ref, g_ref, h_ref):
    e = pl.program_id(1)
    eb, tb = a_ref.shape
    rows_i = eb // PEER_NKEYS

    @pl.when(e == 0)
    def _():
        acc_ref[...] = jnp.zeros_like(acc_ref)

    chunk_i = 4
    for ii in range(rows_i):
        if ii % chunk_i == 0:
            cs = slice(ii * PEER_NKEYS, (ii + chunk_i) * PEER_NKEYS)
            a_ref[cs, :] = _dot(down_ref[cs, :], xnt_ref[...])
        for lg in range(tb // LANES):
            ls = slice(lg * LANES, (lg + 1) * LANES)
            thr = [jnp.broadcast_to(thr_ref[h, ii:ii + 1, ls], (SUBLANES, LANES)) for h in range(PEER_HEADS)]
            w1 = [jnp.broadcast_to(w1_ref[h, ii:ii + 1, ls], (SUBLANES, LANES)) for h in range(PEER_HEADS)]
            for j0 in range(0, PEER_NKEYS, SUBLANES):
                js = slice(j0, j0 + SUBLANES)
                gate = jnp.zeros((SUBLANES, LANES), F32)
                for h in range(PEER_HEADS):
                    v2 = v2_ref[h, js, ls]
                    gate = gate + jnp.where(v2 >= thr[h], w1[h] * v2, 0.0)
                g_ref[ii * PEER_NKEYS + j0:ii * PEER_NKEYS + j0 + SUBLANES, ls] = gate

    rows = 2 * SUBLANES
    for r0 in range(0, eb, rows):
        h_ref[r0:r0 + rows, :] = (g_ref[r0:r0 + rows, :] * jax.nn.gelu(a_ref[r0:r0 + rows, :])).astype(BF16)
    mrows = 1024
    for m0 in range(0, acc_ref.shape[0], mrows):
        acc_ref[m0:m0 + mrows, :] += _dot(upt_ref[0, m0:m0 + mrows, :], h_ref[...])

    @pl.when(e == pl.num_programs(1) - 1)
    def _():
        o_ref[...] = x2_ref[...] + acc_ref[...].T


def _peer_experts(xnt, down, up_t, thr, w1, v2, x2, tb, eb):
    d, t = xnt.shape
    ne = down.shape[0]
    rows_i = eb // PEER_NKEYS
    row_blk = pl.BlockSpec((PEER_HEADS, rows_i, tb), lambda i, e: (0, e, i))
    key_blk = pl.BlockSpec((PEER_HEADS, PEER_NKEYS, tb), lambda i, e: (0, 0, i))
    return pl.pallas_call(
        _peer_expert_kernel,
        grid=(t // tb, ne // eb),
        in_specs=[
            pl.BlockSpec((d, tb), lambda i, e: (0, i)),
            pl.BlockSpec((eb, d), lambda i, e: (e, 0)),
            pl.BlockSpec((1, d, eb), lambda i, e: (e, 0, 0)),
            row_blk, row_blk, key_blk,
            pl.BlockSpec((tb, d), lambda i, e: (i, 0)),
        ],
        out_specs=pl.BlockSpec((tb, d), lambda i, e: (i, 0)),
        out_shape=jax.ShapeDtypeStruct((t, d), F32),
        scratch_shapes=[pltpu.VMEM((d, tb), F32), pltpu.VMEM((eb, tb), F32), pltpu.VMEM((eb, tb), F32),
                        pltpu.VMEM((eb, tb), BF16)],
        compiler_params=_cparams(("parallel", "arbitrary")),
        name="peer_experts",
    )(xnt, down, up_t, thr, w1, v2, x2)


def _reorder_in_proj(w_in):
    qa, ka, va, gdf, gdb, ra, qd, kd, vd = jnp.split(w_in, [int(c) for c in np.cumsum(IN_SPLITS)[:-1]], axis=1)
    pad = jnp.zeros((w_in.shape[0], IN_W_PAD - COL_GD - 2 * GLA_GATE_RANK), w_in.dtype)
    return jnp.concatenate([qd, kd, vd, qa, ka, va, ra, gdf, gdb, pad], axis=1).astype(BF16)


def _rope_tables(seq):
    half = ROPE_DIM // 2
    inv_freq = ROPE_THETA ** (-jnp.arange(half, dtype=F32) / half)
    ang = jnp.arange(seq, dtype=jnp.int32).astype(F32)[:, None] * inv_freq[None, :]
    cos = jnp.cos(ang)
    sin = jnp.sin(ang)
    rest = HEAD_DIM - ROPE_DIM
    c = jnp.concatenate([cos, cos, jnp.ones((seq, rest), F32)], axis=1)
    s = jnp.concatenate([sin, sin, jnp.zeros((seq, rest), F32)], axis=1)
    return c, s


def _tiles(bsz, seq):
    t = bsz * seq
    max_dil = max(DILATIONS)
    return dict(
        inproj_rows=min(1024, t),
        gla_rows=min(256, seq),
        prep_rows=min(16 * max_dil, seq),
        attn_rows=min(512, seq // max_dil),
        combine_rows=min(512, seq),
        outproj_rows=min(512, t),
        peer_tokens=min(2 * MXU_COLS, t),
        peer_experts=8 * PEER_NKEYS,
    )


def _layer(x, norm1_g, w_in, up_f, bias_f, up_b, bias_b, out_g, q_norm_g, k_norm_g, w_out, norm2_g,
           w_query, sub_keys, down, up):
    bsz, seq, d = x.shape
    t = bsz * seq
    xf = x.reshape(t, d)
    tl = _tiles(bsz, seq)

    h = _rms_matmul(xf, norm1_g.reshape(1, d), _reorder_in_proj(w_in), tm=tl["inproj_rows"], tn=IN_TN)

    zpad = jnp.zeros((LANES, GLA_QK_W), F32)
    upf_pad = zpad.at[:GLA_GATE_RANK].set(up_f)
    upb_pad = zpad.at[GLA_GATE_RANK:2 * GLA_GATE_RANK].set(up_b)
    o_fwd, o_bwd = _gla(h, upf_pad, bias_f.reshape(1, -1), upb_pad, bias_b.reshape(1, -1), bsz, seq,
                        tl["gla_rows"])

    cos, sin = _rope_tables(seq)
    qs, ks, vs = _qk_prep(h, q_norm_g.reshape(1, -1), k_norm_g.reshape(1, -1), cos, sin, bsz, seq,
                          rc=tl["prep_rows"])
    outs, lses = [], []
    for p, (window, dilation) in enumerate(DIL_PATTERNS):
        o_p, lse_p = _dilated(qs[p], ks[p], vs[p], window, dilation, tq=min(tl["attn_rows"], seq // dilation))
        outs.append(o_p)
        lses.append(lse_p)
    mixed = _combine(o_fwd, o_bwd, h, out_g.reshape(1, -1), outs, lses, bsz, seq, rc=tl["combine_rows"])

    x2, xnt = _outproj(mixed, w_out.astype(BF16), xf, norm2_g.reshape(1, d), tm=tl["outproj_rows"])

    thr, w1, v2 = _peer_route(w_query.T.astype(BF16), xnt, sub_keys, tb=tl["peer_tokens"])
    eb = tl["peer_experts"]
    up_blocks = up.reshape(-1, eb, d).transpose(0, 2, 1).astype(BF16)
    out = _peer_experts(xnt, down.astype(BF16), up_blocks, thr, w1, v2, x2, tb=tl["peer_tokens"], eb=eb)
    return out.reshape(bsz, seq, d)


def kernel(x, norm1_g, w_in, gla_up_f, gla_bias_f, gla_up_b, gla_bias_b, gla_out_g, q_norm_g, k_norm_g, w_out,
           norm2_g, peer_w_query, peer_sub_keys, peer_down, peer_up):
    for l in range(norm1_g.shape[0]):
        x = _layer(x, norm1_g[l], w_in[l], gla_up_f[l], gla_bias_f[l], gla_up_b[l], gla_bias_b[l], gla_out_g[l],
                   q_norm_g[l], k_norm_g[l], w_out[l], norm2_g[l], peer_w_query[l], peer_sub_keys[l],
                   peer_down[l], peer_up[l])
    return x
```

```python
import functools

import jax
import jax.numpy as jnp
import numpy as np
from jax import lax
from jax.experimental import pallas as pl
from jax.experimental.pallas import tpu as pltpu

F32 = jnp.float32
BF16 = jnp.bfloat16

EPS = 1e-6
HEAD_DIM = 128

GLA_HEADS = 4
GLA_DK = 64
GLA_DV = 128
GLA_GATE_RANK = 16
GLA_TAU = 16.0
GLA_CHUNK = 64
GLA_QK_W = GLA_HEADS * GLA_DK
GLA_V_W = GLA_HEADS * GLA_DV

DIL_HEADS = 12
DIL_PATTERNS = ((128, 1), (512, 4), (2048, 16))
DIL_W = DIL_HEADS * HEAD_DIM
ROPE_THETA = 500000.0
ROPE_DIM = HEAD_DIM // 4
MASK_VALUE = -1e30

PEER_HEADS = 8
PEER_NKEYS = 128
PEER_QDIM = 256
PEER_TOPK = 16

MIX_W = GLA_V_W + DIL_W
IN_SPLITS = (GLA_QK_W, GLA_QK_W, GLA_V_W, GLA_GATE_RANK, GLA_GATE_RANK, GLA_V_W, DIL_W, DIL_W, DIL_W)

LANES = 128
SUBLANES = 8
COL_QD = 0
COL_KD = COL_QD + DIL_W
COL_VD = COL_KD + DIL_W
COL_QA = COL_VD + DIL_W
COL_KA = COL_QA + GLA_QK_W
COL_VA = COL_KA + GLA_QK_W
COL_RA = COL_VA + GLA_V_W
COL_GD = COL_RA + GLA_V_W
MXU_COLS = 256
IN_TN = 5 * MXU_COLS
IN_W_PAD = -(-(COL_GD + LANES) // IN_TN) * IN_TN

VMEM_LIMIT = 56 * 1024 * 1024


def _cparams(sem):
    return pltpu.CompilerParams(dimension_semantics=sem, vmem_limit_bytes=VMEM_LIMIT)


def _split_bf16(a):
    hi = a.astype(BF16)
    lo = (a - hi.astype(F32)).astype(BF16)
    return hi, lo


def _dot(a, b):
    return jnp.dot(a, b, preferred_element_type=F32)


def _dot_nt(a, b):
    return lax.dot_general(a, b, (((1,), (1,)), ((), ())), preferred_element_type=F32)


def _dot_tn(a, b):
    return lax.dot_general(a, b, (((0,), (0,)), ((), ())), preferred_element_type=F32)


def _dot3(a, b):
    ah, al = _split_bf16(a)
    bh, bl = _split_bf16(b)
    return _dot(ah, bh) + (_dot(ah, bl) + _dot(al, bh))


def _rms_matmul_kernel(x_ref, g_ref, w_ref, o_ref, xn_ref):
    @pl.when(pl.program_id(1) == 0)
    def _():
        x = x_ref[...]
        ms = jnp.mean(x * x, axis=-1, keepdims=True)
        xn_ref[...] = (x * lax.rsqrt(ms + EPS) * g_ref[...]).astype(BF16)

    o_ref[...] = _dot(xn_ref[...], w_ref[...]).astype(o_ref.dtype)


def _rms_matmul(x, g, w, tm, tn):
    m, k = x.shape
    n = w.shape[1]
    return pl.pallas_call(
        _rms_matmul_kernel,
        grid=(m // tm, n // tn),
        in_specs=[
            pl.BlockSpec((tm, k), lambda i, j: (i, 0)),
            pl.BlockSpec((1, k), lambda i, j: (0, 0)),
            pl.BlockSpec((k, tn), lambda i, j: (0, j)),
        ],
        out_specs=pl.BlockSpec((tm, tn), lambda i, j: (i, j)),
        out_shape=jax.ShapeDtypeStruct((m, n), BF16),
        scratch_shapes=[pltpu.VMEM((tm, k), BF16)],
        compiler_params=_cparams(("parallel", "arbitrary")),
        name="rms_inproj",
    )(x, g, w)


def _log_sigmoid(z):
    return -(jnp.maximum(-z, 0.0) + jnp.log1p(jnp.exp(-jnp.abs(z))))


def _gla_kernel(qf, kf, vf, gf, upf, bf, qb, kb, vb, gb, upb, bb, of_ref, ob_ref, stf_ref, stb_ref):
    @pl.when(pl.program_id(1) == 0)
    def _():
        stf_ref[...] = jnp.zeros_like(stf_ref)
        stb_ref[...] = jnp.zeros_like(stb_ref)

    both = (0, 1)
    q_refs, k_refs, v_refs, gd_refs = (qf, qb), (kf, kb), (vf, vb), (gf, gb)
    up_refs, bias_refs, o_refs, st_refs = (upf, upb), (bf, bb), (of_ref, ob_ref), (stf_ref, stb_ref)
    rc = qf.shape[0]
    nchunk = rc // GLA_CHUNK
    ups = [_split_bf16(up_refs[d][...]) for d in both]
    z = [_dot(gd_refs[d][...], ups[d][0]) + _dot(gd_refs[d][...], ups[d][1]) + bias_refs[d][...] for d in both]
    la = [_log_sigmoid(z[d]) * (1.0 / GLA_TAU) for d in both]
    row = lax.broadcasted_iota(jnp.int32, (rc, rc), 0)
    col = lax.broadcasted_iota(jnp.int32, (rc, rc), 1)
    same = (row // GLA_CHUNK) == (col // GLA_CHUNK)
    tri = [same & (col <= row), same & (col >= row)]
    las = [_split_bf16(la[d]) for d in both]
    sums = [jnp.concatenate([tri[d].astype(BF16), same.astype(BF16)], axis=0) for d in both]
    bt = [_dot(sums[d], las[d][0]) + _dot(sums[d], las[d][1]) for d in both]
    b = [bt[d][:rc] for d in both]
    tot = [bt[d][rc:] for d in both]
    q = [q_refs[d][...].astype(F32) for d in both]
    k = [k_refs[d][...].astype(F32) for d in both]
    q_in = [(q[d] * jnp.exp(b[d]) * (GLA_DK ** -0.5)).astype(BF16) for d in both]
    k_in = [(k[d] * jnp.exp(-b[d])).astype(BF16) for d in both]
    k_st = [(k[d] * jnp.exp(tot[d] - b[d])).astype(BF16) for d in both]
    decay = [jnp.exp(tot[d]) for d in both]
    v = [v_refs[d][...].astype(BF16) for d in both]

    def head_blocks(shape, row_w, col_w):
        r = lax.broadcasted_iota(jnp.int32, shape, 0) // row_w
        c = lax.broadcasted_iota(jnp.int32, shape, 1) // col_w
        return r == c

    hc = GLA_HEADS * GLA_CHUNK
    kk_blocks = head_blocks((hc, GLA_QK_W), GLA_CHUNK, GLA_DK)
    kv_blocks = head_blocks((hc, GLA_V_W), GLA_CHUNK, GLA_DV)
    st_blocks = head_blocks((GLA_V_W, GLA_QK_W), GLA_DV, GLA_DK)
    r_i = lax.broadcasted_iota(jnp.int32, (GLA_CHUNK, hc), 0)
    j_i = lax.broadcasted_iota(jnp.int32, (GLA_CHUNK, hc), 1) % GLA_CHUNK
    mask = [j_i <= r_i, j_i > r_i]
    zero_b = jnp.zeros((), BF16)

    for step in range(nchunk):
        cs = (step, nchunk - 1 - step)
        rs = [slice(cs[d] * GLA_CHUNK, (cs[d] + 1) * GLA_CHUNK) for d in both]
        qc = [q_in[d][rs[d], :] for d in both]
        vc = [v[d][rs[d], :] for d in both]
        k_bd = [jnp.where(kk_blocks, jnp.concatenate([k_in[d][rs[d], :]] * GLA_HEADS, axis=0), zero_b)
                for d in both]
        v_bd = [jnp.where(kv_blocks, jnp.concatenate([vc[d]] * GLA_HEADS, axis=0), zero_b) for d in both]
        sc = [_dot_nt(qc[d], k_bd[d]) for d in both]
        p = [jnp.where(mask[d], sc[d], 0.0).astype(BF16) for d in both]
        st = [st_refs[d][...] for d in both]
        o_all = [_dot(p[d], v_bd[d]) + _dot_nt(qc[d], st[d].astype(BF16)) for d in both]
        inc = [jnp.where(st_blocks, _dot_tn(vc[d], k_st[d][rs[d], :]), 0.0) for d in both]
        for d in both:
            row0 = cs[d] * GLA_CHUNK
            st_refs[d][...] = decay[d][row0:row0 + 1, :] * st[d] + inc[d]
            o_refs[d][rs[d], :] = o_all[d]


def _gla(h, upf_pad, bias_f, upb_pad, bias_b, bsz, seq, rc):
    nsteps = seq // rc

    def specs(rowblk):
        return [
            pl.BlockSpec((rc, GLA_QK_W), lambda b, n: (rowblk(b, n), COL_QA // GLA_QK_W)),
            pl.BlockSpec((rc, GLA_QK_W), lambda b, n: (rowblk(b, n), COL_KA // GLA_QK_W)),
            pl.BlockSpec((rc, GLA_V_W), lambda b, n: (rowblk(b, n), COL_VA // GLA_V_W)),
            pl.BlockSpec((rc, LANES), lambda b, n: (rowblk(b, n), COL_GD // LANES)),
            pl.BlockSpec((LANES, GLA_QK_W), lambda b, n: (0, 0)),
            pl.BlockSpec((1, GLA_QK_W), lambda b, n: (0, 0)),
        ]

    fwd_blk = lambda b, n: b * nsteps + n
    bwd_blk = lambda b, n: b * nsteps + (nsteps - 1 - n)
    oshape = jax.ShapeDtypeStruct((bsz * seq, GLA_V_W), F32)
    return pl.pallas_call(
        _gla_kernel,
        grid=(bsz, nsteps),
        in_specs=specs(fwd_blk) + specs(bwd_blk),
        out_specs=[pl.BlockSpec((rc, GLA_V_W), lambda b, n: (fwd_blk(b, n), 0)),
                   pl.BlockSpec((rc, GLA_V_W), lambda b, n: (bwd_blk(b, n), 0))],
        out_shape=[oshape, oshape],
        scratch_shapes=[pltpu.VMEM((GLA_V_W, GLA_QK_W), F32)] * 2,
        compiler_params=_cparams(("parallel", "arbitrary")),
        name="gla",
    )(h, h, h, h, upf_pad, bias_f, h, h, h, h, upb_pad, bias_b)


DILATIONS = tuple(d for _, d in DIL_PATTERNS)


def _qk_prep_kernel(q_ref, k_ref, v_ref, qg_ref, kg_ref, c_ref, s_ref, *rest):
    nd = len(DILATIONS)
    outs = rest[:3 * nd]
    scr = rest[3 * nd:]
    rc = q_ref.shape[0]
    cos = c_ref[...]
    sin = s_ref[...]
    src = lax.broadcasted_iota(jnp.int32, (HEAD_DIM, HEAD_DIM), 0)
    dst = lax.broadcasted_iota(jnp.int32, (HEAD_DIM, HEAD_DIM), 1)
    hr = ROPE_DIM // 2
    swap = (jnp.where((dst < hr) & (src == dst + hr), -1.0, 0.0)
            + jnp.where((dst >= hr) & (dst < ROPE_DIM) & (src == dst - hr), 1.0, 0.0)).astype(BF16)
    ones = jnp.ones((HEAD_DIM, HEAD_DIM), BF16)

    def dot2(a, b):
        hi, lo = _split_bf16(a)
        return _dot(hi, b) + _dot(lo, b)

    def norm_rope(x, g):
        ms = dot2(x * x, ones) * (1.0 / HEAD_DIM)
        y = x * lax.rsqrt(ms + EPS) * g
        return y * cos + dot2(y, swap) * sin

    for h in range(DIL_HEADS):
        hs = slice(h * HEAD_DIM, (h + 1) * HEAD_DIM)
        scr[0][h] = norm_rope(q_ref[:, hs].astype(F32), qg_ref[...]) * (HEAD_DIM ** -0.5)
        scr[1][h] = norm_rope(k_ref[:, hs].astype(F32), kg_ref[...])
        scr[2][h] = v_ref[:, hs].astype(F32)
    for which in range(3):
        for di, d in enumerate(DILATIONS):
            o_ref = outs[which * nd + di]
            for h in range(DIL_HEADS):
                for r in range(d):
                    rows = pl.ds(r, rc // d, stride=d) if d > 1 else slice(None)
                    c0 = r * DIL_W + h * HEAD_DIM
                    o_ref[0, :, c0:c0 + HEAD_DIM] = scr[which].at[h][rows, :].astype(BF16)


def _qk_prep(h, qg, kg, cos, sin, bsz, seq, rc):
    nsteps = seq // rc
    hblk = lambda c: pl.BlockSpec((rc, DIL_W), lambda b, n: (b * nsteps + n, c // DIL_W))
    tab = pl.BlockSpec((rc, HEAD_DIM), lambda b, n: (n, 0))
    gsp = pl.BlockSpec((1, HEAD_DIM), lambda b, n: (0, 0))
    osp = [pl.BlockSpec((1, rc // d, d * DIL_W), lambda b, n: (b, n, 0)) for d in DILATIONS]
    oshape = [jax.ShapeDtypeStruct((bsz, seq // d, d * DIL_W), BF16) for d in DILATIONS]
    outs = pl.pallas_call(
        _qk_prep_kernel,
        grid=(bsz, nsteps),
        in_specs=[hblk(COL_QD), hblk(COL_KD), hblk(COL_VD), gsp, gsp, tab, tab],
        out_specs=osp * 3,
        out_shape=oshape * 3,
        scratch_shapes=[pltpu.VMEM((DIL_HEADS, rc, HEAD_DIM), F32)] * 3,
        compiler_params=_cparams(("parallel", "parallel")),
        name="qk_prep",
    )(h, h, h, qg, kg, cos, sin)
    nd = len(DILATIONS)
    return outs[:nd], outs[nd:2 * nd], outs[2 * nd:]


def _dilated_kernel(q_ref, km_ref, kl_ref, kr_ref, vm_ref, vl_ref, vr_ref, o_ref, lse_ref, *, half):
    n = pl.program_id(2)
    last = pl.num_programs(2) - 1
    tq = q_ref.shape[1]
    qb = 2 * half
    nsb = tq // qb
    r_i = lax.broadcasted_iota(jnp.int32, (qb, 2 * qb), 0)
    c_i = lax.broadcasted_iota(jnp.int32, (qb, 2 * qb), 1)
    band = (c_i >= r_i) & (c_i <= r_i + 2 * half)
    valid = []
    for sb in range(nsb):
        v = band
        if sb == 0:
            v = v & ((c_i >= half) | (n > 0))
        if sb == nsb - 1:
            v = v & ((c_i < 2 * qb - half) | (n < last))
        valid.append(v)
    lane = lax.broadcasted_iota(jnp.int32, (qb, LANES), 1)
    lse_tiles = [jnp.zeros((qb, LANES), F32) for _ in range(nsb)]
    for h in range(DIL_HEADS):
        hs = slice(h * HEAD_DIM, (h + 1) * HEAD_DIM)
        k_h = jnp.concatenate([kl_ref[0, :, hs], km_ref[0, :, hs], kr_ref[0, :, hs]], axis=0)
        v_h = jnp.concatenate([vl_ref[0, :, hs], vm_ref[0, :, hs], vr_ref[0, :, hs]], axis=0)
        for sb in range(nsb):
            q = q_ref[0, sb * qb:(sb + 1) * qb, hs]
            s = jnp.where(valid[sb], _dot_nt(q, k_h[sb * qb:sb * qb + 2 * qb]), MASK_VALUE)
            m = jnp.max(jnp.maximum(s[:, :qb], s[:, qb:]), axis=-1, keepdims=True)
            p = jnp.exp(s - m)
            den = jnp.sum(p[:, :qb] + p[:, qb:], axis=-1, keepdims=True)
            o = _dot(p.astype(BF16), v_h[sb * qb:sb * qb + 2 * qb])
            o_ref[0, sb * qb:(sb + 1) * qb, hs] = (o / den).astype(o_ref.dtype)
            lse_tiles[sb] = jnp.where(lane == h, m + jnp.log(den), lse_tiles[sb])
    for sb in range(nsb):
        lse_ref[0, sb * qb:(sb + 1) * qb, :] = lse_tiles[sb]


def _dilated(qv, kv, vv, window, dilation, tq):
    bsz, ln, _ = qv.shape
    half = window // (2 * dilation)
    nt = ln // tq
    per = tq // half
    nh = ln // half
    main = pl.BlockSpec((1, tq, DIL_W), lambda b, r, n: (b, n, r))
    left = pl.BlockSpec((1, half, DIL_W), lambda b, r, n: (b, jnp.maximum(n * per - 1, 0), r))
    right = pl.BlockSpec((1, half, DIL_W), lambda b, r, n: (b, jnp.minimum((n + 1) * per, nh - 1), r))
    return pl.pallas_call(
        functools.partial(_dilated_kernel, half=half),
        grid=(bsz, dilation, nt),
        in_specs=[main, main, left, right, main, left, right],
        out_specs=[main, pl.BlockSpec((1, tq, LANES), lambda b, r, n: (b, n, r))],
        out_shape=[jax.ShapeDtypeStruct((bsz, ln, dilation * DIL_W), BF16),
                   jax.ShapeDtypeStruct((bsz, ln, dilation * LANES), F32)],
        compiler_params=_cparams(("parallel", "parallel", "parallel")),
        name=f"dilated_d{dilation}",
    )(qv, kv, kv, kv, vv, vv, vv)


def _combine_kernel(of_ref, ob_ref, r_ref, og_ref, o1_ref, o2_ref, o3_ref, l1_ref, l2_ref, l3_ref,
                    out_ref, on_ref, ln_ref):
    for h in range(GLA_HEADS):
        vs = slice(h * GLA_DV, (h + 1) * GLA_DV)
        o = of_ref[:, vs] + ob_ref[:, vs]
        y = o * lax.rsqrt(jnp.mean(o * o, axis=-1, keepdims=True) + EPS) * og_ref[:, vs]
        rr = r_ref[:, vs].astype(F32)
        out_ref[:, vs] = (y * (rr * jax.nn.sigmoid(rr))).astype(BF16)
    rc = of_ref.shape[0]
    for p, (d, o_ref, l_ref) in enumerate(zip(DILATIONS, (o1_ref, o2_ref, o3_ref), (l1_ref, l2_ref, l3_ref))):
        for r in range(d):
            rows = pl.ds(r, rc // d, stride=d) if d > 1 else slice(None)
            ln_ref.at[p][rows, :] = l_ref[0, :, r * LANES:(r + 1) * LANES]
            for h in range(DIL_HEADS):
                c0 = r * DIL_W + h * HEAD_DIM
                on_ref.at[p * DIL_HEADS + h][rows, :] = o_ref[0, :, c0:c0 + HEAD_DIM].astype(F32)
    l1 = ln_ref[0]
    l2 = ln_ref[1]
    l3 = ln_ref[2]
    mx = jnp.maximum(l1, jnp.maximum(l2, l3))
    e1 = jnp.exp(l1 - mx)
    e2 = jnp.exp(l2 - mx)
    e3 = jnp.exp(l3 - mx)
    inv = 1.0 / (e1 + e2 + e3)
    w1 = e1 * inv
    w2 = e2 * inv
    w3 = e3 * inv
    rows = rc
    for h in range(DIL_HEADS):
        hs = slice(h * HEAD_DIM, (h + 1) * HEAD_DIM)
        bc = lambda w: jnp.broadcast_to(w[:, h:h + 1], (rows, HEAD_DIM))
        o = bc(w1) * on_ref[h] + bc(w2) * on_ref[DIL_HEADS + h] + bc(w3) * on_ref[2 * DIL_HEADS + h]
        out_ref[:, GLA_V_W + h * HEAD_DIM:GLA_V_W + (h + 1) * HEAD_DIM] = o.astype(BF16)


def _combine(o_fwd, o_bwd, h, out_g, os, ls, bsz, seq, rc):
    t = o_fwd.shape[0]
    nsteps = seq // rc
    rb = lambda w, c=0: pl.BlockSpec((rc, w), lambda b, n: (b * nsteps + n, c))
    pat = lambda w: [pl.BlockSpec((1, rc // d, d * w), lambda b, n: (b, n, 0)) for d in DILATIONS]
    return pl.pallas_call(
        _combine_kernel,
        grid=(bsz, nsteps),
        in_specs=[rb(GLA_V_W), rb(GLA_V_W), rb(GLA_V_W, COL_RA // GLA_V_W),
                  pl.BlockSpec((1, GLA_V_W), lambda b, n: (0, 0))] + pat(DIL_W) + pat(LANES),
        out_specs=rb(MIX_W),
        out_shape=jax.ShapeDtypeStruct((t, MIX_W), BF16),
        scratch_shapes=[pltpu.VMEM((len(DILATIONS) * DIL_HEADS, rc, HEAD_DIM), F32),
                        pltpu.VMEM((len(DILATIONS), rc, LANES), F32)],
        compiler_params=_cparams(("parallel", "parallel")),
        name="combine",
    )(o_fwd, o_bwd, h, out_g, *os, *ls)


def _outproj_kernel(mix_ref, w_ref, x_ref, g_ref, x2_ref, xnt_ref):
    x2 = x_ref[...] + _dot(mix_ref[...], w_ref[...])
    x2_ref[...] = x2
    xn = x2 * lax.rsqrt(jnp.mean(x2 * x2, axis=-1, keepdims=True) + EPS) * g_ref[...]
    xnt_ref[...] = xn.T.astype(BF16)


def _outproj(mixed, w_out, x, g2, tm):
    t, d = x.shape
    return pl.pallas_call(
        _outproj_kernel,
        grid=(t // tm,),
        in_specs=[
            pl.BlockSpec((tm, MIX_W), lambda i: (i, 0)),
            pl.BlockSpec((MIX_W, d), lambda i: (0, 0)),
            pl.BlockSpec((tm, d), lambda i: (i, 0)),
            pl.BlockSpec((1, d), lambda i: (0, 0)),
        ],
        out_specs=[pl.BlockSpec((tm, d), lambda i: (i, 0)), pl.BlockSpec((d, tm), lambda i: (0, i))],
        out_shape=[jax.ShapeDtypeStruct((t, d), F32), jax.ShapeDtypeStruct((d, t), BF16)],
        compiler_params=_cparams(("parallel",)),
        name="outproj",
    )(mixed, w_out, x, g2)


_NTOP = PEER_TOPK + 1
_CAND = [(a, b) for a in range(_NTOP) for b in range(_NTOP) if (a + 1) * (b + 1) <= _NTOP]
_NCAND_PAD = 64


def _merge_sort_network(n):
    pairs = []
    p = 1
    while p < n:
        k = p
        while k >= 1:
            for j in range(k % p, n - k, 2 * k):
                for i in range(min(k, n - j - k)):
                    if (i + j) // (2 * p) == (i + j + k) // (2 * p):
                        pairs.append((i + j, i + j + k))
            k //= 2
        p *= 2
    return pairs


def _top_values(sc, count):
    rows = sc.shape[0]
    ntile = rows // SUBLANES
    tiles = [sc[k * SUBLANES:(k + 1) * SUBLANES, :] for k in range(ntile)]
    for lo, hi in _merge_sort_network(ntile):
        tiles[lo], tiles[hi] = jnp.maximum(tiles[lo], tiles[hi]), jnp.minimum(tiles[lo], tiles[hi])
    vals = []
    for it in range(count):
        m = jnp.max(tiles[0], axis=0, keepdims=True)
        vals.append(m)
        hit = tiles[0] >= m
        for k in range(min(ntile, count - it - 1)):
            below = tiles[k + 1] if k + 1 < ntile else jnp.full_like(tiles[k], -jnp.inf)
            tiles[k] = jnp.where(hit, below, tiles[k])
    return vals


def _peer_route_kernel(wq_ref, xnt_ref, sk_ref, tw_ref, v2_ref, q_ref, cand_ref):
    q_ref[...] = _dot(wq_ref[...], xnt_ref[...])
    half = PEER_QDIM // 2
    cand_ref[...] = jnp.full(cand_ref.shape, -jnp.inf, F32)

    def head(h, carry):
        sc_a = _dot3(sk_ref[h, 0], q_ref[pl.ds(pl.multiple_of(h * PEER_QDIM, PEER_QDIM), half), :])
        sc_b = _dot3(sk_ref[h, 1], q_ref[pl.ds(pl.multiple_of(h * PEER_QDIM + half, half), half), :])
        top_a = _top_values(sc_a, _NTOP)
        top_b = _top_values(sc_b, _NTOP)
        for i, (a, b) in enumerate(_CAND):
            cand_ref[i:i + 1, :] = top_a[a] + top_b[b]
        cand = cand_ref[...]
        top_c = _top_values(cand, _NTOP)
        tau = 0.5 * (top_c[PEER_TOPK - 1] + top_c[PEER_TOPK])
        best = top_a[0] + top_b[0]
        zsum = jnp.sum(jnp.where(cand > tau, jnp.exp(cand - best), 0.0), axis=0, keepdims=True)
        tw_ref[h] = jnp.exp((tau - sc_a) - top_b[0])
        tw_ref[PEER_HEADS + h] = jnp.exp(sc_a - top_a[0]) * (1.0 / zsum)
        v2_ref[h] = jnp.exp(sc_b - top_b[0])
        return carry

    lax.fori_loop(0, PEER_HEADS, head, 0)


def _peer_route(wq_t, xnt, sub_keys, tb):
    d, t = xnt.shape
    hq = wq_t.shape[0]
    key_blk = pl.BlockSpec((PEER_HEADS, PEER_NKEYS, tb), lambda i: (0, 0, i))
    key_shape = jax.ShapeDtypeStruct((PEER_HEADS, PEER_NKEYS, t), F32)
    return pl.pallas_call(
        _peer_route_kernel,
        grid=(t // tb,),
        in_specs=[
            pl.BlockSpec((hq, d), lambda i: (0, 0)),
            pl.BlockSpec((d, tb), lambda i: (0, i)),
            pl.BlockSpec(sub_keys.shape, lambda i: (0, 0, 0, 0)),
        ],
        out_specs=[pl.BlockSpec((2 * PEER_HEADS, PEER_NKEYS, tb), lambda i: (0, 0, i)), key_blk],
        out_shape=[jax.ShapeDtypeStruct((2 * PEER_HEADS, PEER_NKEYS, t), F32), key_shape],
        scratch_shapes=[pltpu.VMEM((hq, tb), F32), pltpu.VMEM((_NCAND_PAD, tb), F32)],
        compiler_params=_cparams(("parallel",)),
        name="peer_route",
    )(wq_t, xnt, sub_keys)


def _peer_expert_kernel(xnt_ref, down_ref, upt_ref, tw_ref, v2_ref, x2_ref,
                        o_ref, acc_ref, a_ref, g_ref, h_ref):
    e = pl.program_id(1)
    eb, tb = a_ref.shape
    rows_i = eb // PEER_NKEYS

    @pl.when(e == 0)
    def _():
        acc_ref[...] = jnp.zeros_like(acc_ref)

    chunk_i = 4
    for ii in range(rows_i):
        if ii % chunk_i == 0:
            cs = slice(ii * PEER_NKEYS, (ii + chunk_i) * PEER_NKEYS)
            a_ref[cs, :] = _dot(down_ref[cs, :], xnt_ref[...])
        for lg in range(tb // LANES):
            ls = slice(lg * LANES, (lg + 1) * LANES)
            thr = [jnp.broadcast_to(tw_ref[h, ii:ii + 1, ls], (SUBLANES, LANES)) for h in range(PEER_HEADS)]
            w1 = [jnp.broadcast_to(tw_ref[PEER_HEADS + h, ii:ii + 1, ls], (SUBLANES, LANES))
                  for h in range(PEER_HEADS)]
            for j0 in range(0, PEER_NKEYS, SUBLANES):
                js = slice(j0, j0 + SUBLANES)
                gate = jnp.zeros((SUBLANES, LANES), F32)
                for h in range(PEER_HEADS):
                    v2 = v2_ref[h, js, ls]
                    gate = gate + jnp.where(v2 >= thr[h], w1[h] * v2, 0.0)
                g_ref[ii * PEER_NKEYS + j0:ii * PEER_NKEYS + j0 + SUBLANES, ls] = gate

    rows = 2 * SUBLANES
    for r0 in range(0, eb, rows):
        h_ref[r0:r0 + rows, :] = (g_ref[r0:r0 + rows, :] * jax.nn.gelu(a_ref[r0:r0 + rows, :])).astype(BF16)
    mrows = 1024
    for m0 in range(0, acc_ref.shape[0], mrows):
        acc_ref[m0:m0 + mrows, :] += _dot(upt_ref[0, m0:m0 + mrows, :], h_ref[...])

    @pl.when(e == pl.num_programs(1) - 1)
    def _():
        o_ref[...] = x2_ref[...] + acc_ref[...].T


def _peer_experts(xnt, down, up_t, tw, v2, x2, tb, eb):
    d, t = xnt.shape
    ne = down.shape[0]
    rows_i = eb // PEER_NKEYS
    row_blk = pl.BlockSpec((2 * PEER_HEADS, rows_i, tb), lambda i, e: (0, e, i))
    key_blk = pl.BlockSpec((PEER_HEADS, PEER_NKEYS, tb), lambda i, e: (0, 0, i))
    return pl.pallas_call(
        _peer_expert_kernel,
        grid=(t // tb, ne // eb),
        in_specs=[
            pl.BlockSpec((d, tb), lambda i, e: (0, i)),
            pl.BlockSpec((eb, d), lambda i, e: (e, 0)),
            pl.BlockSpec((1, d, eb), lambda i, e: (e, 0, 0)),
            row_blk, key_blk,
            pl.BlockSpec((tb, d), lambda i, e: (i, 0)),
        ],
        out_specs=pl.BlockSpec((tb, d), lambda i, e: (i, 0)),
        out_shape=jax.ShapeDtypeStruct((t, d), F32),
        scratch_shapes=[pltpu.VMEM((d, tb), F32), pltpu.VMEM((eb, tb), F32), pltpu.VMEM((eb, tb), F32),
                        pltpu.VMEM((eb, tb), BF16)],
        compiler_params=_cparams(("parallel", "arbitrary")),
        name="peer_experts",
    )(xnt, down, up_t, tw, v2, x2)


def _reorder_in_proj(w_in):
    qa, ka, va, gdf, gdb, ra, qd, kd, vd = jnp.split(w_in, [int(c) for c in np.cumsum(IN_SPLITS)[:-1]], axis=1)
    pad = jnp.zeros((w_in.shape[0], IN_W_PAD - COL_GD - 2 * GLA_GATE_RANK), w_in.dtype)
    return jnp.concatenate([qd, kd, vd, qa, ka, va, ra, gdf, gdb, pad], axis=1).astype(BF16)


def _rope_tables(seq):
    half = ROPE_DIM // 2
    inv_freq = ROPE_THETA ** (-jnp.arange(half, dtype=F32) / half)
    ang = jnp.arange(seq, dtype=jnp.int32).astype(F32)[:, None] * inv_freq[None, :]
    cos = jnp.cos(ang)
    sin = jnp.sin(ang)
    rest = HEAD_DIM - ROPE_DIM
    c = jnp.concatenate([cos, cos, jnp.ones((seq, rest), F32)], axis=1)
    s = jnp.concatenate([sin, sin, jnp.zeros((seq, rest), F32)], axis=1)
    return c, s


def _tiles(bsz, seq):
    t = bsz * seq
    max_dil = max(DILATIONS)
    return dict(
        inproj_rows=min(1024, t),
        gla_rows=min(256, seq),
        prep_rows=min(16 * max_dil, seq),
        attn_rows=min(512, seq // max_dil),
        combine_rows=min(512, seq),
        outproj_rows=min(512, t),
        peer_tokens=min(2 * MXU_COLS, t),
        peer_experts=8 * PEER_NKEYS,
    )


def _layer(x, norm1_g, w_in, up_f, bias_f, up_b, bias_b, out_g, q_norm_g, k_norm_g, w_out, norm2_g,
           w_query, sub_keys, down, up):
    bsz, seq, d = x.shape
    t = bsz * seq
    xf = x.reshape(t, d)
    tl = _tiles(bsz, seq)

    h = _rms_matmul(xf, norm1_g.reshape(1, d), _reorder_in_proj(w_in), tm=tl["inproj_rows"], tn=IN_TN)

    zpad = jnp.zeros((LANES, GLA_QK_W), F32)
    upf_pad = zpad.at[:GLA_GATE_RANK].set(up_f)
    upb_pad = zpad.at[GLA_GATE_RANK:2 * GLA_GATE_RANK].set(up_b)
    o_fwd, o_bwd = _gla(h, upf_pad, bias_f.reshape(1, -1), upb_pad, bias_b.reshape(1, -1), bsz, seq,
                        tl["gla_rows"])

    cos, sin = _rope_tables(seq)
    qs, ks, vs = _qk_prep(h, q_norm_g.reshape(1, -1), k_norm_g.reshape(1, -1), cos, sin, bsz, seq,
                          rc=tl["prep_rows"])
    outs, lses = [], []
    for p, (window, dilation) in enumerate(DIL_PATTERNS):
        o_p, lse_p = _dilated(qs[p], ks[p], vs[p], window, dilation, tq=min(tl["attn_rows"], seq // dilation))
        outs.append(o_p)
        lses.append(lse_p)
    mixed = _combine(o_fwd, o_bwd, h, out_g.reshape(1, -1), outs, lses, bsz, seq, rc=tl["combine_rows"])

    x2, xnt = _outproj(mixed, w_out.astype(BF16), xf, norm2_g.reshape(1, d), tm=tl["outproj_rows"])

    tw, v2 = _peer_route(w_query.T.astype(BF16), xnt, sub_keys, tb=tl["peer_tokens"])
    eb = tl["peer_experts"]
    up_blocks = up.reshape(-1, eb, d).transpose(0, 2, 1).astype(BF16)
    out = _peer_experts(xnt, down.astype(BF16), up_blocks, tw, v2, x2, tb=tl["peer_tokens"], eb=eb)
    return out.reshape(bsz, seq, d)


def kernel(x, norm1_g, w_in, gla_up_f, gla_bias_f, gla_up_b, gla_bias_b, gla_out_g, q_norm_g, k_norm_g, w_out,
           norm2_g, peer_w_query, peer_sub_keys, peer_down, peer_up):
    for l in range(norm1_g.shape[0]):
        x = _layer(x, norm1_g[l], w_in[l], gla_up_f[l], gla_bias_f[l], gla_up_b[l], gla_bias_b[l], gla_out_g[l],
                   q_norm_g[l], k_norm_g[l], w_out[l], norm2_g[l], peer_w_query[l], peer_sub_keys[l],
                   peer_down[l], peer_up[l])
    return x
```
